```python
import jax, jax.numpy as jnp
from jax import lax
import numpy as np

D_MODEL = 1024
BATCH = 2
SEQ = 8192
DEPTH = 4

HEAD_DIM = 64
N_SB_HEADS = 8
N_FOX_HEADS = 8
D_SB = N_SB_HEADS * HEAD_DIM
D_FOX = N_FOX_HEADS * HEAD_DIM
D_MIX = D_SB + D_FOX
D_IN = 3 * D_SB + 3 * D_FOX + D_MIX + N_FOX_HEADS
Q_BLOCK = 128
EPS = 1e-6

kernel_name = "hybrid_stickbreaking_fox_adaln"


def rmsnorm(x, g):
    x32 = x.astype(jnp.float32)
    r = lax.rsqrt(jnp.mean(x32 * x32, axis=-1, keepdims=True) + EPS)
    return (x32 * r).astype(x.dtype) * g


def to_heads(t, n_heads):
    b, s, _ = t.shape
    return t.reshape(b, s, n_heads, HEAD_DIM).transpose(0, 2, 1, 3)


def from_heads(t):
    b, h, s, d = t.shape
    return t.transpose(0, 2, 1, 3).reshape(b, s, h * d)


def to_query_blocks(t):
    b, h, s = t.shape[:3]
    nb = s // Q_BLOCK
    t = t.reshape((b, h, nb, Q_BLOCK) + t.shape[3:])
    return jnp.moveaxis(t, 2, 0)


def from_query_blocks(t):
    nb, b, h, qb, d = t.shape
    return jnp.moveaxis(t, 0, 2).reshape(b, h, nb * qb, d)


def stick_breaking_attention(q, k, v):
    s_len = q.shape[2]
    scale = HEAD_DIM ** -0.5
    key_pos = jnp.arange(s_len)

    def block(args):
        qb, i = args
        q_pos = i * Q_BLOCK + jnp.arange(Q_BLOCK)
        z = jnp.einsum('bhqd,bhkd->bhqk', qb, k).astype(jnp.float32) * scale
        causal = key_pos[None, :] < q_pos[:, None]
        log_rem = jnp.where(causal, jax.nn.log_sigmoid(-z), 0.0)
        after = lax.cumsum(log_rem, axis=3, reverse=True) - log_rem
        a = jnp.where(causal, jnp.exp(jax.nn.log_sigmoid(z) + after), 0.0)
        return jnp.einsum('bhqk,bhkd->bhqd', a.astype(v.dtype), v)

    out = lax.map(block, (to_query_blocks(q), jnp.arange(s_len // Q_BLOCK)))
    return from_query_blocks(out)


def forgetting_attention(q, k, v, log_f):
    s_len = q.shape[2]
    scale = HEAD_DIM ** -0.5
    key_pos = jnp.arange(s_len)
    cum = lax.cumsum(log_f, axis=2)

    def block(args):
        qb, cum_q, i = args
        q_pos = i * Q_BLOCK + jnp.arange(Q_BLOCK)
        logits = jnp.einsum('bhqd,bhkd->bhqk', qb, k).astype(jnp.float32) * scale
        logits = logits + cum_q[..., :, None] - cum[..., None, :]
        causal = key_pos[None, :] <= q_pos[:, None]
        p = jax.nn.softmax(jnp.where(causal, logits, -jnp.inf), axis=-1)
        return jnp.einsum('bhqk,bhkd->bhqd', p.astype(v.dtype), v)

    out = lax.map(block, (to_query_blocks(q), to_query_blocks(cum), jnp.arange(s_len // Q_BLOCK)))
    return from_query_blocks(out)


def setup_inputs(seed: int = 0) -> dict:
    key = jax.random.key(seed)
    ks = jax.random.split(key, 12)
    x = jax.random.normal(ks[0], (BATCH, SEQ, D_MODEL), jnp.float32)
    c = jax.random.normal(ks[1], (BATCH, D_MODEL), jnp.float32)
    w_ada = jax.random.normal(ks[2], (DEPTH, D_MODEL, 3 * D_MODEL), jnp.float32) * (0.1 * D_MODEL ** -0.5)
    b_ada = jax.random.normal(ks[3], (DEPTH, 3 * D_MODEL), jnp.float32) * 0.02
    g_norm = 1.0 + 0.02 * jax.random.normal(ks[4], (DEPTH, D_MODEL), jnp.float32)
    w_in = jax.random.normal(ks[5], (DEPTH, D_MODEL, D_IN), jnp.float32) * (D_MODEL ** -0.5)
    b_f = (jnp.linspace(1.0, 6.0, N_FOX_HEADS, dtype=jnp.float32)[None, :]
           + 0.1 * jax.random.normal(ks[6], (DEPTH, N_FOX_HEADS), jnp.float32))
    g_grp = 1.0 + 0.02 * jax.random.normal(ks[7], (DEPTH, D_MIX), jnp.float32)
    w_out = jax.random.normal(ks[8], (DEPTH, D_MIX, D_MODEL), jnp.float32) * (D_MIX ** -0.5)
    g_final = 1.0 + 0.02 * jax.random.normal(ks[9], (D_MODEL,), jnp.float32)
    return {"x": x, "c": c, "w_ada": w_ada, "b_ada": b_ada, "g_norm": g_norm,
            "w_in": w_in, "b_f": b_f, "g_grp": g_grp, "w_out": w_out, "g_final": g_final}


def reference(x, c, w_ada, b_ada, g_norm, w_in, b_f, g_grp, w_out, g_final):
    split_points = list(np.cumsum([D_SB, D_SB, D_SB, D_FOX, D_FOX, D_FOX, D_MIX]))
    c_act = jax.nn.silu(c)
    for layer in range(DEPTH):
        ada = c_act @ w_ada[layer] + b_ada[layer]
        shift, scale, gate = jnp.split(ada, 3, axis=-1)
        h = rmsnorm(x, g_norm[layer]) * (1.0 + scale[:, None, :]) + shift[:, None, :]

        z = h @ w_in[layer]
        q_sb, k_sb, v_sb, q_fx, k_fx, v_fx, g_path, f_logit = jnp.split(z, split_points, axis=-1)

        o_sb = from_heads(stick_breaking_attention(
            to_heads(q_sb, N_SB_HEADS), to_heads(k_sb, N_SB_HEADS), to_heads(v_sb, N_SB_HEADS)))

        log_f = jax.nn.log_sigmoid((f_logit + b_f[layer]).astype(jnp.float32)).transpose(0, 2, 1)
        o_fx = from_heads(forgetting_attention(
            to_heads(q_fx, N_FOX_HEADS), to_heads(k_fx, N_FOX_HEADS), to_heads(v_fx, N_FOX_HEADS), log_f))

        y = jnp.concatenate([rmsnorm(o_sb, g_grp[layer, :D_SB]),
                             rmsnorm(o_fx, g_grp[layer, D_SB:])], axis=-1)
        y = y * jax.nn.silu(g_path)
        x = x + (1.0 + gate[:, None, :]) * (y @ w_out[layer])
    return rmsnorm(x, g_final)
```

```python
import functools

import numpy as np
import jax
import jax.numpy as jnp
from jax import lax
from jax.experimental import pallas as pl
from jax.experimental.pallas import tpu as pltpu

F32 = jnp.float32
BF16 = jnp.bfloat16

HEAD_DIM = 64
N_HEADS = 8
D_GRP = N_HEADS * HEAD_DIM
LANES = 128
N_PAIRS = D_GRP // LANES
AUG_STRIDE = 16
N_SPLIT = 3
EPS = 1e-6
Q_SCALE = HEAD_DIM ** -0.5
VMEM_LIMIT = 52 * 1024 * 1024


def _split_bf16(x, n):
    parts = []
    r = x
    for i in range(n):
        p = r.astype(BF16)
        parts.append(p)
        if i + 1 < n:
            r = r - p.astype(F32)
    return parts


def _dot(a, b):
    return jnp.dot(a, b, preferred_element_type=F32)


def _dot_nt(a, b):
    return lax.dot_general(a, b, (((1,), (1,)), ((), ())), preferred_element_type=F32)


def _softplus(z):
    return jnp.maximum(z, 0.0) + jnp.log1p(jnp.exp(-jnp.abs(z)))


def _ada_kernel(c_ref, w_ref, b_ref, o_ref):
    c = c_ref[...]
    c_act = c / (1.0 + jnp.exp(-c))
    w_parts = _split_bf16(w_ref[0], 2)
    acc = jnp.zeros(o_ref.shape[1:], F32)
    for cp in _split_bf16(c_act, N_SPLIT):
        for wp in w_parts:
            acc = acc + _dot(cp, wp)
    o_ref[0] = acc + b_ref[0]


def _adaln(c, w_ada, b_ada):
    depth, d, d3 = w_ada.shape
    b = c.shape[0]
    rows = 8
    tn = 1024
    c_pad = jnp.zeros((rows, d), F32).at[:b].set(c)
    out = pl.pallas_call(
        _ada_kernel,
        grid=(depth, d3 // tn),
        in_specs=[
            pl.BlockSpec((rows, d), lambda l, n: (0, 0)),
            pl.BlockSpec((1, d, tn), lambda l, n: (l, 0, n)),
            pl.BlockSpec((1, 1, tn), lambda l, n: (l, 0, n)),
        ],
        out_specs=pl.BlockSpec((1, rows, tn), lambda l, n: (l, 0, n)),
        out_shape=jax.ShapeDtypeStruct((depth, rows, d3), F32),
        compiler_params=pltpu.CompilerParams(
            dimension_semantics=("parallel", "parallel"), vmem_limit_bytes=VMEM_LIMIT),
        name="adaln",
    )(c_pad, w_ada, b_ada.reshape(depth, 1, d3))
    return out[:, :b]


def _proj_kernel(x_ref, ada_ref, gn_ref, wqkv_ref, wg_ref, wf_ref, bf_ref, pq_ref, pk_ref,
                 oq_ref, ok_ref,
                 qsb_ref, ksb_ref, vsb_ref, qfx_ref, kfx_ref, vfx_ref, gate_ref, qaug_ref, kaug_ref,
                 carry_ref, *, d_model):
    ts = x_ref.shape[1]
    x = x_ref[0]
    ada = ada_ref[0]
    shift = ada[:, :d_model]
    scale = ada[:, d_model:2 * d_model]
    r = lax.rsqrt(jnp.mean(x * x, axis=-1, keepdims=True) + EPS)
    h = ((x * r) * gn_ref[...] * (1.0 + scale) + shift).astype(BF16)

    outs = (qsb_ref, ksb_ref, vsb_ref, qfx_ref, kfx_ref, vfx_ref)
    for i, o_ref in enumerate(outs):
        o_ref[0] = _dot(h, wqkv_ref[:, i * D_GRP:(i + 1) * D_GRP]).astype(o_ref.dtype)
    gate_ref[0] = _dot(h, wg_ref[...])

    zf = _dot(h, wf_ref[...]) + bf_ref[...]
    log_f = -_softplus(-zf)

    @pl.when(pl.program_id(1) == 0)
    def _():
        carry_ref[...] = jnp.zeros_like(carry_ref)

    row = lax.broadcasted_iota(jnp.int32, (ts, ts), 0)
    col = lax.broadcasted_iota(jnp.int32, (ts, ts), 1)
    lower = jnp.where(col <= row, 1.0, 0.0).astype(BF16)
    cum = carry_ref[...]
    for part in _split_bf16(log_f, N_SPLIT):
        cum = cum + _dot(lower, part)
    carry_ref[...] = cum[ts - 1:ts, :]

    pieces = jnp.concatenate(_split_bf16(cum, N_SPLIT), axis=1)
    qaug_ref[0] = (_dot(pieces, pq_ref[...]) + oq_ref[...]).astype(BF16)
    kaug_ref[0] = (_dot(pieces, pk_ref[...]) + ok_ref[...]).astype(BF16)


def _aug_constants():
    pq = np.zeros((N_SPLIT * LANES, D_GRP), np.float32)
    pk = np.zeros((N_SPLIT * LANES, D_GRP), np.float32)
    oq = np.zeros((1, D_GRP), np.float32)
    ok = np.zeros((1, D_GRP), np.float32)
    for h in range(N_HEADS):
        base = (h // 2) * LANES + AUG_STRIDE * (h % 2)
        for j in range(N_SPLIT):
            pq[j * LANES + h, base + j] = 1.0
            ok[0, base + j] = 1.0
            pk[j * LANES + h, base + N_SPLIT + j] = -1.0
            oq[0, base + N_SPLIT + j] = 1.0
    return (jnp.asarray(pq, BF16), jnp.asarray(pk, BF16), jnp.asarray(oq), jnp.asarray(ok))


def _project(x, ada_l, g_norm_l, wqkv, wg, wf, bf, consts, ts):
    b, s, d = x.shape
    pq, pk, oq, ok = consts
    const = lambda shape: pl.BlockSpec(shape, lambda i, j: (0,) * len(shape))
    grp = pl.BlockSpec((1, ts, D_GRP), lambda i, j: (i, j, 0))
    grp_shape = jax.ShapeDtypeStruct((b, s, D_GRP), BF16)
    return pl.pallas_call(
        functools.partial(_proj_kernel, d_model=d),
        grid=(b, s // ts),
        in_specs=[
            pl.BlockSpec((1, ts, d), lambda i, j: (i, j, 0)),
            pl.BlockSpec((1, 1, 3 * d), lambda i, j: (i, 0, 0)),
            const((1, d)),
            const(wqkv.shape), const(wg.shape), const(wf.shape), const(bf.shape),
            const(pq.shape), const(pk.shape), const(oq.shape), const(ok.shape),
        ],
        out_specs=[grp] * 6 + [pl.BlockSpec((1, ts, 2 * D_GRP), lambda i, j: (i, j, 0)), grp, grp],
        out_shape=[grp_shape] * 6 + [jax.ShapeDtypeStruct((b, s, 2 * D_GRP), F32), grp_shape, grp_shape],
        scratch_shapes=[pltpu.VMEM((1, LANES), F32)],
        compiler_params=pltpu.CompilerParams(
            dimension_semantics=("parallel", "arbitrary"), vmem_limit_bytes=VMEM_LIMIT),
        name="project",
    )(x, ada_l.reshape(b, 1, 3 * d), g_norm_l.reshape(1, d), wqkv, wg, wf, bf, pq, pk, oq, ok)


def _sb_kernel(q_ref, k_ref, v_ref, o_ref):
    tq = q_ref.shape[1]
    tk = tq
    qt = pl.program_id(2)
    q_pair = q_ref[0]
    lane = lax.broadcasted_iota(jnp.int32, (tq, LANES), 1)
    row = lax.broadcasted_iota(jnp.int32, (tq, tk), 0)
    col = lax.broadcasted_iota(jnp.int32, (tq, tk), 1)
    causal = col < row
    later = jnp.where(row > col, 1.0, 0.0).astype(BF16)

    def tile(kt, carry, acc, q_head, masked):
        start = pl.multiple_of(kt * tk, tk)
        k = k_ref[0, pl.ds(start, tk), :]
        v = v_ref[0, pl.ds(start, tk), :]
        z = _dot_nt(q_head, k)
        log_rem = -_softplus(z)
        if masked:
            log_rem = jnp.where(causal, log_rem, 0.0)
        hi, lo = _split_bf16(log_rem, 2)
        after = _dot(hi, later) + _dot(lo, later) + carry
        a = jnp.exp(z - _softplus(z) + after)
        if masked:
            a = jnp.where(causal, a, 0.0)
        acc = acc + _dot(a.astype(BF16), v)
        carry = carry + jnp.sum(log_rem, axis=1, keepdims=True)
        return carry, acc

    heads = []
    for j in range(2):
        in_head = (lane >= j * HEAD_DIM) & (lane < (j + 1) * HEAD_DIM)
        q_head = jnp.where(in_head, q_pair, jnp.zeros_like(q_pair))
        state = tile(qt, jnp.zeros((tq, 1), F32), jnp.zeros((tq, LANES), F32), q_head, True)

        def body(i, st, q_head=q_head):
            return tile(qt - 1 - i, st[0], st[1], q_head, False)

        _, acc = lax.fori_loop(0, qt, body, state)
        heads.append(acc)
    o_ref[0] = jnp.where(lane < HEAD_DIM, heads[0], heads[1])


def _sb_attention(q, k, v, tq):
    b, s, _ = q.shape
    return pl.pallas_call(
        _sb_kernel,
        grid=(b, N_PAIRS, s // tq),
        in_specs=[
            pl.BlockSpec((1, tq, LANES), lambda i, p, t: (i, t, p)),
            pl.BlockSpec((1, s, LANES), lambda i, p, t: (i, 0, p)),
            pl.BlockSpec((1, s, LANES), lambda i, p, t: (i, 0, p)),
        ],
        out_specs=pl.BlockSpec((1, tq, LANES), lambda i, p, t: (i, t, p)),
        out_shape=jax.ShapeDtypeStruct((b, s, D_GRP), F32),
        compiler_params=pltpu.CompilerParams(
            dimension_semantics=("parallel", "parallel", "arbitrary"), vmem_limit_bytes=VMEM_LIMIT),
        name="sb_attention",
    )(q, k, v)


def _fox_kernel(q_ref, qa_ref, k_ref, ka_ref, v_ref, o_ref):
    tq = q_ref.shape[1]
    tk = tq
    qt = pl.program_id(2)
    q_pair = q_ref[0]
    qa_pair = qa_ref[0]
    lane = lax.broadcasted_iota(jnp.int32, (tq, LANES), 1)
    row = lax.broadcasted_iota(jnp.int32, (tq, tk), 0)
    col = lax.broadcasted_iota(jnp.int32, (tq, tk), 1)
    causal = col <= row

    def tile(kt, m, l, acc, q_head, masked):
        start = pl.multiple_of(kt * tk, tk)
        k = jnp.concatenate([k_ref[0, pl.ds(start, tk), :], ka_ref[0, pl.ds(start, tk), :]], axis=1)
        v = v_ref[0, pl.ds(start, tk), :]
        logits = _dot_nt(q_head, k)
        if masked:
            logits = jnp.where(causal, logits, -jnp.inf)
        m_new = jnp.maximum(m, jnp.max(logits, axis=1, keepdims=True))
        p = jnp.exp(logits - m_new)
        alpha = jnp.exp(m - m_new)
        l = alpha * l + jnp.sum(p, axis=1, keepdims=True)
        acc = alpha * acc + _dot(p.astype(BF16), v)
        return m_new, l, acc

    heads = []
    for j in range(2):
        in_head = (lane >= j * HEAD_DIM) & (lane < (j + 1) * HEAD_DIM)
        in_aug = (lane >= j * AUG_STRIDE) & (lane < (j + 1) * AUG_STRIDE)
        q_head = jnp.concatenate([jnp.where(in_head, q_pair, jnp.zeros_like(q_pair)),
                                  jnp.where(in_aug, qa_pair, jnp.zeros_like(qa_pair))], axis=1)
        state = tile(qt, jnp.full((tq, 1), -jnp.inf, F32), jnp.zeros((tq, 1), F32),
                     jnp.zeros((tq, LANES), F32), q_head, True)

        def body(i, st, q_head=q_head):
            return tile(qt - 1 - i, st[0], st[1], st[2], q_head, False)

        _, l, acc = lax.fori_loop(0, qt, body, state)
        heads.append(acc / l)
    o_ref[0] = jnp.where(lane < HEAD_DIM, heads[0], heads[1])


def _fox_attention(q, qaug, k, kaug, v, tq):
    b, s, _ = q.shape
    q_spec = pl.BlockSpec((1, tq, LANES), lambda i, p, t: (i, t, p))
    kv_spec = pl.BlockSpec((1, s, LANES), lambda i, p, t: (i, 0, p))
    return pl.pallas_call(
        _fox_kernel,
        grid=(b, N_PAIRS, s // tq),
        in_specs=[q_spec, q_spec, kv_spec, kv_spec, kv_spec],
        out_specs=q_spec,
        out_shape=jax.ShapeDtypeStruct((b, s, D_GRP), F32),
        compiler_params=pltpu.CompilerParams(
            dimension_semantics=("parallel", "parallel", "arbitrary"), vmem_limit_bytes=VMEM_LIMIT),
        name="fox_attention",
    )(q, qaug, k, kaug, v)


def _rms(x):
    return x * lax.rsqrt(jnp.mean(x * x, axis=-1, keepdims=True) + EPS)


def _out_kernel(osb_ref, ofx_ref, gate_ref, x_ref, gg_ref, wout_ref, ada_ref, gf_ref, o_ref, *,
                d_model, final):
    gg = gg_ref[...]
    y = jnp.concatenate([_rms(osb_ref[0]) * gg[:, :D_GRP], _rms(ofx_ref[0]) * gg[:, D_GRP:]], axis=1)
    g = gate_ref[0]
    y = y * (g / (1.0 + jnp.exp(-g)))
    gate = ada_ref[0][:, 2 * d_model:]
    out = x_ref[0] + (1.0 + gate) * _dot(y.astype(BF16), wout_ref[...])
    if final:
        out = _rms(out) * gf_ref[...]
    o_ref[0] = out


def _output(o_sb, o_fx, gate, x, g_grp_l, wout, ada_l, g_final, ts, final):
    b, s, d = x.shape
    const = lambda shape: pl.BlockSpec(shape, lambda i, j: (0,) * len(shape))
    return pl.pallas_call(
        functools.partial(_out_kernel, d_model=d, final=final),
        grid=(b, s // ts),
        in_specs=[
            pl.BlockSpec((1, ts, D_GRP), lambda i, j: (i, j, 0)),
            pl.BlockSpec((1, ts, D_GRP), lambda i, j: (i, j, 0)),
            pl.BlockSpec((1, ts, 2 * D_GRP), lambda i, j: (i, j, 0)),
            pl.BlockSpec((1, ts, d), lambda i, j: (i, j, 0)),
            const((1, 2 * D_GRP)),
            const(wout.shape),
            pl.BlockSpec((1, 1, 3 * d), lambda i, j: (i, 0, 0)),
            const((1, d)),
        ],
        out_specs=pl.BlockSpec((1, ts, d), lambda i, j: (i, j, 0)),
        out_shape=jax.ShapeDtypeStruct((b, s, d), F32),
        compiler_params=pltpu.CompilerParams(
            dimension_semantics=("parallel", "parallel"), vmem_limit_bytes=VMEM_LIMIT),
        name="output",
    )(o_sb, o_fx, gate, x, g_grp_l.reshape(1, 2 * D_GRP), wout, ada_l.reshape(b, 1, 3 * d),
      g_final.reshape(1, d))


def kernel(x, c, w_ada, b_ada, g_norm, w_in, b_f, g_grp, w_out, g_final):
    b, s, d = x.shape
    depth = w_ada.shape[0]
    ts = min(512, s)
    tq = min(256, s)
    consts = _aug_constants()

    col_scale = np.ones((6 * D_GRP,), np.float32)
    col_scale[0:D_GRP] = Q_SCALE
    col_scale[3 * D_GRP:4 * D_GRP] = Q_SCALE
    wqkv = (w_in[:, :, :6 * D_GRP] * col_scale).astype(BF16)
    wg = w_in[:, :, 6 * D_GRP:8 * D_GRP].astype(BF16)
    wf = jnp.zeros((depth, d, LANES), BF16).at[:, :, :N_HEADS].set(w_in[:, :, 8 * D_GRP:].astype(BF16))
    bf = jnp.zeros((depth, 1, LANES), F32).at[:, 0, :N_HEADS].set(b_f)
    wout = w_out.astype(BF16)

    ada = _adaln(c, w_ada, b_ada)
    for l in range(depth):
        q_sb, k_sb, v_sb, q_fx, k_fx, v_fx, gate, qaug, kaug = _project(
            x, ada[l], g_norm[l], wqkv[l], wg[l], wf[l], bf[l], consts, ts)
        o_sb = _sb_attention(q_sb, k_sb, v_sb, tq)
        o_fx = _fox_attention(q_fx, qaug, k_fx, kaug, v_fx, tq)
        x = _output(o_sb, o_fx, gate, x, g_grp[l], wout[l], ada[l], g_final, ts, l == depth - 1)
    return x
```

```python
import functools

import numpy as np
import jax
import jax.numpy as jnp
from jax import lax
from jax.experimental import pallas as pl
from jax.experimental.pallas import tpu as pltpu

F32 = jnp.float32
BF16 = jnp.bfloat16

HEAD_DIM = 64
N_HEADS = 8
D_GRP = N_HEADS * HEAD_DIM
LANES = 128
SUBLANES = 8
N_PAIRS = D_GRP // LANES
KEY_TILE = 256
KEY_RUN = KEY_TILE // SUBLANES
AUG_STRIDE = 16
N_SPLIT = 3
EPS = 1e-6
Q_SCALE = HEAD_DIM ** -0.5
VMEM_LIMIT = 52 * 1024 * 1024


def _split_bf16(x, n):
    parts = []
    r = x
    for i in range(n):
        p = r.astype(BF16)
        parts.append(p)
        if i + 1 < n:
            r = r - p.astype(F32)
    return parts


def _dot(a, b):
    return jnp.dot(a, b, preferred_element_type=F32)


def _dot_nt(a, b):
    return lax.dot_general(a, b, (((1,), (1,)), ((), ())), preferred_element_type=F32)


def _dot_tn(a, b):
    return lax.dot_general(a, b, (((0,), (0,)), ((), ())), preferred_element_type=F32)


def _softplus(z):
    return jnp.maximum(z, 0.0) + jnp.log1p(jnp.exp(-jnp.abs(z)))


def _key_of_row(p):
    return (p & (SUBLANES - 1)) * KEY_RUN + (p >> 3)


def _ada_kernel(c_ref, w_ref, b_ref, o_ref):
    c = c_ref[...]
    c_act = c / (1.0 + jnp.exp(-c))
    w_parts = _split_bf16(w_ref[0], 2)
    acc = jnp.zeros(o_ref.shape[1:], F32)
    for cp in _split_bf16(c_act, N_SPLIT):
        for wp in w_parts:
            acc = acc + _dot(cp, wp)
    o_ref[0] = acc + b_ref[0]


def _adaln(c, w_ada, b_ada):
    depth, d, d3 = w_ada.shape
    b = c.shape[0]
    rows = SUBLANES
    tn = 1024
    c_pad = jnp.zeros((rows, d), F32).at[:b].set(c)
    out = pl.pallas_call(
        _ada_kernel,
        grid=(depth, d3 // tn),
        in_specs=[
            pl.BlockSpec((rows, d), lambda l, n: (0, 0)),
            pl.BlockSpec((1, d, tn), lambda l, n: (l, 0, n)),
            pl.BlockSpec((1, 1, tn), lambda l, n: (l, 0, n)),
        ],
        out_specs=pl.BlockSpec((1, rows, tn), lambda l, n: (l, 0, n)),
        out_shape=jax.ShapeDtypeStruct((depth, rows, d3), F32),
        compiler_params=pltpu.CompilerParams(
            dimension_semantics=("parallel", "parallel"), vmem_limit_bytes=VMEM_LIMIT),
        name="adaln",
    )(c_pad, w_ada, b_ada.reshape(depth, 1, d3))
    return out[:, :b]


def _proj_kernel(x_ref, ada_ref, gn_ref, wq_ref, wk_ref, wv_ref, wg_ref, wf_ref, bf_ref,
                 pq_ref, pk_ref,
                 qsb_ref, ksb_ref, vsb_ref, qfx_ref, kfx_ref, vfx_ref, gate_ref, qaug_ref, kaug_ref,
                 carry_ref, *, d_model):
    ts = x_ref.shape[1]
    x = x_ref[0]
    ada = ada_ref[0]
    shift = ada[:, :d_model]
    scale = ada[:, d_model:2 * d_model]
    r = lax.rsqrt(jnp.mean(x * x, axis=-1, keepdims=True) + EPS)
    h = ((x * r) * gn_ref[...] * (1.0 + scale) + shift).astype(BF16)

    row = lax.broadcasted_iota(jnp.int32, (ts, ts), 0)
    col = lax.broadcasted_iota(jnp.int32, (ts, ts), 1)
    in_tile = row & (KEY_TILE - 1)
    key_row = (row - in_tile) + _key_of_row(in_tile)
    perm = jnp.where(col == key_row, 1.0, 0.0).astype(BF16)
    h_keys = _dot(perm, h).astype(BF16)

    qsb_ref[0] = _dot_nt(wq_ref[0], h).astype(BF16)
    qfx_ref[0] = _dot_nt(wq_ref[1], h).astype(BF16)
    gate_ref[0] = _dot_nt(wg_ref[...], h)
    ksb_ref[0] = _dot(h_keys, wk_ref[0]).astype(BF16)
    kfx_ref[0] = _dot(h_keys, wk_ref[1]).astype(BF16)
    for o_ref, w in ((vsb_ref, wv_ref[0]), (vfx_ref, wv_ref[1])):
        v_t = _dot_nt(w, h_keys).astype(BF16)
        for t in range(ts // KEY_TILE):
            o_ref[0, t] = v_t[:, t * KEY_TILE:(t + 1) * KEY_TILE]

    zf = _dot(h, wf_ref[...]) + bf_ref[...]
    log_f_parts = _split_bf16(-_softplus(-zf), N_SPLIT)

    @pl.when(pl.program_id(1) == 0)
    def _():
        carry_ref[...] = jnp.zeros_like(carry_ref)

    carry = carry_ref[...]
    upto = jnp.where(col <= row, 1.0, 0.0).astype(BF16)
    upto_keys = jnp.where(col <= key_row, 1.0, 0.0).astype(BF16)
    cum = carry
    cum_keys = carry
    for part in log_f_parts:
        cum = cum + _dot(upto, part)
        cum_keys = cum_keys + _dot(upto_keys, part)
    carry_ref[...] = cum[ts - 1:ts, :]

    ones = jnp.ones((ts, LANES), BF16)
    pieces = jnp.concatenate(_split_bf16(cum, N_SPLIT) + [ones], axis=1)
    pieces_keys = jnp.concatenate(_split_bf16(cum_keys, N_SPLIT) + [ones], axis=1)
    qaug_ref[0] = _dot_nt(pq_ref[...], pieces).astype(BF16)
    kaug_ref[0] = _dot(pieces_keys, pk_ref[...]).astype(BF16)


def _aug_constants():
    pq = np.zeros((D_GRP, (N_SPLIT + 1) * LANES), np.float32)
    pk = np.zeros(((N_SPLIT + 1) * LANES, D_GRP), np.float32)
    one_col = N_SPLIT * LANES
    for h in range(N_HEADS):
        base = (h // 2) * LANES + AUG_STRIDE * (h % 2)
        for j in range(N_SPLIT):
            pq[base + j, j * LANES + h] = 1.0
            pk[one_col, base + j] = 1.0
            pk[j * LANES + h, base + N_SPLIT + j] = -1.0
            pq[base + N_SPLIT + j, one_col] = 1.0
    return jnp.asarray(pq, BF16), jnp.asarray(pk, BF16)


def _project(x, ada_l, g_norm_l, wq, wk, wv, wg, wf, bf, consts, ts):
    b, s, d = x.shape
    pq, pk = consts
    const = lambda shape: pl.BlockSpec(shape, lambda i, j: (0,) * len(shape))
    row_major = pl.BlockSpec((1, ts, D_GRP), lambda i, j: (i, j, 0))
    feat_major = pl.BlockSpec((1, D_GRP, ts), lambda i, j: (i, 0, j))
    v_spec = pl.BlockSpec((1, ts // KEY_TILE, D_GRP, KEY_TILE), lambda i, j: (i, j, 0, 0))
    row_shape = jax.ShapeDtypeStruct((b, s, D_GRP), BF16)
    feat_shape = jax.ShapeDtypeStruct((b, D_GRP, s), BF16)
    v_shape = jax.ShapeDtypeStruct((b, s // KEY_TILE, D_GRP, KEY_TILE), BF16)
    return pl.pallas_call(
        functools.partial(_proj_kernel, d_model=d),
        grid=(b, s // ts),
        in_specs=[
            pl.BlockSpec((1, ts, d), lambda i, j: (i, j, 0)),
            pl.BlockSpec((1, 1, 3 * d), lambda i, j: (i, 0, 0)),
            const((1, d)),
            const(wq.shape), const(wk.shape), const(wv.shape), const(wg.shape), const(wf.shape),
            const(bf.shape), const(pq.shape), const(pk.shape),
        ],
        out_specs=[feat_major, row_major, v_spec, feat_major, row_major, v_spec,
                   pl.BlockSpec((1, 2 * D_GRP, ts), lambda i, j: (i, 0, j)), feat_major, row_major],
        out_shape=[feat_shape, row_shape, v_shape, feat_shape, row_shape, v_shape,
                   jax.ShapeDtypeStruct((b, 2 * D_GRP, s), F32), feat_shape, row_shape],
        scratch_shapes=[pltpu.VMEM((1, LANES), F32)],
        compiler_params=pltpu.CompilerParams(
            dimension_semantics=("parallel", "arbitrary"), vmem_limit_bytes=VMEM_LIMIT),
        name="project",
    )(x, ada_l.reshape(b, 1, 3 * d), g_norm_l.reshape(1, d), wq, wk, wv, wg, wf, bf, pq, pk)


def _tile_iotas(tq):
    shape = (KEY_RUN, SUBLANES, tq)
    run = lax.broadcasted_iota(jnp.int32, shape, 0)
    sub = lax.broadcasted_iota(jnp.int32, shape, 1)
    qry = lax.broadcasted_iota(jnp.int32, shape, 2)
    return run, sub, qry


def _head_rows_mask(shape, j, width):
    row = lax.broadcasted_iota(jnp.int32, shape, 0)
    return (row >= j * width) & (row < (j + 1) * width)


def _shift_up(x, k, fill):
    sub = lax.broadcasted_iota(jnp.int32, x.shape, 0)
    return jnp.where(sub + k < SUBLANES, pltpu.roll(x, SUBLANES - k, axis=0), fill)


def _sb_kernel(q_ref, k_ref, v_ref, o_ref):
    tq = q_ref.shape[2]
    qt = pl.program_id(2)
    q_pair = q_ref[0]
    run, sub, qry = _tile_iotas(tq)
    causal = sub * KEY_RUN + run < qry
    q_heads = [jnp.where(_head_rows_mask(q_pair.shape, j, HEAD_DIM), q_pair, jnp.zeros_like(q_pair))
               for j in range(2)]

    def scores(kt):
        start = pl.multiple_of(kt * KEY_TILE, KEY_TILE)
        k = k_ref[0, pl.ds(start, KEY_TILE), :]
        return tuple(_dot(k, q_heads[j]) for j in range(2))

    def weights(half_z, carry, masked):
        th = jnp.tanh(half_z.reshape(KEY_RUN, SUBLANES, tq))
        rem = 0.5 - 0.5 * th
        beta = 1.0 - rem
        if masked:
            rem = jnp.where(causal, rem, 1.0)
            beta = jnp.where(causal, beta, 0.0)
        prod = jnp.ones((SUBLANES, tq), F32)
        parts = [None] * KEY_RUN
        for r in reversed(range(KEY_RUN)):
            parts[r] = beta[r] * prod
            prod = prod * rem[r]
        incl = prod
        for step in (1, 2, 4):
            incl = incl * _shift_up(incl, step, 1.0)
        scale = _shift_up(incl, 1, 1.0) * carry
        a = (jnp.stack(parts) * scale[None]).reshape(KEY_TILE, tq).astype(BF16)
        return a, carry * jnp.broadcast_to(incl[0:1], carry.shape)

    def accumulate(kt, acc, a):
        return tuple(acc[j] + _dot(v_ref[0, kt, j * HEAD_DIM:(j + 1) * HEAD_DIM, :], a[j])
                     for j in range(2))

    def both(fn, *args):
        out = tuple(fn(*(x[j] for x in args)) for j in range(2))
        return tuple(zip(*out))

    ones = jnp.ones((SUBLANES, tq), F32)
    a, carry = both(lambda z, c: weights(z, c, True), scores(qt), (ones, ones))
    acc = (jnp.zeros((HEAD_DIM, tq), F32),) * 2

    def body(i, st):
        half_z, a, carry, acc = st
        kt = qt - 1 - i
        acc = accumulate(kt + 1, acc, a)
        z_next = scores(jnp.maximum(kt - 1, 0))
        a, carry = both(lambda z, c: weights(z, c, False), half_z, carry)
        return z_next, a, carry, acc

    _, a, _, acc = lax.fori_loop(0, qt, body, (scores(jnp.maximum(qt - 1, 0)), a, carry, acc))
    acc = accumulate(0, acc, a)
    o_ref[0] = jnp.concatenate(acc, axis=0)


def _sb_attention(q_t, k, v_t, tq):
    b, _, s = q_t.shape
    q_spec = pl.BlockSpec((1, LANES, tq), lambda i, p, t: (i, p, t))
    return pl.pallas_call(
        _sb_kernel,
        grid=(b, N_PAIRS, s // tq),
        in_specs=[
            q_spec,
            pl.BlockSpec((1, s, LANES), lambda i, p, t: (i, 0, p)),
            pl.BlockSpec((1, s // KEY_TILE, LANES, KEY_TILE), lambda i, p, t: (i, 0, p, 0)),
        ],
        out_specs=q_spec,
        out_shape=jax.ShapeDtypeStruct((b, D_GRP, s), F32),
        compiler_params=pltpu.CompilerParams(
            dimension_semantics=("parallel", "parallel", "arbitrary"), vmem_limit_bytes=VMEM_LIMIT),
        name="sb_attention",
    )(q_t, k, v_t)


def _fox_kernel(q_ref, qa_ref, k_ref, ka_ref, v_ref, o_ref):
    tq = q_ref.shape[2]
    qt = pl.program_id(2)
    q_pair = q_ref[0]
    qa_pair = qa_ref[0]
    run, sub, qry = _tile_iotas(tq)
    causal = (sub * KEY_RUN + run <= qry).reshape(KEY_TILE, tq)
    q_heads = [
        jnp.concatenate([
            jnp.where(_head_rows_mask(q_pair.shape, j, HEAD_DIM), q_pair, jnp.zeros_like(q_pair)),
            jnp.where(_head_rows_mask(qa_pair.shape, j, AUG_STRIDE), qa_pair, jnp.zeros_like(qa_pair)),
        ], axis=0) for j in range(2)]

    def scores(kt):
        start = pl.multiple_of(kt * KEY_TILE, KEY_TILE)
        k = jnp.concatenate([k_ref[0, pl.ds(start, KEY_TILE), :],
                             ka_ref[0, pl.ds(start, KEY_TILE), :]], axis=1)
        return tuple(_dot(k, q_heads[j]) for j in range(2))

    def weights(logits, m, l, masked):
        if masked:
            logits = jnp.where(causal, logits, -jnp.inf)
        m_new = jnp.maximum(m, jnp.max(logits, axis=0, keepdims=True))
        p = jnp.exp(logits - m_new)
        alpha = jnp.exp(m - m_new)
        l = alpha * l + jnp.sum(p, axis=0, keepdims=True)
        return p.astype(BF16), alpha, m_new, l

    def accumulate(kt, acc, p, alpha):
        return tuple(alpha[j] * acc[j] + _dot(v_ref[0, kt, j * HEAD_DIM:(j + 1) * HEAD_DIM, :], p[j])
                     for j in range(2))

    def both(fn, *args):
        out = tuple(fn(*(x[j] for x in args)) for j in range(2))
        return tuple(zip(*out))

    m0 = (jnp.full((1, tq), -jnp.inf, F32),) * 2
    l0 = (jnp.zeros((1, tq), F32),) * 2
    p, alpha, m, l = both(lambda z, mm, ll: weights(z, mm, ll, True), scores(qt), m0, l0)
    acc = (jnp.zeros((HEAD_DIM, tq), F32),) * 2

    def body(i, st):
        logits, p, alpha, m, l, acc = st
        kt = qt - 1 - i
        acc = accumulate(kt + 1, acc, p, alpha)
        z_next = scores(jnp.maximum(kt - 1, 0))
        p, alpha, m, l = both(lambda z, mm, ll: weights(z, mm, ll, False), logits, m, l)
        return z_next, p, alpha, m, l, acc

    _, p, alpha, _, l, acc = lax.fori_loop(
        0, qt, body, (scores(jnp.maximum(qt - 1, 0)), p, alpha, m, l, acc))
    acc = accumulate(0, acc, p, alpha)
    o_ref[0] = jnp.concatenate([acc[j] / l[j] for j in range(2)], axis=0)


def _fox_attention(q_t, qaug_t, k, kaug, v_t, tq):
    b, _, s = q_t.shape
    q_spec = pl.BlockSpec((1, LANES, tq), lambda i, p, t: (i, p, t))
    k_spec = pl.BlockSpec((1, s, LANES), lambda i, p, t: (i, 0, p))
    return pl.pallas_call(
        _fox_kernel,
        grid=(b, N_PAIRS, s // tq),
        in_specs=[q_spec, q_spec, k_spec, k_spec,
                  pl.BlockSpec((1, s // KEY_TILE, LANES, KEY_TILE), lambda i, p, t: (i, 0, p, 0))],
        out_specs=q_spec,
        out_shape=jax.ShapeDtypeStruct((b, D_GRP, s), F32),
        compiler_params=pltpu.CompilerParams(
            dimension_semantics=("parallel", "parallel", "arbitrary"), vmem_limit_bytes=VMEM_LIMIT),
        name="fox_attention",
    )(q_t, qaug_t, k, kaug, v_t)


def _rms_rows(x):
    return x * lax.rsqrt(jnp.mean(x * x, axis=0, keepdims=True) + EPS)


def _out_kernel(osb_ref, ofx_ref, gate_ref, x_ref, wout_ref, ada_ref, gf_ref, o_ref, *, d_model, final):
    y = jnp.concatenate([_rms_rows(osb_ref[0]), _rms_rows(ofx_ref[0])], axis=0)
    g = gate_ref[0]
    y = (y * (g / (1.0 + jnp.exp(-g)))).astype(BF16)
    gate = ada_ref[0][:, 2 * d_model:]
    out = x_ref[0] + (1.0 + gate) * _dot_tn(y, wout_ref[...])
    if final:
        out = out * lax.rsqrt(jnp.mean(out * out, axis=-1, keepdims=True) + EPS) * gf_ref[...]
    o_ref[0] = out


def _output(o_sb, o_fx, gate_t, x, wout, ada_l, g_final, ts, final):
    b, s, d = x.shape
    const = lambda shape: pl.BlockSpec(shape, lambda i, j: (0,) * len(shape))
    return pl.pallas_call(
        functools.partial(_out_kernel, d_model=d, final=final),
        grid=(b, s // ts),
        in_specs=[
            pl.BlockSpec((1, D_GRP, ts), lambda i, j: (i, 0, j)),
            pl.BlockSpec((1, D_GRP, ts), lambda i, j: (i, 0, j)),
            pl.BlockSpec((1, 2 * D_GRP, ts), lambda i, j: (i, 0, j)),
            pl.BlockSpec((1, ts, d), lambda i, j: (i, j, 0)),
            const(wout.shape),
            pl.BlockSpec((1, 1, 3 * d), lambda i, j: (i, 0, 0)),
            const((1, d)),
        ],
        out_specs=pl.BlockSpec((1, ts, d), lambda i, j: (i, j, 0)),
        out_shape=jax.ShapeDtypeStruct((b, s, d), F32),
        compiler_params=pltpu.CompilerParams(
            dimension_semantics=("parallel", "parallel"), vmem_limit_bytes=VMEM_LIMIT),
        name="output",
    )(o_sb, o_fx, gate_t, x, wout, ada_l.reshape(b, 1, 3 * d), g_final.reshape(1, d))


def kernel(x, c, w_ada, b_ada, g_norm, w_in, b_f, g_grp, w_out, g_final):
    b, s, d = x.shape
    depth = w_ada.shape[0]
    ts = min(512, s)
    tq = min(256, s)
    consts = _aug_constants()

    grp = lambda i: w_in[:, :, i * D_GRP:(i + 1) * D_GRP]
    t = lambda w: jnp.swapaxes(w, 1, 2)
    wq = jnp.stack([t(grp(0)) * (0.5 * Q_SCALE), t(grp(3)) * Q_SCALE], axis=1).astype(BF16)
    wk = jnp.stack([grp(1), grp(4)], axis=1).astype(BF16)
    wv = jnp.stack([t(grp(2)), t(grp(5))], axis=1).astype(BF16)
    wg = t(w_in[:, :, 6 * D_GRP:8 * D_GRP]).astype(BF16)
    wf = jnp.zeros((depth, d, LANES), BF16).at[:, :, :N_HEADS].set(w_in[:, :, 8 * D_GRP:].astype(BF16))
    bf = jnp.zeros((depth, 1, LANES), F32).at[:, 0, :N_HEADS].set(b_f)
    wout = (g_grp[:, :, None] * w_out).astype(BF16)

    ada = _adaln(c, w_ada, b_ada)
    for l in range(depth):
        q_sb, k_sb, v_sb, q_fx, k_fx, v_fx, gate_t, qaug, kaug = _project(
            x, ada[l], g_norm[l], wq[l], wk[l], wv[l], wg[l], wf[l], bf[l], consts, ts)
        o_sb = _sb_attention(q_sb, k_sb, v_sb, tq)
        o_fx = _fox_attention(q_fx, qaug, k_fx, kaug, v_fx, tq)
        x = _output(o_sb, o_fx, gate_t, x, wout[l], ada[l], g_final, ts, l == depth - 1)
    return x
```

```python
import functools

import numpy as np
import jax
import jax.numpy as jnp
from jax import lax
from jax.experimental import pallas as pl
from jax.experimental.pallas import tpu as pltpu

F32 = jnp.float32
BF16 = jnp.bfloat16

HEAD_DIM = 64
N_HEADS = 8
D_GRP = N_HEADS * HEAD_DIM
LANES = 128
SUBLANES = 8
N_PAIRS = D_GRP // LANES
KEY_TILE = 256
KEY_RUN = KEY_TILE // SUBLANES
AUG_STRIDE = 16
N_SPLIT = 3
EPS = 1e-6
Q_SCALE = HEAD_DIM ** -0.5
LOG2E = 1.4426950408889634
VMEM_LIMIT = 52 * 1024 * 1024


def _split_bf16(x, n):
    parts = []
    r = x
    for i in range(n):
        p = r.astype(BF16)
        parts.append(p)
        if i + 1 < n:
            r = r - p.astype(F32)
    return parts


def _dot(a, b):
    return jnp.dot(a, b, preferred_element_type=F32)


def _dot_nt(a, b):
    return lax.dot_general(a, b, (((1,), (1,)), ((), ())), preferred_element_type=F32)


def _dot_tn(a, b):
    return lax.dot_general(a, b, (((0,), (0,)), ((), ())), preferred_element_type=F32)


def _softplus(z):
    return jnp.maximum(z, 0.0) + jnp.log1p(jnp.exp(-jnp.abs(z)))


def _key_of_row(p):
    return (p & (SUBLANES - 1)) * KEY_RUN + (p >> 3)


def _ada_kernel(c_ref, w_ref, b_ref, o_ref):
    c = c_ref[...]
    c_act = c / (1.0 + jnp.exp(-c))
    w_parts = _split_bf16(w_ref[0], 2)
    acc = jnp.zeros(o_ref.shape[1:], F32)
    for cp in _split_bf16(c_act, N_SPLIT):
        for wp in w_parts:
            acc = acc + _dot(cp, wp)
    o_ref[0] = acc + b_ref[0]


def _adaln(c, w_ada, b_ada):
    depth, d, d3 = w_ada.shape
    b = c.shape[0]
    rows = SUBLANES
    tn = 1024
    c_pad = jnp.zeros((rows, d), F32).at[:b].set(c)
    out = pl.pallas_call(
        _ada_kernel,
        grid=(depth, d3 // tn),
        in_specs=[
            pl.BlockSpec((rows, d), lambda l, n: (0, 0)),
            pl.BlockSpec((1, d, tn), lambda l, n: (l, 0, n)),
            pl.BlockSpec((1, 1, tn), lambda l, n: (l, 0, n)),
        ],
        out_specs=pl.BlockSpec((1, rows, tn), lambda l, n: (l, 0, n)),
        out_shape=jax.ShapeDtypeStruct((depth, rows, d3), F32),
        compiler_params=pltpu.CompilerParams(
            dimension_semantics=("parallel", "parallel"), vmem_limit_bytes=VMEM_LIMIT),
        name="adaln",
    )(c_pad, w_ada, b_ada.reshape(depth, 1, d3))
    return out[:, :b]


def _proj_kernel(x_ref, ada_ref, gn_ref, wq_ref, wk_ref, wv_ref, wg_ref, wf_ref, bf_ref,
                 pq_ref, pk_ref,
                 qsb_ref, ksb_ref, vsb_ref, qfx_ref, kfx_ref, vfx_ref, gate_ref, qaug_ref, kaug_ref,
                 carry_ref, *, d_model):
    ts = x_ref.shape[1]
    x = x_ref[0]
    ada = ada_ref[0]
    shift = ada[:, :d_model]
    scale = ada[:, d_model:2 * d_model]
    r = lax.rsqrt(jnp.mean(x * x, axis=-1, keepdims=True) + EPS)
    h = ((x * r) * gn_ref[...] * (1.0 + scale) + shift).astype(BF16)

    row = lax.broadcasted_iota(jnp.int32, (ts, ts), 0)
    col = lax.broadcasted_iota(jnp.int32, (ts, ts), 1)
    in_tile = row & (KEY_TILE - 1)
    key_row = (row - in_tile) + _key_of_row(in_tile)
    perm = jnp.where(col == key_row, 1.0, 0.0).astype(BF16)
    h_keys = _dot(perm, h).astype(BF16)

    qsb_ref[0] = _dot_nt(wq_ref[0], h).astype(BF16)
    qfx_ref[0] = _dot_nt(wq_ref[1], h).astype(BF16)
    gate_ref[0] = _dot_nt(wg_ref[...], h)
    ksb_ref[0] = _dot(h_keys, wk_ref[0]).astype(BF16)
    kfx_ref[0] = _dot(h_keys, wk_ref[1]).astype(BF16)
    for o_ref, w in ((vsb_ref, wv_ref[0]), (vfx_ref, wv_ref[1])):
        v_t = _dot_nt(w, h_keys).astype(BF16)
        for t in range(ts // KEY_TILE):
            o_ref[0, t] = v_t[:, t * KEY_TILE:(t + 1) * KEY_TILE]

    zf = _dot(h, wf_ref[...]) + bf_ref[...]
    log_f_parts = _split_bf16(-_softplus(-zf), N_SPLIT)

    @pl.when(pl.program_id(1) == 0)
    def _():
        carry_ref[...] = jnp.zeros_like(carry_ref)

    carry = carry_ref[...]
    upto = jnp.where(col <= row, 1.0, 0.0).astype(BF16)
    upto_keys = jnp.where(col <= key_row, 1.0, 0.0).astype(BF16)
    cum = carry
    cum_keys = carry
    for part in log_f_parts:
        cum = cum + _dot(upto, part)
        cum_keys = cum_keys + _dot(upto_keys, part)
    carry_ref[...] = cum[ts - 1:ts, :]

    ones = jnp.ones((ts, LANES), BF16)
    pieces = jnp.concatenate(_split_bf16(cum * LOG2E, N_SPLIT) + [ones], axis=1)
    pieces_keys = jnp.concatenate(_split_bf16(cum_keys * LOG2E, N_SPLIT) + [ones], axis=1)
    qaug_ref[0] = _dot_nt(pq_ref[...], pieces).astype(BF16)
    kaug_ref[0] = _dot(pieces_keys, pk_ref[...]).astype(BF16)


def _aug_constants():
    pq = np.zeros((D_GRP, (N_SPLIT + 1) * LANES), np.float32)
    pk = np.zeros(((N_SPLIT + 1) * LANES, D_GRP), np.float32)
    one_col = N_SPLIT * LANES
    for h in range(N_HEADS):
        base = (h // 2) * LANES + AUG_STRIDE * (h % 2)
        for j in range(N_SPLIT):
            pq[base + j, j * LANES + h] = 1.0
            pk[one_col, base + j] = 1.0
            pk[j * LANES + h, base + N_SPLIT + j] = -1.0
            pq[base + N_SPLIT + j, one_col] = 1.0
    return jnp.asarray(pq, BF16), jnp.asarray(pk, BF16)


def _project(x, ada_l, g_norm_l, wq, wk, wv, wg, wf, bf, consts, ts):
    b, s, d = x.shape
    pq, pk = consts
    const = lambda shape: pl.BlockSpec(shape, lambda i, j: (0,) * len(shape))
    row_major = pl.BlockSpec((1, ts, D_GRP), lambda i, j: (i, j, 0))
    feat_major = pl.BlockSpec((1, D_GRP, ts), lambda i, j: (i, 0, j))
    v_spec = pl.BlockSpec((1, ts // KEY_TILE, D_GRP, KEY_TILE), lambda i, j: (i, j, 0, 0))
    row_shape = jax.ShapeDtypeStruct((b, s, D_GRP), BF16)
    feat_shape = jax.ShapeDtypeStruct((b, D_GRP, s), BF16)
    v_shape = jax.ShapeDtypeStruct((b, s // KEY_TILE, D_GRP, KEY_TILE), BF16)
    return pl.pallas_call(
        functools.partial(_proj_kernel, d_model=d),
        grid=(b, s // ts),
        in_specs=[
            pl.BlockSpec((1, ts, d), lambda i, j: (i, j, 0)),
            pl.BlockSpec((1, 1, 3 * d), lambda i, j: (i, 0, 0)),
            const((1, d)),
            const(wq.shape), const(wk.shape), const(wv.shape), const(wg.shape), const(wf.shape),
            const(bf.shape), const(pq.shape), const(pk.shape),
        ],
        out_specs=[feat_major, row_major, v_spec, feat_major, row_major, v_spec,
                   pl.BlockSpec((1, 2 * D_GRP, ts), lambda i, j: (i, 0, j)), feat_major, row_major],
        out_shape=[feat_shape, row_shape, v_shape, feat_shape, row_shape, v_shape,
                   jax.ShapeDtypeStruct((b, 2 * D_GRP, s), F32), feat_shape, row_shape],
        scratch_shapes=[pltpu.VMEM((1, LANES), F32)],
        compiler_params=pltpu.CompilerParams(
            dimension_semantics=("parallel", "arbitrary"), vmem_limit_bytes=VMEM_LIMIT),
        name="project",
    )(x, ada_l.reshape(b, 1, 3 * d), g_norm_l.reshape(1, d), wq, wk, wv, wg, wf, bf, pq, pk)


def _tile_iotas(tq):
    shape = (KEY_RUN, SUBLANES, tq)
    run = lax.broadcasted_iota(jnp.int32, shape, 0)
    sub = lax.broadcasted_iota(jnp.int32, shape, 1)
    qry = lax.broadcasted_iota(jnp.int32, shape, 2)
    return run, sub, qry


def _head_rows(block, j, width):
    row = lax.broadcasted_iota(jnp.int32, block.shape, 0)
    return jnp.where((row >= j * width) & (row < (j + 1) * width), block, jnp.zeros_like(block))


def _shift_up(x, k, fill):
    sub = lax.broadcasted_iota(jnp.int32, x.shape, 0)
    return jnp.where(sub + k < SUBLANES, pltpu.roll(x, SUBLANES - k, axis=0), fill)


def _tile_pipeline(qt, stage_scores, stage_weights, stage_values, state):
    clamp = lambda kt: jnp.maximum(kt, 0)
    stage_scores(1, qt)
    state = stage_weights(1, 0, state, True)
    stage_scores(0, clamp(qt - 1))

    def half(slot, kt, state):
        stage_scores(1 - slot, clamp(kt - 1))
        state = stage_values(slot, kt + 1, state, None)
        return stage_weights(slot, 1 - slot, state, False)

    def body(i, state):
        kt = qt - 1 - 2 * i
        return half(1, kt - 1, half(0, kt, state))

    state = lax.fori_loop(0, (qt + 1) // 2, body, state)
    return stage_values(0, 0, state, qt % 2 == 0)


def _attn_scratch(tq):
    return [pltpu.VMEM((2, 2, KEY_TILE, tq), F32), pltpu.VMEM((2, 2, KEY_TILE, tq), BF16)]


def _sb_kernel(q_ref, k_ref, v_ref, o_ref, z_ref, a_ref):
    tq = q_ref.shape[2]
    qt = pl.program_id(2)
    run, sub, qry = _tile_iotas(tq)
    causal = sub * KEY_RUN + run < qry
    q_heads = [_head_rows(q_ref[0], h, HEAD_DIM) for h in range(2)]

    def stage_scores(slot, kt):
        start = pl.multiple_of(kt * KEY_TILE, KEY_TILE)
        k = k_ref[0, pl.ds(start, KEY_TILE), :]
        for h in range(2):
            z_ref[slot, h] = _dot(k, q_heads[h])

    def weights(half_z, carry, masked):
        th = jnp.tanh(half_z.reshape(KEY_RUN, SUBLANES, tq))
        rem = 0.5 - 0.5 * th
        if masked:
            rem = jnp.where(causal, rem, 1.0)
        prod = jnp.ones((SUBLANES, tq), F32)
        parts = [None] * KEY_RUN
        for r in reversed(range(KEY_RUN)):
            below = prod * rem[r]
            parts[r] = prod - below
            prod = below
        incl = prod
        for step in (1, 2, 4):
            incl = incl * _shift_up(incl, step, 1.0)
        scale = _shift_up(incl, 1, 1.0) * carry
        a = (jnp.stack(parts) * scale[None]).reshape(KEY_TILE, tq).astype(BF16)
        return a, carry * jnp.broadcast_to(incl[0:1], carry.shape)

    def stage_weights(src, dst, state, masked):
        carry, acc = state
        new_carry = []
        for h in range(2):
            a_ref[dst, h], c = weights(z_ref[src, h], carry[h], masked)
            new_carry.append(c)
        return tuple(new_carry), acc

    def stage_values(slot, kt, state, valid):
        carry, acc = state
        new_acc = []
        for h in range(2):
            av = _dot(v_ref[0, kt, h * HEAD_DIM:(h + 1) * HEAD_DIM, :], a_ref[slot, h])
            new_acc.append(acc[h] + (av if valid is None else jnp.where(valid, av, 0.0)))
        return carry, tuple(new_acc)

    state = ((jnp.ones((SUBLANES, tq), F32),) * 2, (jnp.zeros((HEAD_DIM, tq), F32),) * 2)
    _, acc = _tile_pipeline(qt, stage_scores, stage_weights, stage_values, state)
    o_ref[0] = jnp.concatenate(acc, axis=0)


def _sb_attention(q_t, k, v_t, tq):
    b, _, s = q_t.shape
    q_spec = pl.BlockSpec((1, LANES, tq), lambda i, p, t: (i, p, t))
    return pl.pallas_call(
        _sb_kernel,
        grid=(b, N_PAIRS, s // tq),
        in_specs=[
            q_spec,
            pl.BlockSpec((1, s, LANES), lambda i, p, t: (i, 0, p)),
            pl.BlockSpec((1, s // KEY_TILE, LANES, KEY_TILE), lambda i, p, t: (i, 0, p, 0)),
        ],
        out_specs=q_spec,
        out_shape=jax.ShapeDtypeStruct((b, D_GRP, s), F32),
        scratch_shapes=_attn_scratch(tq),
        compiler_params=pltpu.CompilerParams(
            dimension_semantics=("parallel", "parallel", "arbitrary"), vmem_limit_bytes=VMEM_LIMIT),
        name="sb_attention",
    )(q_t, k, v_t)


def _fox_kernel(q_ref, qa_ref, k_ref, ka_ref, v_ref, o_ref, z_ref, p_ref):
    tq = q_ref.shape[2]
    qt = pl.program_id(2)
    run, sub, qry = _tile_iotas(tq)
    causal = (sub * KEY_RUN + run <= qry).reshape(KEY_TILE, tq)
    q_heads = [jnp.concatenate([_head_rows(q_ref[0], h, HEAD_DIM),
                                _head_rows(qa_ref[0], h, AUG_STRIDE)], axis=0) for h in range(2)]

    def stage_scores(slot, kt):
        start = pl.multiple_of(kt * KEY_TILE, KEY_TILE)
        k = jnp.concatenate([k_ref[0, pl.ds(start, KEY_TILE), :],
                             ka_ref[0, pl.ds(start, KEY_TILE), :]], axis=1)
        for h in range(2):
            z_ref[slot, h] = _dot(k, q_heads[h])

    def stage_weights(src, dst, state, masked):
        m, l, _, _, acc = state
        m_new, l_new, alpha = [], [], []
        for h in range(2):
            logits = z_ref[src, h]
            if masked:
                logits = jnp.where(causal, logits, -jnp.inf)
            mh = jnp.maximum(m[h], jnp.max(logits, axis=0, keepdims=True))
            p = jnp.exp2(logits - mh)
            ah = jnp.exp2(m[h] - mh)
            p_ref[dst, h] = p.astype(BF16)
            m_new.append(mh)
            alpha.append(ah)
            l_new.append(ah * l[h] + jnp.sum(p, axis=0, keepdims=True))
        return tuple(m_new), tuple(l_new), l, tuple(alpha), acc

    def stage_values(slot, kt, state, valid):
        m, l, l_prev, alpha, acc = state
        new_acc = []
        for h in range(2):
            pv = _dot(v_ref[0, kt, h * HEAD_DIM:(h + 1) * HEAD_DIM, :], p_ref[slot, h])
            upd = alpha[h] * acc[h] + pv
            new_acc.append(upd if valid is None else jnp.where(valid, upd, acc[h]))
        if valid is not None:
            l = tuple(jnp.where(valid, l[h], l_prev[h]) for h in range(2))
        return m, l, l_prev, alpha, tuple(new_acc)

    row = lambda v: (jnp.full((1, tq), v, F32),) * 2
    state = (row(-jnp.inf), row(0.0), row(0.0), row(1.0), (jnp.zeros((HEAD_DIM, tq), F32),) * 2)
    _, l, _, _, acc = _tile_pipeline(qt, stage_scores, stage_weights, stage_values, state)
    o_ref[0] = jnp.concatenate([acc[h] / l[h] for h in range(2)], axis=0)


def _fox_attention(q_t, qaug_t, k, kaug, v_t, tq):
    b, _, s = q_t.shape
    q_spec = pl.BlockSpec((1, LANES, tq), lambda i, p, t: (i, p, t))
    k_spec = pl.BlockSpec((1, s, LANES), lambda i, p, t: (i, 0, p))
    return pl.pallas_call(
        _fox_kernel,
        grid=(b, N_PAIRS, s // tq),
        in_specs=[q_spec, q_spec, k_spec, k_spec,
                  pl.BlockSpec((1, s // KEY_TILE, LANES, KEY_TILE), lambda i, p, t: (i, 0, p, 0))],
        out_specs=q_spec,
        out_shape=jax.ShapeDtypeStruct((b, D_GRP, s), F32),
        scratch_shapes=_attn_scratch(tq),
        compiler_params=pltpu.CompilerParams(
            dimension_semantics=("parallel", "parallel", "arbitrary"), vmem_limit_bytes=VMEM_LIMIT),
        name="fox_attention",
    )(q_t, qaug_t, k, kaug, v_t)


def _rms_rows(x):
    return x * lax.rsqrt(jnp.mean(x * x, axis=0, keepdims=True) + EPS)


def _out_kernel(osb_ref, ofx_ref, gate_ref, x_ref, wout_ref, ada_ref, gf_ref, o_ref, *, d_model, final):
    y = jnp.concatenate([_rms_rows(osb_ref[0]), _rms_rows(ofx_ref[0])], axis=0)
    g = gate_ref[0]
    y = (y * (g / (1.0 + jnp.exp(-g)))).astype(BF16)
    gate = ada_ref[0][:, 2 * d_model:]
    out = x_ref[0] + (1.0 + gate) * _dot_tn(y, wout_ref[...])
    if final:
        out = out * lax.rsqrt(jnp.mean(out * out, axis=-1, keepdims=True) + EPS) * gf_ref[...]
    o_ref[0] = out


def _output(o_sb, o_fx, gate_t, x, wout, ada_l, g_final, ts, final):
    b, s, d = x.shape
    const = lambda shape: pl.BlockSpec(shape, lambda i, j: (0,) * len(shape))
    return pl.pallas_call(
        functools.partial(_out_kernel, d_model=d, final=final),
        grid=(b, s // ts),
        in_specs=[
            pl.BlockSpec((1, D_GRP, ts), lambda i, j: (i, 0, j)),
            pl.BlockSpec((1, D_GRP, ts), lambda i, j: (i, 0, j)),
            pl.BlockSpec((1, 2 * D_GRP, ts), lambda i, j: (i, 0, j)),
            pl.BlockSpec((1, ts, d), lambda i, j: (i, j, 0)),
            const(wout.shape),
            pl.BlockSpec((1, 1, 3 * d), lambda i, j: (i, 0, 0)),
            const((1, d)),
        ],
        out_specs=pl.BlockSpec((1, ts, d), lambda i, j: (i, j, 0)),
        out_shape=jax.ShapeDtypeStruct((b, s, d), F32),
        compiler_params=pltpu.CompilerParams(
            dimension_semantics=("parallel", "parallel"), vmem_limit_bytes=VMEM_LIMIT),
        name="output",
    )(o_sb, o_fx, gate_t, x, wout, ada_l.reshape(b, 1, 3 * d), g_final.reshape(1, d))


def kernel(x, c, w_ada, b_ada, g_norm, w_in, b_f, g_grp, w_out, g_final):
    b, s, d = x.shape
    depth = w_ada.shape[0]
    ts = min(512, s)
    tq = min(256, s)
    consts = _aug_constants()

    grp = lambda i: w_in[:, :, i * D_GRP:(i + 1) * D_GRP]
    t = lambda w: jnp.swapaxes(w, 1, 2)
    wq = jnp.stack([t(grp(0)) * (0.5 * Q_SCALE), t(grp(3)) * (LOG2E * Q_SCALE)], axis=1).astype(BF16)
    wk = jnp.stack([grp(1), grp(4)], axis=1).astype(BF16)
    wv = jnp.stack([t(grp(2)), t(grp(5))], axis=1).astype(BF16)
    wg = t(w_in[:, :, 6 * D_GRP:8 * D_GRP]).astype(BF16)
    wf = jnp.zeros((depth, d, LANES), BF16).at[:, :, :N_HEADS].set(w_in[:, :, 8 * D_GRP:].astype(BF16))
    bf = jnp.zeros((depth, 1, LANES), F32).at[:, 0, :N_HEADS].set(b_f)
    wout = (g_grp[:, :, None] * w_out).astype(BF16)

    ada = _adaln(c, w_ada, b_ada)
    for l in range(depth):
        q_sb, k_sb, v_sb, q_fx, k_fx, v_fx, gate_t, qaug, kaug = _project(
            x, ada[l], g_norm[l], wq[l], wk[l], wv[l], wg[l], wf[l], bf[l], consts, ts)
        o_sb = _sb_attention(q_sb, k_sb, v_sb, tq)
        o_fx = _fox_attention(q_fx, qaug, k_fx, kaug, v_fx, tq)
        x = _output(o_sb, o_fx, gate_t, x, wout[l], ada[l], g_final, ts, l == depth - 1)
    return x
```

```python
import functools

import numpy as np
import jax
import jax.numpy as jnp
from jax import lax
from jax.experimental import pallas as pl
from jax.experimental.pallas import tpu as pltpu

F32 = jnp.float32
BF16 = jnp.bfloat16

HEAD_DIM = 64
N_HEADS = 8
D_GRP = N_HEADS * HEAD_DIM
LANES = 128
SUBLANES = 8
N_PAIRS = D_GRP // LANES
KEY_TILE = 256
KEY_RUN = KEY_TILE // SUBLANES
AUG_STRIDE = 16
N_SPLIT = 3
EPS = 1e-6
Q_SCALE = HEAD_DIM ** -0.5
LOG2E = 1.4426950408889634
VMEM_LIMIT = 52 * 1024 * 1024


def _split_bf16(x, n):
    parts = []
    r = x
    for i in range(n):
        p = r.astype(BF16)
        parts.append(p)
        if i + 1 < n:
            r = r - p.astype(F32)
    return parts


def _dot(a, b):
    return jnp.dot(a, b, preferred_element_type=F32)


def _dot_nt(a, b):
    return lax.dot_general(a, b, (((1,), (1,)), ((), ())), preferred_element_type=F32)


def _dot_tn(a, b):
    return lax.dot_general(a, b, (((0,), (0,)), ((), ())), preferred_element_type=F32)


def _softplus(z):
    return jnp.maximum(z, 0.0) + jnp.log1p(jnp.exp(-jnp.abs(z)))


def _key_of_row(p):
    return (p & (SUBLANES - 1)) * KEY_RUN + (p >> 3)


def _ada_kernel(c_ref, w_ref, b_ref, o_ref):
    c = c_ref[...]
    c_act = c / (1.0 + jnp.exp(-c))
    w_parts = _split_bf16(w_ref[0], 2)
    acc = jnp.zeros(o_ref.shape[1:], F32)
    for cp in _split_bf16(c_act, N_SPLIT):
        for wp in w_parts:
            acc = acc + _dot(cp, wp)
    o_ref[0] = acc + b_ref[0]


def _adaln(c, w_ada, b_ada):
    depth, d, d3 = w_ada.shape
    b = c.shape[0]
    rows = SUBLANES
    tn = 1024
    c_pad = jnp.zeros((rows, d), F32).at[:b].set(c)
    out = pl.pallas_call(
        _ada_kernel,
        grid=(depth, d3 // tn),
        in_specs=[
            pl.BlockSpec((rows, d), lambda l, n: (0, 0)),
            pl.BlockSpec((1, d, tn), lambda l, n: (l, 0, n)),
            pl.BlockSpec((1, 1, tn), lambda l, n: (l, 0, n)),
        ],
        out_specs=pl.BlockSpec((1, rows, tn), lambda l, n: (l, 0, n)),
        out_shape=jax.ShapeDtypeStruct((depth, rows, d3), F32),
        compiler_params=pltpu.CompilerParams(
            dimension_semantics=("parallel", "parallel"), vmem_limit_bytes=VMEM_LIMIT),
        name="adaln",
    )(c_pad, w_ada, b_ada.reshape(depth, 1, d3))
    return out[:, :b]


def _proj_kernel(x_ref, ada_ref, gn_ref, wq_ref, wk_ref, wv_ref, wg_ref, wf_ref, bf_ref,
                 pq_ref, pk_ref,
                 qsb_ref, ksb_ref, vsb_ref, qfx_ref, kfx_ref, vfx_ref, gate_ref, qaug_ref, kaug_ref,
                 carry_ref, *, d_model):
    ts = x_ref.shape[1]
    x = x_ref[0]
    ada = ada_ref[0]
    shift = ada[:, :d_model]
    scale = ada[:, d_model:2 * d_model]
    r = lax.rsqrt(jnp.mean(x * x, axis=-1, keepdims=True) + EPS)
    h = ((x * r) * gn_ref[...] * (1.0 + scale) + shift).astype(BF16)

    row = lax.broadcasted_iota(jnp.int32, (ts, ts), 0)
    col = lax.broadcasted_iota(jnp.int32, (ts, ts), 1)
    in_tile = row & (KEY_TILE - 1)
    key_row = (row - in_tile) + _key_of_row(in_tile)
    perm = jnp.where(col == key_row, 1.0, 0.0).astype(BF16)
    h_keys = _dot(perm, h).astype(BF16)

    qsb_ref[0] = _dot_nt(wq_ref[0], h).astype(BF16)
    qfx_ref[0] = _dot_nt(wq_ref[1], h).astype(BF16)
    gate_ref[0] = _dot_nt(wg_ref[...], h)
    ksb_ref[0] = _dot(h_keys, wk_ref[0]).astype(BF16)
    kfx_ref[0] = _dot(h_keys, wk_ref[1]).astype(BF16)
    for o_ref, w in ((vsb_ref, wv_ref[0]), (vfx_ref, wv_ref[1])):
        v_t = _dot_nt(w, h_keys).astype(BF16)
        for t in range(ts // KEY_TILE):
            o_ref[0, t] = v_t[:, t * KEY_TILE:(t + 1) * KEY_TILE]

    zf = _dot(h, wf_ref[...]) + bf_ref[...]
    log_f_parts = _split_bf16(-_softplus(-zf), N_SPLIT)

    @pl.when(pl.program_id(1) == 0)
    def _():
        carry_ref[...] = jnp.zeros_like(carry_ref)

    carry = carry_ref[...]
    upto = jnp.where(col <= row, 1.0, 0.0).astype(BF16)
    upto_keys = jnp.where(col <= key_row, 1.0, 0.0).astype(BF16)
    cum = carry
    cum_keys = carry
    for part in log_f_parts:
        cum = cum + _dot(upto, part)
        cum_keys = cum_keys + _dot(upto_keys, part)
    carry_ref[...] = cum[ts - 1:ts, :]

    ones = jnp.ones((ts, LANES), BF16)
    pieces = jnp.concatenate(_split_bf16(cum * LOG2E, N_SPLIT) + [ones], axis=1)
    pieces_keys = jnp.concatenate(_split_bf16(cum_keys * LOG2E, N_SPLIT) + [ones], axis=1)
    qaug_ref[0] = _dot_nt(pq_ref[...], pieces).astype(BF16)
    kaug_ref[0] = _dot(pieces_keys, pk_ref[...]).astype(BF16)


def _aug_constants():
    pq = np.zeros((D_GRP, (N_SPLIT + 1) * LANES), np.float32)
    pk = np.zeros(((N_SPLIT + 1) * LANES, D_GRP), np.float32)
    one_col = N_SPLIT * LANES
    for h in range(N_HEADS):
        base = (h // 2) * LANES + AUG_STRIDE * (h % 2)
        for j in range(N_SPLIT):
            pq[base + j, j * LANES + h] = 1.0
            pk[one_col, base + j] = 1.0
            pk[j * LANES + h, base + N_SPLIT + j] = -1.0
            pq[base + N_SPLIT + j, one_col] = 1.0
    return jnp.asarray(pq, BF16), jnp.asarray(pk, BF16)


def _project(x, ada_l, g_norm_l, wq, wk, wv, wg, wf, bf, consts, ts):
    b, s, d = x.shape
    pq, pk = consts
    const = lambda shape: pl.BlockSpec(shape, lambda i, j: (0,) * len(shape))
    row_major = pl.BlockSpec((1, ts, D_GRP), lambda i, j: (i, j, 0))
    feat_major = pl.BlockSpec((1, D_GRP, ts), lambda i, j: (i, 0, j))
    v_spec = pl.BlockSpec((1, ts // KEY_TILE, D_GRP, KEY_TILE), lambda i, j: (i, j, 0, 0))
    row_shape = jax.ShapeDtypeStruct((b, s, D_GRP), BF16)
    feat_shape = jax.ShapeDtypeStruct((b, D_GRP, s), BF16)
    v_shape = jax.ShapeDtypeStruct((b, s // KEY_TILE, D_GRP, KEY_TILE), BF16)
    return pl.pallas_call(
        functools.partial(_proj_kernel, d_model=d),
        grid=(b, s // ts),
        in_specs=[
            pl.BlockSpec((1, ts, d), lambda i, j: (i, j, 0)),
            pl.BlockSpec((1, 1, 3 * d), lambda i, j: (i, 0, 0)),
            const((1, d)),
            const(wq.shape), const(wk.shape), const(wv.shape), const(wg.shape), const(wf.shape),
            const(bf.shape), const(pq.shape), const(pk.shape),
        ],
        out_specs=[feat_major, row_major, v_spec, feat_major, row_major, v_spec,
                   pl.BlockSpec((1, 2 * D_GRP, ts), lambda i, j: (i, 0, j)), feat_major, row_major],
        out_shape=[feat_shape, row_shape, v_shape, feat_shape, row_shape, v_shape,
                   jax.ShapeDtypeStruct((b, 2 * D_GRP, s), F32), feat_shape, row_shape],
        scratch_shapes=[pltpu.VMEM((1, LANES), F32)],
        compiler_params=pltpu.CompilerParams(
            dimension_semantics=("parallel", "arbitrary"), vmem_limit_bytes=VMEM_LIMIT),
        name="project",
    )(x, ada_l.reshape(b, 1, 3 * d), g_norm_l.reshape(1, d), wq, wk, wv, wg, wf, bf, pq, pk)


def _tile_iotas(tq):
    shape = (KEY_RUN, SUBLANES, tq)
    run = lax.broadcasted_iota(jnp.int32, shape, 0)
    sub = lax.broadcasted_iota(jnp.int32, shape, 1)
    qry = lax.broadcasted_iota(jnp.int32, shape, 2)
    return run, sub, qry


def _head_rows(block, j, width):
    row = lax.broadcasted_iota(jnp.int32, block.shape, 0)
    return jnp.where((row >= j * width) & (row < (j + 1) * width), block, jnp.zeros_like(block))


def _shift_up(x, k, fill):
    sub = lax.broadcasted_iota(jnp.int32, x.shape, 0)
    return jnp.where(sub + k < SUBLANES, pltpu.roll(x, SUBLANES - k, axis=0), fill)


def _tile_pipeline(qt, stage_scores, stage_weights, stage_values, state):
    clamp = lambda kt: jnp.maximum(kt, 0)
    stage_scores(1, qt)
    state = stage_weights(1, 0, state, True)
    stage_scores(0, clamp(qt - 1))

    def half(slot, kt, state):
        stage_scores(1 - slot, clamp(kt - 1))
        state = stage_values(slot, kt + 1, state, None)
        return stage_weights(slot, 1 - slot, state, False)

    def body(i, state):
        kt = qt - 1 - 2 * i
        return half(1, kt - 1, half(0, kt, state))

    state = lax.fori_loop(0, (qt + 1) // 2, body, state)
    return stage_values(0, 0, state, qt % 2 == 0)


def _attn_scratch(tq):
    return [pltpu.VMEM((2, 2, KEY_TILE, tq), F32), pltpu.VMEM((2, 2, KEY_TILE, tq), BF16)]


def _sb_kernel(q_ref, k_ref, v_ref, o_ref, z_ref, a_ref, *, tq):
    run, sub, qry = _tile_iotas(tq)
    causal = sub * KEY_RUN + run < qry
    lax.fori_loop(0, q_ref.shape[2] // tq,
                  lambda qt, _: _sb_query_tile(qt, causal, q_ref, k_ref, v_ref, o_ref, z_ref, a_ref, tq), 0)


def _sb_query_tile(qt, causal, q_ref, k_ref, v_ref, o_ref, z_ref, a_ref, tq):
    cols = pl.ds(pl.multiple_of(qt * tq, tq), tq)
    q_heads = [_head_rows(q_ref[0, :, cols], h, HEAD_DIM) for h in range(2)]

    def stage_scores(slot, kt):
        start = pl.multiple_of(kt * KEY_TILE, KEY_TILE)
        k = k_ref[0, pl.ds(start, KEY_TILE), :]
        for h in range(2):
            z_ref[slot, h] = _dot(k, q_heads[h])

    def weights(half_z, carry, masked):
        th = jnp.tanh(half_z.reshape(KEY_RUN, SUBLANES, tq))
        rem = 0.5 - 0.5 * th
        if masked:
            rem = jnp.where(causal, rem, 1.0)
        prod = jnp.ones((SUBLANES, tq), F32)
        parts = [None] * KEY_RUN
        for r in reversed(range(KEY_RUN)):
            below = prod * rem[r]
            parts[r] = prod - below
            prod = below
        incl = prod
        for step in (1, 2, 4):
            incl = incl * _shift_up(incl, step, 1.0)
        scale = _shift_up(incl, 1, 1.0) * carry
        a = (jnp.stack(parts) * scale[None]).reshape(KEY_TILE, tq).astype(BF16)
        return a, carry * jnp.broadcast_to(incl[0:1], carry.shape)

    def stage_weights(src, dst, state, masked):
        carry, acc = state
        new_carry = []
        for h in range(2):
            a_ref[dst, h], c = weights(z_ref[src, h], carry[h], masked)
            new_carry.append(c)
        return tuple(new_carry), acc

    def stage_values(slot, kt, state, valid):
        carry, acc = state
        new_acc = []
        for h in range(2):
            av = _dot(v_ref[0, kt, h * HEAD_DIM:(h + 1) * HEAD_DIM, :], a_ref[slot, h])
            new_acc.append(acc[h] + (av if valid is None else jnp.where(valid, av, 0.0)))
        return carry, tuple(new_acc)

    state = ((jnp.ones((SUBLANES, tq), F32),) * 2, (jnp.zeros((HEAD_DIM, tq), F32),) * 2)
    _, acc = _tile_pipeline(qt, stage_scores, stage_weights, stage_values, state)
    o_ref[0, :, cols] = jnp.concatenate(acc, axis=0)
    return 0


def _sb_attention(q_t, k, v_t, tq):
    b, _, s = q_t.shape
    q_spec = pl.BlockSpec((1, LANES, s), lambda i, p: (i, p, 0))
    return pl.pallas_call(
        functools.partial(_sb_kernel, tq=tq),
        grid=(b, N_PAIRS),
        in_specs=[
            q_spec,
            pl.BlockSpec((1, s, LANES), lambda i, p: (i, 0, p)),
            pl.BlockSpec((1, s // KEY_TILE, LANES, KEY_TILE), lambda i, p: (i, 0, p, 0)),
        ],
        out_specs=q_spec,
        out_shape=jax.ShapeDtypeStruct((b, D_GRP, s), F32),
        scratch_shapes=_attn_scratch(tq),
        compiler_params=pltpu.CompilerParams(
            dimension_semantics=("parallel", "parallel"), vmem_limit_bytes=VMEM_LIMIT),
        name="sb_attention",
    )(q_t, k, v_t)


def _fox_kernel(q_ref, qa_ref, k_ref, ka_ref, v_ref, o_ref, z_ref, p_ref, *, tq):
    run, sub, qry = _tile_iotas(tq)
    causal = (sub * KEY_RUN + run <= qry).reshape(KEY_TILE, tq)
    lax.fori_loop(0, q_ref.shape[2] // tq,
                  lambda qt, _: _fox_query_tile(qt, causal, q_ref, qa_ref, k_ref, ka_ref, v_ref, o_ref,
                                                z_ref, p_ref, tq), 0)


def _fox_query_tile(qt, causal, q_ref, qa_ref, k_ref, ka_ref, v_ref, o_ref, z_ref, p_ref, tq):
    cols = pl.ds(pl.multiple_of(qt * tq, tq), tq)
    q_heads = [jnp.concatenate([_head_rows(q_ref[0, :, cols], h, HEAD_DIM),
                                _head_rows(qa_ref[0, :, cols], h, AUG_STRIDE)], axis=0) for h in range(2)]

    def stage_scores(slot, kt):
        start = pl.multiple_of(kt * KEY_TILE, KEY_TILE)
        k = jnp.concatenate([k_ref[0, pl.ds(start, KEY_TILE), :],
                             ka_ref[0, pl.ds(start, KEY_TILE), :]], axis=1)
        for h in range(2):
            z_ref[slot, h] = _dot(k, q_heads[h])

    def stage_weights(src, dst, state, masked):
        m, l, _, _, acc = state
        m_new, l_new, alpha = [], [], []
        for h in range(2):
            logits = z_ref[src, h]
            if masked:
                logits = jnp.where(causal, logits, -jnp.inf)
            mh = jnp.maximum(m[h], jnp.max(logits, axis=0, keepdims=True))
            p = jnp.exp2(logits - mh)
            ah = jnp.exp2(m[h] - mh)
            p_ref[dst, h] = p.astype(BF16)
            m_new.append(mh)
            alpha.append(ah)
            l_new.append(ah * l[h] + jnp.sum(p, axis=0, keepdims=True))
        return tuple(m_new), tuple(l_new), l, tuple(alpha), acc

    def stage_values(slot, kt, state, valid):
        m, l, l_prev, alpha, acc = state
        new_acc = []
        for h in range(2):
            pv = _dot(v_ref[0, kt, h * HEAD_DIM:(h + 1) * HEAD_DIM, :], p_ref[slot, h])
            upd = alpha[h] * acc[h] + pv
            new_acc.append(upd if valid is None else jnp.where(valid, upd, acc[h]))
        if valid is not None:
            l = tuple(jnp.where(valid, l[h], l_prev[h]) for h in range(2))
        return m, l, l_prev, alpha, tuple(new_acc)

    row = lambda v: (jnp.full((1, tq), v, F32),) * 2
    state = (row(-jnp.inf), row(0.0), row(0.0), row(1.0), (jnp.zeros((HEAD_DIM, tq), F32),) * 2)
    _, l, _, _, acc = _tile_pipeline(qt, stage_scores, stage_weights, stage_values, state)
    o_ref[0, :, cols] = jnp.concatenate([acc[h] / l[h] for h in range(2)], axis=0)
    return 0


def _fox_attention(q_t, qaug_t, k, kaug, v_t, tq):
    b, _, s = q_t.shape
    q_spec = pl.BlockSpec((1, LANES, s), lambda i, p: (i, p, 0))
    k_spec = pl.BlockSpec((1, s, LANES), lambda i, p: (i, 0, p))
    return pl.pallas_call(
        functools.partial(_fox_kernel, tq=tq),
        grid=(b, N_PAIRS),
        in_specs=[q_spec, q_spec, k_spec, k_spec,
                  pl.BlockSpec((1, s // KEY_TILE, LANES, KEY_TILE), lambda i, p: (i, 0, p, 0))],
        out_specs=q_spec,
        out_shape=jax.ShapeDtypeStruct((b, D_GRP, s), F32),
        scratch_shapes=_attn_scratch(tq),
        compiler_params=pltpu.CompilerParams(
            dimension_semantics=("parallel", "parallel"), vmem_limit_bytes=VMEM_LIMIT),
        name="fox_attention",
    )(q_t, qaug_t, k, kaug, v_t)


def _rms_rows(x):
    return x * lax.rsqrt(jnp.mean(x * x, axis=0, keepdims=True) + EPS)


def _out_kernel(osb_ref, ofx_ref, gate_ref, x_ref, wout_ref, ada_ref, gf_ref, o_ref, *, d_model, final):
    y = jnp.concatenate([_rms_rows(osb_ref[0]), _rms_rows(ofx_ref[0])], axis=0)
    g = gate_ref[0]
    y = (y * (g / (1.0 + jnp.exp(-g)))).astype(BF16)
    gate = ada_ref[0][:, 2 * d_model:]
    out = x_ref[0] + (1.0 + gate) * _dot_tn(y, wout_ref[...])
    if final:
        out = out * lax.rsqrt(jnp.mean(out * out, axis=-1, keepdims=True) + EPS) * gf_ref[...]
    o_ref[0] = out


def _output(o_sb, o_fx, gate_t, x, wout, ada_l, g_final, ts, final):
    b, s, d = x.shape
    const = lambda shape: pl.BlockSpec(shape, lambda i, j: (0,) * len(shape))
    return pl.pallas_call(
        functools.partial(_out_kernel, d_model=d, final=final),
        grid=(b, s // ts),
        in_specs=[
            pl.BlockSpec((1, D_GRP, ts), lambda i, j: (i, 0, j)),
            pl.BlockSpec((1, D_GRP, ts), lambda i, j: (i, 0, j)),
            pl.BlockSpec((1, 2 * D_GRP, ts), lambda i, j: (i, 0, j)),
            pl.BlockSpec((1, ts, d), lambda i, j: (i, j, 0)),
            const(wout.shape),
            pl.BlockSpec((1, 1, 3 * d), lambda i, j: (i, 0, 0)),
            const((1, d)),
        ],
        out_specs=pl.BlockSpec((1, ts, d), lambda i, j: (i, j, 0)),
        out_shape=jax.ShapeDtypeStruct((b, s, d), F32),
        compiler_params=pltpu.CompilerParams(
            dimension_semantics=("parallel", "parallel"), vmem_limit_bytes=VMEM_LIMIT),
        name="output",
    )(o_sb, o_fx, gate_t, x, wout, ada_l.reshape(b, 1, 3 * d), g_final.reshape(1, d))


def kernel(x, c, w_ada, b_ada, g_norm, w_in, b_f, g_grp, w_out, g_final):
    b, s, d = x.shape
    depth = w_ada.shape[0]
    ts = min(512, s)
    tq = min(256, s)
    consts = _aug_constants()

    grp = lambda i: w_in[:, :, i * D_GRP:(i + 1) * D_GRP]
    t = lambda w: jnp.swapaxes(w, 1, 2)
    wq = jnp.stack([t(grp(0)) * (0.5 * Q_SCALE), t(grp(3)) * (LOG2E * Q_SCALE)], axis=1).astype(BF16)
    wk = jnp.stack([grp(1), grp(4)], axis=1).astype(BF16)
    wv = jnp.stack([t(grp(2)), t(grp(5))], axis=1).astype(BF16)
    wg = t(w_in[:, :, 6 * D_GRP:8 * D_GRP]).astype(BF16)
    wf = jnp.zeros((depth, d, LANES), BF16).at[:, :, :N_HEADS].set(w_in[:, :, 8 * D_GRP:].astype(BF16))
    bf = jnp.zeros((depth, 1, LANES), F32).at[:, 0, :N_HEADS].set(b_f)
    wout = (g_grp[:, :, None] * w_out).astype(BF16)

    ada = _adaln(c, w_ada, b_ada)
    for l in range(depth):
        q_sb, k_sb, v_sb, q_fx, k_fx, v_fx, gate_t, qaug, kaug = _project(
            x, ada[l], g_norm[l], wq[l], wk[l], wv[l], wg[l], wf[l], bf[l], consts, ts)
        o_sb = _sb_attention(q_sb, k_sb, v_sb, tq)
        o_fx = _fox_attention(q_fx, qaug, k_fx, kaug, v_fx, tq)
        x = _output(o_sb, o_fx, gate_t, x, wout[l], ada[l], g_final, ts, l == depth - 1)
    return x
```

```python
import functools

import numpy as np
import jax
import jax.numpy as jnp
from jax import lax
from jax.experimental import pallas as pl
from jax.experimental.pallas import tpu as pltpu

F32 = jnp.float32
BF16 = jnp.bfloat16

HEAD_DIM = 64
N_HEADS = 8
D_GRP = N_HEADS * HEAD_DIM
LANES = 128
SUBLANES = 8
N_PAIRS = D_GRP // LANES
KEY_TILE = 256
KEY_RUN = KEY_TILE // SUBLANES
AUG_STRIDE = 16
N_SPLIT = 3
EPS = 1e-6
Q_SCALE = HEAD_DIM ** -0.5
LOG2E = 1.4426950408889634
VMEM_LIMIT = 52 * 1024 * 1024


def _split_bf16(x, n):
    parts = []
    r = x
    for i in range(n):
        p = r.astype(BF16)
        parts.append(p)
        if i + 1 < n:
            r = r - p.astype(F32)
    return parts


def _dot(a, b):
    return jnp.dot(a, b, preferred_element_type=F32)


def _dot_nt(a, b):
    return lax.dot_general(a, b, (((1,), (1,)), ((), ())), preferred_element_type=F32)


def _dot_tn(a, b):
    return lax.dot_general(a, b, (((0,), (0,)), ((), ())), preferred_element_type=F32)


def _softplus(z):
    return jnp.maximum(z, 0.0) + jnp.log1p(jnp.exp(-jnp.abs(z)))


def _key_of_row(p):
    return (p & (SUBLANES - 1)) * KEY_RUN + (p >> 3)


def _ada_kernel(c_ref, w_ref, b_ref, o_ref):
    c = c_ref[...]
    c_act = c / (1.0 + jnp.exp(-c))
    w_parts = _split_bf16(w_ref[0], 2)
    acc = jnp.zeros(o_ref.shape[1:], F32)
    for cp in _split_bf16(c_act, N_SPLIT):
        for wp in w_parts:
            acc = acc + _dot(cp, wp)
    o_ref[0] = acc + b_ref[0]


def _adaln(c, w_ada, b_ada):
    depth, d, d3 = w_ada.shape
    b = c.shape[0]
    rows = SUBLANES
    tn = 1024
    c_pad = jnp.zeros((rows, d), F32).at[:b].set(c)
    out = pl.pallas_call(
        _ada_kernel,
        grid=(depth, d3 // tn),
        in_specs=[
            pl.BlockSpec((rows, d), lambda l, n: (0, 0)),
            pl.BlockSpec((1, d, tn), lambda l, n: (l, 0, n)),
            pl.BlockSpec((1, 1, tn), lambda l, n: (l, 0, n)),
        ],
        out_specs=pl.BlockSpec((1, rows, tn), lambda l, n: (l, 0, n)),
        out_shape=jax.ShapeDtypeStruct((depth, rows, d3), F32),
        compiler_params=pltpu.CompilerParams(
            dimension_semantics=("parallel", "parallel"), vmem_limit_bytes=VMEM_LIMIT),
        name="adaln",
    )(c_pad, w_ada, b_ada.reshape(depth, 1, d3))
    return out[:, :b]


def _proj_kernel(x_ref, ada_ref, gn_ref, wq_ref, wk_ref, wv_ref, wg_ref, wf_ref, bf_ref,
                 pq_ref, pk_ref,
                 qsb_ref, ksb_ref, vsb_ref, qfx_ref, kfx_ref, vfx_ref, gate_ref, qaug_ref, kaug_ref,
                 carry_ref, *, d_model):
    ts = x_ref.shape[1]
    x = x_ref[0]
    ada = ada_ref[0]
    shift = ada[:, :d_model]
    scale = ada[:, d_model:2 * d_model]
    r = lax.rsqrt(jnp.mean(x * x, axis=-1, keepdims=True) + EPS)
    h = ((x * r) * gn_ref[...] * (1.0 + scale) + shift).astype(BF16)

    row = lax.broadcasted_iota(jnp.int32, (ts, ts), 0)
    col = lax.broadcasted_iota(jnp.int32, (ts, ts), 1)
    in_tile = row & (KEY_TILE - 1)
    key_row = (row - in_tile) + _key_of_row(in_tile)
    perm = jnp.where(col == key_row, 1.0, 0.0).astype(BF16)
    h_keys = _dot(perm, h).astype(BF16)

    qsb_ref[0] = _dot_nt(wq_ref[0], h).astype(BF16)
    qfx_ref[0] = _dot_nt(wq_ref[1], h).astype(BF16)
    gate_ref[0] = _dot_nt(wg_ref[...], h)
    ksb_ref[0] = _dot(h_keys, wk_ref[0]).astype(BF16)
    kfx_ref[0] = _dot(h_keys, wk_ref[1]).astype(BF16)
    for o_ref, w in ((vsb_ref, wv_ref[0]), (vfx_ref, wv_ref[1])):
        v_t = _dot_nt(w, h_keys).astype(BF16)
        for t in range(ts // KEY_TILE):
            o_ref[0, t] = v_t[:, t * KEY_TILE:(t + 1) * KEY_TILE]

    zf = _dot(h, wf_ref[...]) + bf_ref[...]
    log_f_parts = _split_bf16(-_softplus(-zf), N_SPLIT)

    @pl.when(pl.program_id(1) == 0)
    def _():
        carry_ref[...] = jnp.zeros_like(carry_ref)

    carry = carry_ref[...]
    upto = jnp.where(col <= row, 1.0, 0.0).astype(BF16)
    upto_keys = jnp.where(col <= key_row, 1.0, 0.0).astype(BF16)
    cum = carry
    cum_keys = carry
    for part in log_f_parts:
        cum = cum + _dot(upto, part)
        cum_keys = cum_keys + _dot(upto_keys, part)
    carry_ref[...] = cum[ts - 1:ts, :]

    ones = jnp.ones((ts, LANES), BF16)
    pieces = jnp.concatenate(_split_bf16(cum * LOG2E, N_SPLIT) + [ones], axis=1)
    pieces_keys = jnp.concatenate(_split_bf16(cum_keys * LOG2E, N_SPLIT) + [ones], axis=1)
    qaug_ref[0] = _dot_nt(pq_ref[...], pieces).astype(BF16)
    kaug_ref[0] = _dot(pieces_keys, pk_ref[...]).astype(BF16)


def _aug_constants():
    pq = np.zeros((D_GRP, (N_SPLIT + 1) * LANES), np.float32)
    pk = np.zeros(((N_SPLIT + 1) * LANES, D_GRP), np.float32)
    one_col = N_SPLIT * LANES
    for h in range(N_HEADS):
        base = (h // 2) * LANES + AUG_STRIDE * (h % 2)
        for j in range(N_SPLIT):
            pq[base + j, j * LANES + h] = 1.0
            pk[one_col, base + j] = 1.0
            pk[j * LANES + h, base + N_SPLIT + j] = -1.0
            pq[base + N_SPLIT + j, one_col] = 1.0
    return jnp.asarray(pq, BF16), jnp.asarray(pk, BF16)


def _project(x, ada_l, g_norm_l, wq, wk, wv, wg, wf, bf, consts, ts):
    b, s, d = x.shape
    pq, pk = consts
    const = lambda shape: pl.BlockSpec(shape, lambda i, j: (0,) * len(shape))
    row_major = pl.BlockSpec((1, ts, D_GRP), lambda i, j: (i, j, 0))
    feat_major = pl.BlockSpec((1, D_GRP, ts), lambda i, j: (i, 0, j))
    v_spec = pl.BlockSpec((1, ts // KEY_TILE, D_GRP, KEY_TILE), lambda i, j: (i, j, 0, 0))
    row_shape = jax.ShapeDtypeStruct((b, s, D_GRP), BF16)
    feat_shape = jax.ShapeDtypeStruct((b, D_GRP, s), BF16)
    v_shape = jax.ShapeDtypeStruct((b, s // KEY_TILE, D_GRP, KEY_TILE), BF16)
    return pl.pallas_call(
        functools.partial(_proj_kernel, d_model=d),
        grid=(b, s // ts),
        in_specs=[
            pl.BlockSpec((1, ts, d), lambda i, j: (i, j, 0)),
            pl.BlockSpec((1, 1, 3 * d), lambda i, j: (i, 0, 0)),
            const((1, d)),
            const(wq.shape), const(wk.shape), const(wv.shape), const(wg.shape), const(wf.shape),
            const(bf.shape), const(pq.shape), const(pk.shape),
        ],
        out_specs=[feat_major, row_major, v_spec, feat_major, row_major, v_spec,
                   pl.BlockSpec((1, 2 * D_GRP, ts), lambda i, j: (i, 0, j)), feat_major, row_major],
        out_shape=[feat_shape, row_shape, v_shape, feat_shape, row_shape, v_shape,
                   jax.ShapeDtypeStruct((b, 2 * D_GRP, s), F32), feat_shape, row_shape],
        scratch_shapes=[pltpu.VMEM((1, LANES), F32)],
        compiler_params=pltpu.CompilerParams(
            dimension_semantics=("parallel", "arbitrary"), vmem_limit_bytes=VMEM_LIMIT),
        name="project",
    )(x, ada_l.reshape(b, 1, 3 * d), g_norm_l.reshape(1, d), wq, wk, wv, wg, wf, bf, pq, pk)


def _tile_iotas(tq):
    shape = (KEY_RUN, SUBLANES, tq)
    run = lax.broadcasted_iota(jnp.int32, shape, 0)
    sub = lax.broadcasted_iota(jnp.int32, shape, 1)
    qry = lax.broadcasted_iota(jnp.int32, shape, 2)
    return run, sub, qry


def _head_rows(block, j, width):
    row = lax.broadcasted_iota(jnp.int32, block.shape, 0)
    return jnp.where((row >= j * width) & (row < (j + 1) * width), block, jnp.zeros_like(block))


def _shift_up(x, k, fill):
    sub = lax.broadcasted_iota(jnp.int32, x.shape, 0)
    return jnp.where(sub + k < SUBLANES, pltpu.roll(x, SUBLANES - k, axis=0), fill)


def _tile_pipeline(qt, scores, park, stage_weights, stage_values, state):
    clamp = lambda kt: jnp.maximum(kt, 0)
    park(1, scores(qt))
    state = stage_weights(1, state, True)
    park(0, scores(clamp(qt - 1)))
    park(1, scores(clamp(qt - 2)))

    def body(i, state):
        t0 = qt - 1 - 2 * i
        z0 = scores(clamp(t0 - 2))
        z1 = scores(clamp(t0 - 3))
        state = stage_values(0, jnp.minimum(t0 + 2, qt), state, None)
        state = stage_values(1, t0 + 1, state, None)
        state = stage_weights(0, state, False)
        state = stage_weights(1, state, False)
        park(0, z0)
        park(1, z1)
        return state

    state = lax.fori_loop(0, (qt + 1) // 2, body, state)
    state = stage_values(0, jnp.minimum(1 - qt % 2, qt), state, None)
    return stage_values(1, 0, state, qt % 2 == 0)


def _attn_scratch(tq):
    return [pltpu.VMEM((2, 2, KEY_TILE, tq), F32), pltpu.VMEM((2, 2, KEY_TILE, tq), BF16)]


def _sb_kernel(q_ref, k_ref, v_ref, o_ref, z_ref, a_ref, *, tq):
    run, sub, qry = _tile_iotas(tq)
    causal = sub * KEY_RUN + run < qry
    lax.fori_loop(0, q_ref.shape[2] // tq,
                  lambda qt, _: _sb_query_tile(qt, causal, q_ref, k_ref, v_ref, o_ref, z_ref, a_ref, tq), 0)


def _sb_query_tile(qt, causal, q_ref, k_ref, v_ref, o_ref, z_ref, a_ref, tq):
    cols = pl.ds(pl.multiple_of(qt * tq, tq), tq)
    q_heads = [_head_rows(q_ref[0, :, cols], h, HEAD_DIM) for h in range(2)]

    def scores(kt):
        start = pl.multiple_of(kt * KEY_TILE, KEY_TILE)
        k = k_ref[0, pl.ds(start, KEY_TILE), :]
        return tuple(_dot(k, q_heads[h]) for h in range(2))

    def park(slot, z):
        for h in range(2):
            z_ref[slot, h] = z[h]

    def weights(half_z, carry, masked):
        th = jnp.tanh(half_z.reshape(KEY_RUN, SUBLANES, tq))
        rem = 0.5 - 0.5 * th
        if masked:
            rem = jnp.where(causal, rem, 1.0)
        prod = jnp.ones((SUBLANES, tq), F32)
        parts = [None] * KEY_RUN
        for r in reversed(range(KEY_RUN)):
            below = prod * rem[r]
            parts[r] = prod - below
            prod = below
        incl = prod
        for step in (1, 2, 4):
            incl = incl * _shift_up(incl, step, 1.0)
        scale = _shift_up(incl, 1, 1.0) * carry
        a = (jnp.stack(parts) * scale[None]).reshape(KEY_TILE, tq).astype(BF16)
        return a, carry * jnp.broadcast_to(incl[0:1], carry.shape)

    def stage_weights(slot, state, masked):
        carry, acc = state
        new_carry = []
        for h in range(2):
            a_ref[slot, h], c = weights(z_ref[slot, h], carry[h], masked)
            new_carry.append(c)
        return tuple(new_carry), acc

    def stage_values(slot, kt, state, valid):
        carry, acc = state
        new_acc = []
        for h in range(2):
            av = _dot(v_ref[0, kt, h * HEAD_DIM:(h + 1) * HEAD_DIM, :], a_ref[slot, h])
            new_acc.append(acc[h] + (av if valid is None else jnp.where(valid, av, 0.0)))
        return carry, tuple(new_acc)

    a_ref[0] = jnp.zeros(a_ref.shape[1:], BF16)
    state = ((jnp.ones((SUBLANES, tq), F32),) * 2, (jnp.zeros((HEAD_DIM, tq), F32),) * 2)
    _, acc = _tile_pipeline(qt, scores, park, stage_weights, stage_values, state)
    o_ref[0, :, cols] = jnp.concatenate(acc, axis=0)
    return 0


def _sb_attention(q_t, k, v_t, tq):
    b, _, s = q_t.shape
    q_spec = pl.BlockSpec((1, LANES, s), lambda i, p: (i, p, 0))
    return pl.pallas_call(
        functools.partial(_sb_kernel, tq=tq),
        grid=(b, N_PAIRS),
        in_specs=[
            q_spec,
            pl.BlockSpec((1, s, LANES), lambda i, p: (i, 0, p)),
            pl.BlockSpec((1, s // KEY_TILE, LANES, KEY_TILE), lambda i, p: (i, 0, p, 0)),
        ],
        out_specs=q_spec,
        out_shape=jax.ShapeDtypeStruct((b, D_GRP, s), F32),
        scratch_shapes=_attn_scratch(tq),
        compiler_params=pltpu.CompilerParams(
            dimension_semantics=("parallel", "parallel"), vmem_limit_bytes=VMEM_LIMIT),
        name="sb_attention",
    )(q_t, k, v_t)


def _fox_kernel(q_ref, qa_ref, k_ref, ka_ref, v_ref, o_ref, z_ref, p_ref, *, tq):
    run, sub, qry = _tile_iotas(tq)
    causal = (sub * KEY_RUN + run <= qry).reshape(KEY_TILE, tq)
    lax.fori_loop(0, q_ref.shape[2] // tq,
                  lambda qt, _: _fox_query_tile(qt, causal, q_ref, qa_ref, k_ref, ka_ref, v_ref, o_ref,
                                                z_ref, p_ref, tq), 0)


def _fox_query_tile(qt, causal, q_ref, qa_ref, k_ref, ka_ref, v_ref, o_ref, z_ref, p_ref, tq):
    cols = pl.ds(pl.multiple_of(qt * tq, tq), tq)
    q_heads = [jnp.concatenate([_head_rows(q_ref[0, :, cols], h, HEAD_DIM),
                                _head_rows(qa_ref[0, :, cols], h, AUG_STRIDE)], axis=0) for h in range(2)]

    def scores(kt):
        start = pl.multiple_of(kt * KEY_TILE, KEY_TILE)
        k = jnp.concatenate([k_ref[0, pl.ds(start, KEY_TILE), :],
                             ka_ref[0, pl.ds(start, KEY_TILE), :]], axis=1)
        return tuple(_dot(k, q_heads[h]) for h in range(2))

    def park(slot, z):
        for h in range(2):
            z_ref[slot, h] = z[h]

    def stage_weights(slot, state, masked):
        m, l, _, alpha, acc = state
        m_new, l_new, a_new = [], [], []
        for h in range(2):
            logits = z_ref[slot, h]
            if masked:
                logits = jnp.where(causal, logits, -jnp.inf)
            mh = jnp.maximum(m[h], jnp.max(logits, axis=0, keepdims=True))
            p = jnp.exp2(logits - mh)
            ah = jnp.exp2(m[h] - mh)
            p_ref[slot, h] = p.astype(BF16)
            m_new.append(mh)
            a_new.append(ah)
            l_new.append(ah * l[h] + jnp.sum(p, axis=0, keepdims=True))
        alpha = tuple(tuple(a_new) if s == slot else alpha[s] for s in range(2))
        return tuple(m_new), tuple(l_new), l, alpha, acc

    def stage_values(slot, kt, state, valid):
        m, l, l_prev, alpha, acc = state
        new_acc = []
        for h in range(2):
            pv = _dot(v_ref[0, kt, h * HEAD_DIM:(h + 1) * HEAD_DIM, :], p_ref[slot, h])
            upd = alpha[slot][h] * acc[h] + pv
            new_acc.append(upd if valid is None else jnp.where(valid, upd, acc[h]))
        if valid is not None:
            l = tuple(jnp.where(valid, l[h], l_prev[h]) for h in range(2))
        return m, l, l_prev, alpha, tuple(new_acc)

    p_ref[0] = jnp.zeros(p_ref.shape[1:], BF16)
    row = lambda v: (jnp.full((1, tq), v, F32),) * 2
    state = (row(-jnp.inf), row(0.0), row(0.0), (row(1.0),) * 2, (jnp.zeros((HEAD_DIM, tq), F32),) * 2)
    _, l, _, _, acc = _tile_pipeline(qt, scores, park, stage_weights, stage_values, state)
    o_ref[0, :, cols] = jnp.concatenate([acc[h] / l[h] for h in range(2)], axis=0)
    return 0


def _fox_attention(q_t, qaug_t, k, kaug, v_t, tq):
    b, _, s = q_t.shape
    q_spec = pl.BlockSpec((1, LANES, s), lambda i, p: (i, p, 0))
    k_spec = pl.BlockSpec((1, s, LANES), lambda i, p: (i, 0, p))
    return pl.pallas_call(
        functools.partial(_fox_kernel, tq=tq),
        grid=(b, N_PAIRS),
        in_specs=[q_spec, q_spec, k_spec, k_spec,
                  pl.BlockSpec((1, s // KEY_TILE, LANES, KEY_TILE), lambda i, p: (i, 0, p, 0))],
        out_specs=q_spec,
        out_shape=jax.ShapeDtypeStruct((b, D_GRP, s), F32),
        scratch_shapes=_attn_scratch(tq),
        compiler_params=pltpu.CompilerParams(
            dimension_semantics=("parallel", "parallel"), vmem_limit_bytes=VMEM_LIMIT),
        name="fox_attention",
    )(q_t, qaug_t, k, kaug, v_t)


def _rms_rows(x):
    return x * lax.rsqrt(jnp.mean(x * x, axis=0, keepdims=True) + EPS)


def _out_kernel(osb_ref, ofx_ref, gate_ref, x_ref, wout_ref, ada_ref, gf_ref, o_ref, *, d_model, final):
    y = jnp.concatenate([_rms_rows(osb_ref[0]), _rms_rows(ofx_ref[0])], axis=0)
    g = gate_ref[0]
    y = (y * (g / (1.0 + jnp.exp(-g)))).astype(BF16)
    gate = ada_ref[0][:, 2 * d_model:]
    out = x_ref[0] + (1.0 + gate) * _dot_tn(y, wout_ref[...])
    if final:
        out = out * lax.rsqrt(jnp.mean(out * out, axis=-1, keepdims=True) + EPS) * gf_ref[...]
    o_ref[0] = out


def _output(o_sb, o_fx, gate_t, x, wout, ada_l, g_final, ts, final):
    b, s, d = x.shape
    const = lambda shape: pl.BlockSpec(shape, lambda i, j: (0,) * len(shape))
    return pl.pallas_call(
        functools.partial(_out_kernel, d_model=d, final=final),
        grid=(b, s // ts),
        in_specs=[
            pl.BlockSpec((1, D_GRP, ts), lambda i, j: (i, 0, j)),
            pl.BlockSpec((1, D_GRP, ts), lambda i, j: (i, 0, j)),
            pl.BlockSpec((1, 2 * D_GRP, ts), lambda i, j: (i, 0, j)),
            pl.BlockSpec((1, ts, d), lambda i, j: (i, j, 0)),
            const(wout.shape),
            pl.BlockSpec((1, 1, 3 * d), lambda i, j: (i, 0, 0)),
            const((1, d)),
        ],
        out_specs=pl.BlockSpec((1, ts, d), lambda i, j: (i, j, 0)),
        out_shape=jax.ShapeDtypeStruct((b, s, d), F32),
        compiler_params=pltpu.CompilerParams(
            dimension_semantics=("parallel", "parallel"), vmem_limit_bytes=VMEM_LIMIT),
        name="output",
    )(o_sb, o_fx, gate_t, x, wout, ada_l.reshape(b, 1, 3 * d), g_final.reshape(1, d))


def kernel(x, c, w_ada, b_ada, g_norm, w_in, b_f, g_grp, w_out, g_final):
    b, s, d = x.shape
    depth = w_ada.shape[0]
    ts = min(512, s)
    tq = min(256, s)
    consts = _aug_constants()

    grp = lambda i: w_in[:, :, i * D_GRP:(i + 1) * D_GRP]
    t = lambda w: jnp.swapaxes(w, 1, 2)
    wq = jnp.stack([t(grp(0)) * (0.5 * Q_SCALE), t(grp(3)) * (LOG2E * Q_SCALE)], axis=1).astype(BF16)
    wk = jnp.stack([grp(1), grp(4)], axis=1).astype(BF16)
    wv = jnp.stack([t(grp(2)), t(grp(5))], axis=1).astype(BF16)
    wg = t(w_in[:, :, 6 * D_GRP:8 * D_GRP]).astype(BF16)
    wf = jnp.zeros((depth, d, LANES), BF16).at[:, :, :N_HEADS].set(w_in[:, :, 8 * D_GRP:].astype(BF16))
    bf = jnp.zeros((depth, 1, LANES), F32).at[:, 0, :N_HEADS].set(b_f)
    wout = (g_grp[:, :, None] * w_out).astype(BF16)

    ada = _adaln(c, w_ada, b_ada)
    for l in range(depth):
        q_sb, k_sb, v_sb, q_fx, k_fx, v_fx, gate_t, qaug, kaug = _project(
            x, ada[l], g_norm[l], wq[l], wk[l], wv[l], wg[l], wf[l], bf[l], consts, ts)
        o_sb = _sb_attention(q_sb, k_sb, v_sb, tq)
        o_fx = _fox_attention(q_fx, qaug, k_fx, kaug, v_fx, tq)
        x = _output(o_sb, o_fx, gate_t, x, wout[l], ada[l], g_final, ts, l == depth - 1)
    return x
```

```python
import functools

import numpy as np
import jax
import jax.numpy as jnp
from jax import lax
from jax.experimental import pallas as pl
from jax.experimental.pallas import tpu as pltpu

F32 = jnp.float32
BF16 = jnp.bfloat16

HEAD_DIM = 64
N_HEADS = 8
D_GRP = N_HEADS * HEAD_DIM
LANES = 128
SUBLANES = 8
N_PAIRS = D_GRP // LANES
KEY_TILE = 256
KEY_RUN = KEY_TILE // SUBLANES
AUG_STRIDE = 16
N_SPLIT = 3
EPS = 1e-6
Q_SCALE = HEAD_DIM ** -0.5
LOG2E = 1.4426950408889634
VMEM_LIMIT = 52 * 1024 * 1024


def _split_bf16(x, n):
    parts = []
    r = x
    for i in range(n):
        p = r.astype(BF16)
        parts.append(p)
        if i + 1 < n:
            r = r - p.astype(F32)
    return parts


def _dot(a, b):
    return jnp.dot(a, b, preferred_element_type=F32)


def _dot_nt(a, b):
    return lax.dot_general(a, b, (((1,), (1,)), ((), ())), preferred_element_type=F32)


def _dot_tn(a, b):
    return lax.dot_general(a, b, (((0,), (0,)), ((), ())), preferred_element_type=F32)


def _softplus(z):
    return jnp.maximum(z, 0.0) + jnp.log1p(jnp.exp(-jnp.abs(z)))


def _key_of_row(p):
    return (p & (SUBLANES - 1)) * KEY_RUN + (p >> 3)


def _ada_kernel(c_ref, w_ref, b_ref, o_ref):
    c = c_ref[...]
    c_act = c / (1.0 + jnp.exp(-c))
    w_parts = _split_bf16(w_ref[0], 2)
    acc = jnp.zeros(o_ref.shape[1:], F32)
    for cp in _split_bf16(c_act, N_SPLIT):
        for wp in w_parts:
            acc = acc + _dot(cp, wp)
    o_ref[0] = acc + b_ref[0]


def _adaln(c, w_ada, b_ada):
    depth, d, d3 = w_ada.shape
    b = c.shape[0]
    rows = SUBLANES
    tn = 1024
    c_pad = jnp.zeros((rows, d), F32).at[:b].set(c)
    out = pl.pallas_call(
        _ada_kernel,
        grid=(depth, d3 // tn),
        in_specs=[
            pl.BlockSpec((rows, d), lambda l, n: (0, 0)),
            pl.BlockSpec((1, d, tn), lambda l, n: (l, 0, n)),
            pl.BlockSpec((1, 1, tn), lambda l, n: (l, 0, n)),
        ],
        out_specs=pl.BlockSpec((1, rows, tn), lambda l, n: (l, 0, n)),
        out_shape=jax.ShapeDtypeStruct((depth, rows, d3), F32),
        compiler_params=pltpu.CompilerParams(
            dimension_semantics=("parallel", "parallel"), vmem_limit_bytes=VMEM_LIMIT),
        name="adaln",
    )(c_pad, w_ada, b_ada.reshape(depth, 1, d3))
    return out[:, :b]


def _proj_kernel(x_ref, ada_ref, gn_ref, wq_ref, wk_ref, wv_ref, wg_ref, wf_ref, bf_ref,
                 pq_ref, pk_ref,
                 qsb_ref, ksb_ref, vsb_ref, qfx_ref, kfx_ref, vfx_ref, gate_ref, qaug_ref, kaug_ref,
                 carry_ref, *, d_model):
    ts = x_ref.shape[1]
    x = x_ref[0]
    ada = ada_ref[0]
    shift = ada[:, :d_model]
    scale = ada[:, d_model:2 * d_model]
    r = lax.rsqrt(jnp.mean(x * x, axis=-1, keepdims=True) + EPS)
    h = ((x * r) * gn_ref[...] * (1.0 + scale) + shift).astype(BF16)

    row = lax.broadcasted_iota(jnp.int32, (ts, ts), 0)
    col = lax.broadcasted_iota(jnp.int32, (ts, ts), 1)
    in_tile = row & (KEY_TILE - 1)
    key_row = (row - in_tile) + _key_of_row(in_tile)
    perm = jnp.where(col == key_row, 1.0, 0.0).astype(BF16)
    h_keys = _dot(perm, h).astype(BF16)

    qsb_ref[0] = _dot_nt(wq_ref[0], h).astype(BF16)
    qfx_ref[0] = _dot_nt(wq_ref[1], h).astype(BF16)
    gate_ref[0] = _dot_nt(wg_ref[...], h)
    ksb_ref[0] = _dot(h_keys, wk_ref[0]).astype(BF16)
    kfx_ref[0] = _dot(h_keys, wk_ref[1]).astype(BF16)
    for o_ref, w in ((vsb_ref, wv_ref[0]), (vfx_ref, wv_ref[1])):
        v_t = _dot_nt(w, h_keys).astype(BF16)
        for t in range(ts // KEY_TILE):
            o_ref[0, t] = v_t[:, t * KEY_TILE:(t + 1) * KEY_TILE]

    zf = _dot(h, wf_ref[...]) + bf_ref[...]
    log_f_parts = _split_bf16(-_softplus(-zf), N_SPLIT)

    @pl.when(pl.program_id(1) == 0)
    def _():
        carry_ref[...] = jnp.zeros_like(carry_ref)

    carry = carry_ref[...]
    upto = jnp.where(col <= row, 1.0, 0.0).astype(BF16)
    upto_keys = jnp.where(col <= key_row, 1.0, 0.0).astype(BF16)
    cum = carry
    cum_keys = carry
    for part in log_f_parts:
        cum = cum + _dot(upto, part)
        cum_keys = cum_keys + _dot(upto_keys, part)
    carry_ref[...] = cum[ts - 1:ts, :]

    ones = jnp.ones((ts, LANES), BF16)
    pieces = jnp.concatenate(_split_bf16(cum * LOG2E, N_SPLIT) + [ones], axis=1)
    pieces_keys = jnp.concatenate(_split_bf16(cum_keys * LOG2E, N_SPLIT) + [ones], axis=1)
    qaug_ref[0] = _dot_nt(pq_ref[...], pieces).astype(BF16)
    kaug_ref[0] = _dot(pieces_keys, pk_ref[...]).astype(BF16)


def _aug_constants():
    pq = np.zeros((D_GRP, (N_SPLIT + 1) * LANES), np.float32)
    pk = np.zeros(((N_SPLIT + 1) * LANES, D_GRP), np.float32)
    one_col = N_SPLIT * LANES
    for h in range(N_HEADS):
        base = (h // 2) * LANES + AUG_STRIDE * (h % 2)
        for j in range(N_SPLIT):
            pq[base + j, j * LANES + h] = 1.0
            pk[one_col, base + j] = 1.0
            pk[j * LANES + h, base + N_SPLIT + j] = -1.0
            pq[base + N_SPLIT + j, one_col] = 1.0
    return jnp.asarray(pq, BF16), jnp.asarray(pk, BF16)


def _project(x, ada_l, g_norm_l, wq, wk, wv, wg, wf, bf, consts, ts):
    b, s, d = x.shape
    pq, pk = consts
    const = lambda shape: pl.BlockSpec(shape, lambda i, j: (0,) * len(shape))
    row_major = pl.BlockSpec((1, ts, D_GRP), lambda i, j: (i, j, 0))
    feat_major = pl.BlockSpec((1, D_GRP, ts), lambda i, j: (i, 0, j))
    v_spec = pl.BlockSpec((1, ts // KEY_TILE, D_GRP, KEY_TILE), lambda i, j: (i, j, 0, 0))
    row_shape = jax.ShapeDtypeStruct((b, s, D_GRP), BF16)
    feat_shape = jax.ShapeDtypeStruct((b, D_GRP, s), BF16)
    v_shape = jax.ShapeDtypeStruct((b, s // KEY_TILE, D_GRP, KEY_TILE), BF16)
    return pl.pallas_call(
        functools.partial(_proj_kernel, d_model=d),
        grid=(b, s // ts),
        in_specs=[
            pl.BlockSpec((1, ts, d), lambda i, j: (i, j, 0)),
            pl.BlockSpec((1, 1, 3 * d), lambda i, j: (i, 0, 0)),
            const((1, d)),
            const(wq.shape), const(wk.shape), const(wv.shape), const(wg.shape), const(wf.shape),
            const(bf.shape), const(pq.shape), const(pk.shape),
        ],
        out_specs=[feat_major, row_major, v_spec, feat_major, row_major, v_spec,
                   pl.BlockSpec((1, 2 * D_GRP, ts), lambda i, j: (i, 0, j)), feat_major, row_major],
        out_shape=[feat_shape, row_shape, v_shape, feat_shape, row_shape, v_shape,
                   jax.ShapeDtypeStruct((b, 2 * D_GRP, s), F32), feat_shape, row_shape],
        scratch_shapes=[pltpu.VMEM((1, LANES), F32)],
        compiler_params=pltpu.CompilerParams(
            dimension_semantics=("parallel", "arbitrary"), vmem_limit_bytes=VMEM_LIMIT),
        name="project",
    )(x, ada_l.reshape(b, 1, 3 * d), g_norm_l.reshape(1, d), wq, wk, wv, wg, wf, bf, pq, pk)


def _tile_iotas(tq):
    shape = (KEY_RUN, SUBLANES, tq)
    run = lax.broadcasted_iota(jnp.int32, shape, 0)
    sub = lax.broadcasted_iota(jnp.int32, shape, 1)
    qry = lax.broadcasted_iota(jnp.int32, shape, 2)
    return run, sub, qry


def _head_rows(block, j, width):
    row = lax.broadcasted_iota(jnp.int32, block.shape, 0)
    return jnp.where((row >= j * width) & (row < (j + 1) * width), block, jnp.zeros_like(block))


def _shift_up(x, k, fill):
    sub = lax.broadcasted_iota(jnp.int32, x.shape, 0)
    return jnp.where(sub + k < SUBLANES, pltpu.roll(x, SUBLANES - k, axis=0), fill)


def _tile_pipeline(qt, stage_scores, stage_weights, stage_values, state, alive=None):
    clamp = lambda kt: jnp.maximum(kt, 0)
    stage_scores(1, qt)
    state = stage_weights(1, 0, state, True)
    stage_scores(0, clamp(qt - 1))

    def half(slot, kt, state):
        stage_scores(1 - slot, clamp(kt - 1))
        state = stage_values(slot, kt + 1, state, None)
        return stage_weights(slot, 1 - slot, state, False)

    n_bodies = (qt + 1) // 2
    if alive is None:
        def body(i, state):
            kt = qt - 1 - 2 * i
            return half(1, kt - 1, half(0, kt, state))

        state = lax.fori_loop(0, n_bodies, body, state)
        done = n_bodies
    else:
        def body(loop):
            i, _, state = loop
            kt = qt - 1 - 2 * i
            state = half(0, kt, state)
            go = alive(state)
            return i + 1, go, half(1, kt - 1, state)

        done, _, state = lax.while_loop(lambda loop: (loop[0] < n_bodies) & loop[1], body,
                                        (jnp.int32(0), alive(state), state))
    last = qt - 2 * done
    return stage_values(0, clamp(last), state, last >= 0)


def _attn_scratch(tq):
    return [pltpu.VMEM((2, 2, KEY_TILE, tq), F32), pltpu.VMEM((2, 2, KEY_TILE, tq), BF16)]


def _sb_kernel(q_ref, k_ref, v_ref, o_ref, z_ref, a_ref, *, tq):
    run, sub, qry = _tile_iotas(tq)
    causal = sub * KEY_RUN + run < qry
    lax.fori_loop(0, q_ref.shape[2] // tq,
                  lambda qt, _: _sb_query_tile(qt, causal, q_ref, k_ref, v_ref, o_ref, z_ref, a_ref, tq), 0)


def _sb_query_tile(qt, causal, q_ref, k_ref, v_ref, o_ref, z_ref, a_ref, tq):
    cols = pl.ds(pl.multiple_of(qt * tq, tq), tq)
    q_heads = [_head_rows(q_ref[0, :, cols], h, HEAD_DIM) for h in range(2)]

    def stage_scores(slot, kt):
        start = pl.multiple_of(kt * KEY_TILE, KEY_TILE)
        k = k_ref[0, pl.ds(start, KEY_TILE), :]
        for h in range(2):
            z_ref[slot, h] = _dot(k, q_heads[h])

    def weights(half_z, carry, masked):
        th = jnp.tanh(half_z.reshape(KEY_RUN, SUBLANES, tq))
        rem = 0.5 - 0.5 * th
        if masked:
            rem = jnp.where(causal, rem, 1.0)
        prod = jnp.ones((SUBLANES, tq), F32)
        parts = [None] * KEY_RUN
        for r in reversed(range(KEY_RUN)):
            below = prod * rem[r]
            parts[r] = prod - below
            prod = below
        incl = prod
        for step in (1, 2, 4):
            incl = incl * _shift_up(incl, step, 1.0)
        scale = _shift_up(incl, 1, 1.0) * carry
        a = (jnp.stack(parts) * scale[None]).reshape(KEY_TILE, tq).astype(BF16)
        return a, carry * jnp.broadcast_to(incl[0:1], carry.shape)

    def stage_weights(src, dst, state, masked):
        carry, acc = state
        new_carry = []
        for h in range(2):
            a_ref[dst, h], c = weights(z_ref[src, h], carry[h], masked)
            new_carry.append(c)
        return tuple(new_carry), acc

    def stage_values(slot, kt, state, valid):
        carry, acc = state
        new_acc = []
        for h in range(2):
            av = _dot(v_ref[0, kt, h * HEAD_DIM:(h + 1) * HEAD_DIM, :], a_ref[slot, h])
            new_acc.append(acc[h] + (av if valid is None else jnp.where(valid, av, 0.0)))
        return carry, tuple(new_acc)

    state = ((jnp.ones((SUBLANES, tq), F32),) * 2, (jnp.zeros((HEAD_DIM, tq), F32),) * 2)
    alive = lambda st: jnp.max(jnp.maximum(st[0][0], st[0][1])) > 0.0
    _, acc = _tile_pipeline(qt, stage_scores, stage_weights, stage_values, state, alive)
    o_ref[0, :, cols] = jnp.concatenate(acc, axis=0)
    return 0


def _sb_attention(q_t, k, v_t, tq):
    b, _, s = q_t.shape
    q_spec = pl.BlockSpec((1, LANES, s), lambda i, p: (i, p, 0))
    return pl.pallas_call(
        functools.partial(_sb_kernel, tq=tq),
        grid=(b, N_PAIRS),
        in_specs=[
            q_spec,
            pl.BlockSpec((1, s, LANES), lambda i, p: (i, 0, p)),
            pl.BlockSpec((1, s // KEY_TILE, LANES, KEY_TILE), lambda i, p: (i, 0, p, 0)),
        ],
        out_specs=q_spec,
        out_shape=jax.ShapeDtypeStruct((b, D_GRP, s), F32),
        scratch_shapes=_attn_scratch(tq),
        compiler_params=pltpu.CompilerParams(
            dimension_semantics=("parallel", "parallel"), vmem_limit_bytes=VMEM_LIMIT),
        name="sb_attention",
    )(q_t, k, v_t)


def _fox_kernel(q_ref, qa_ref, k_ref, ka_ref, v_ref, o_ref, z_ref, p_ref, *, tq):
    run, sub, qry = _tile_iotas(tq)
    causal = (sub * KEY_RUN + run <= qry).reshape(KEY_TILE, tq)
    lax.fori_loop(0, q_ref.shape[2] // tq,
                  lambda qt, _: _fox_query_tile(qt, causal, q_ref, qa_ref, k_ref, ka_ref, v_ref, o_ref,
                                                z_ref, p_ref, tq), 0)


def _fox_query_tile(qt, causal, q_ref, qa_ref, k_ref, ka_ref, v_ref, o_ref, z_ref, p_ref, tq):
    cols = pl.ds(pl.multiple_of(qt * tq, tq), tq)
    q_heads = [jnp.concatenate([_head_rows(q_ref[0, :, cols], h, HEAD_DIM),
                                _head_rows(qa_ref[0, :, cols], h, AUG_STRIDE)], axis=0) for h in range(2)]

    def stage_scores(slot, kt):
        start = pl.multiple_of(kt * KEY_TILE, KEY_TILE)
        k = jnp.concatenate([k_ref[0, pl.ds(start, KEY_TILE), :],
                             ka_ref[0, pl.ds(start, KEY_TILE), :]], axis=1)
        for h in range(2):
            z_ref[slot, h] = _dot(k, q_heads[h])

    def stage_weights(src, dst, state, masked):
        m, l, _, _, acc = state
        m_new, l_new, alpha = [], [], []
        for h in range(2):
            logits = z_ref[src, h]
            if masked:
                logits = jnp.where(causal, logits, -jnp.inf)
            mh = jnp.maximum(m[h], jnp.max(logits, axis=0, keepdims=True))
            p = jnp.exp2(logits - mh)
            ah = jnp.exp2(m[h] - mh)
            p_ref[dst, h] = p.astype(BF16)
            m_new.append(mh)
            alpha.append(ah)
            l_new.append(ah * l[h] + jnp.sum(p, axis=0, keepdims=True))
        return tuple(m_new), tuple(l_new), l, tuple(alpha), acc

    def stage_values(slot, kt, state, valid):
        m, l, l_prev, alpha, acc = state
        new_acc = []
        for h in range(2):
            pv = _dot(v_ref[0, kt, h * HEAD_DIM:(h + 1) * HEAD_DIM, :], p_ref[slot, h])
            upd = alpha[h] * acc[h] + pv
            new_acc.append(upd if valid is None else jnp.where(valid, upd, acc[h]))
        if valid is not None:
            l = tuple(jnp.where(valid, l[h], l_prev[h]) for h in range(2))
        return m, l, l_prev, alpha, tuple(new_acc)

    row = lambda v: (jnp.full((1, tq), v, F32),) * 2
    state = (row(-jnp.inf), row(0.0), row(0.0), row(1.0), (jnp.zeros((HEAD_DIM, tq), F32),) * 2)
    _, l, _, _, acc = _tile_pipeline(qt, stage_scores, stage_weights, stage_values, state)
    o_ref[0, :, cols] = jnp.concatenate([acc[h] / l[h] for h in range(2)], axis=0)
    return 0


def _fox_attention(q_t, qaug_t, k, kaug, v_t, tq):
    b, _, s = q_t.shape
    q_spec = pl.BlockSpec((1, LANES, s), lambda i, p: (i, p, 0))
    k_spec = pl.BlockSpec((1, s, LANES), lambda i, p: (i, 0, p))
    return pl.pallas_call(
        functools.partial(_fox_kernel, tq=tq),
        grid=(b, N_PAIRS),
        in_specs=[q_spec, q_spec, k_spec, k_spec,
                  pl.BlockSpec((1, s // KEY_TILE, LANES, KEY_TILE), lambda i, p: (i, 0, p, 0))],
        out_specs=q_spec,
        out_shape=jax.ShapeDtypeStruct((b, D_GRP, s), F32),
        scratch_shapes=_attn_scratch(tq),
        compiler_params=pltpu.CompilerParams(
            dimension_semantics=("parallel", "parallel"), vmem_limit_bytes=VMEM_LIMIT),
        name="fox_attention",
    )(q_t, qaug_t, k, kaug, v_t)


def _rms_rows(x):
    return x * lax.rsqrt(jnp.mean(x * x, axis=0, keepdims=True) + EPS)


def _out_kernel(osb_ref, ofx_ref, gate_ref, x_ref, wout_ref, ada_ref, gf_ref, o_ref, *, d_model, final):
    y = jnp.concatenate([_rms_rows(osb_ref[0]), _rms_rows(ofx_ref[0])], axis=0)
    g = gate_ref[0]
    y = (y * (g / (1.0 + jnp.exp(-g)))).astype(BF16)
    gate = ada_ref[0][:, 2 * d_model:]
    out = x_ref[0] + (1.0 + gate) * _dot_tn(y, wout_ref[...])
    if final:
        out = out * lax.rsqrt(jnp.mean(out * out, axis=-1, keepdims=True) + EPS) * gf_ref[...]
    o_ref[0] = out


def _output(o_sb, o_fx, gate_t, x, wout, ada_l, g_final, ts, final):
    b, s, d = x.shape
    const = lambda shape: pl.BlockSpec(shape, lambda i, j: (0,) * len(shape))
    return pl.pallas_call(
        functools.partial(_out_kernel, d_model=d, final=final),
        grid=(b, s // ts),
        in_specs=[
            pl.BlockSpec((1, D_GRP, ts), lambda i, j: (i, 0, j)),
            pl.BlockSpec((1, D_GRP, ts), lambda i, j: (i, 0, j)),
            pl.BlockSpec((1, 2 * D_GRP, ts), lambda i, j: (i, 0, j)),
            pl.BlockSpec((1, ts, d), lambda i, j: (i, j, 0)),
            const(wout.shape),
            pl.BlockSpec((1, 1, 3 * d), lambda i, j: (i, 0, 0)),
            const((1, d)),
        ],
        out_specs=pl.BlockSpec((1, ts, d), lambda i, j: (i, j, 0)),
        out_shape=jax.ShapeDtypeStruct((b, s, d), F32),
        compiler_params=pltpu.CompilerParams(
            dimension_semantics=("parallel", "parallel"), vmem_limit_bytes=VMEM_LIMIT),
        name="output",
    )(o_sb, o_fx, gate_t, x, wout, ada_l.reshape(b, 1, 3 * d), g_final.reshape(1, d))


def kernel(x, c, w_ada, b_ada, g_norm, w_in, b_f, g_grp, w_out, g_final):
    b, s, d = x.shape
    depth = w_ada.shape[0]
    ts = min(512, s)
    tq = min(256, s)
    consts = _aug_constants()

    grp = lambda i: w_in[:, :, i * D_GRP:(i + 1) * D_GRP]
    t = lambda w: jnp.swapaxes(w, 1, 2)
    wq = jnp.stack([t(grp(0)) * (0.5 * Q_SCALE), t(grp(3)) * (LOG2E * Q_SCALE)], axis=1).astype(BF16)
    wk = jnp.stack([grp(1), grp(4)], axis=1).astype(BF16)
    wv = jnp.stack([t(grp(2)), t(grp(5))], axis=1).astype(BF16)
    wg = t(w_in[:, :, 6 * D_GRP:8 * D_GRP]).astype(BF16)
    wf = jnp.zeros((depth, d, LANES), BF16).at[:, :, :N_HEADS].set(w_in[:, :, 8 * D_GRP:].astype(BF16))
    bf = jnp.zeros((depth, 1, LANES), F32).at[:, 0, :N_HEADS].set(b_f)
    wout = (g_grp[:, :, None] * w_out).astype(BF16)

    ada = _adaln(c, w_ada, b_ada)
    for l in range(depth):
        q_sb, k_sb, v_sb, q_fx, k_fx, v_fx, gate_t, qaug, kaug = _project(
            x, ada[l], g_norm[l], wq[l], wk[l], wv[l], wg[l], wf[l], bf[l], consts, ts)
        o_sb = _sb_attention(q_sb, k_sb, v_sb, tq)
        o_fx = _fox_attention(q_fx, qaug, k_fx, kaug, v_fx, tq)
        x = _output(o_sb, o_fx, gate_t, x, wout[l], ada[l], g_final, ts, l == depth - 1)
    return x
```

```python
import functools

import numpy as np
import jax
import jax.numpy as jnp
from jax import lax
from jax.experimental import pallas as pl
from jax.experimental.pallas import tpu as pltpu

F32 = jnp.float32
BF16 = jnp.bfloat16

HEAD_DIM = 64
N_HEADS = 8
D_GRP = N_HEADS * HEAD_DIM
LANES = 128
SUBLANES = 8
N_PAIRS = D_GRP // LANES
KEY_TILE = 256
KEY_RUN = KEY_TILE // SUBLANES
AUG_STRIDE = 16
N_SPLIT = 3
EPS = 1e-6
Q_SCALE = HEAD_DIM ** -0.5
LOG2E = 1.4426950408889634
VMEM_LIMIT = 52 * 1024 * 1024
UNDERFLOW_BITS = 160.0


def _split_bf16(x, n):
    parts = []
    r = x
    for i in range(n):
        p = r.astype(BF16)
        parts.append(p)
        if i + 1 < n:
            r = r - p.astype(F32)
    return parts


def _dot(a, b):
    return jnp.dot(a, b, preferred_element_type=F32)


def _dot_nt(a, b):
    return lax.dot_general(a, b, (((1,), (1,)), ((), ())), preferred_element_type=F32)


def _dot_tn(a, b):
    return lax.dot_general(a, b, (((0,), (0,)), ((), ())), preferred_element_type=F32)


def _softplus(z):
    return jnp.maximum(z, 0.0) + jnp.log1p(jnp.exp(-jnp.abs(z)))


def _key_of_row(p):
    return (p & (SUBLANES - 1)) * KEY_RUN + (p >> 3)


def _ada_kernel(c_ref, w_ref, b_ref, o_ref):
    c = c_ref[...]
    c_act = c / (1.0 + jnp.exp(-c))
    w_parts = _split_bf16(w_ref[0], 2)
    acc = jnp.zeros(o_ref.shape[1:], F32)
    for cp in _split_bf16(c_act, N_SPLIT):
        for wp in w_parts:
            acc = acc + _dot(cp, wp)
    o_ref[0] = acc + b_ref[0]


def _adaln(c, w_ada, b_ada):
    depth, d, d3 = w_ada.shape
    b = c.shape[0]
    rows = SUBLANES
    tn = 1024
    c_pad = jnp.zeros((rows, d), F32).at[:b].set(c)
    out = pl.pallas_call(
        _ada_kernel,
        grid=(depth, d3 // tn),
        in_specs=[
            pl.BlockSpec((rows, d), lambda l, n: (0, 0)),
            pl.BlockSpec((1, d, tn), lambda l, n: (l, 0, n)),
            pl.BlockSpec((1, 1, tn), lambda l, n: (l, 0, n)),
        ],
        out_specs=pl.BlockSpec((1, rows, tn), lambda l, n: (l, 0, n)),
        out_shape=jax.ShapeDtypeStruct((depth, rows, d3), F32),
        compiler_params=pltpu.CompilerParams(
            dimension_semantics=("parallel", "parallel"), vmem_limit_bytes=VMEM_LIMIT),
        name="adaln",
    )(c_pad, w_ada, b_ada.reshape(depth, 1, d3))
    return out[:, :b]


def _proj_kernel(x_ref, ada_ref, gn_ref, wq_ref, wk_ref, wv_ref, wg_ref, wf_ref, bf_ref,
                 pq_ref, pk_ref,
                 qsb_ref, ksb_ref, vsb_ref, qfx_ref, kfx_ref, vfx_ref, gate_ref, qaug_ref, kaug_ref,
                 carry_ref, *, d_model):
    ts = x_ref.shape[1]
    x = x_ref[0]
    ada = ada_ref[0]
    shift = ada[:, :d_model]
    scale = ada[:, d_model:2 * d_model]
    r = lax.rsqrt(jnp.mean(x * x, axis=-1, keepdims=True) + EPS)
    h = ((x * r) * gn_ref[...] * (1.0 + scale) + shift).astype(BF16)

    row = lax.broadcasted_iota(jnp.int32, (ts, ts), 0)
    col = lax.broadcasted_iota(jnp.int32, (ts, ts), 1)
    in_tile = row & (KEY_TILE - 1)
    key_row = (row - in_tile) + _key_of_row(in_tile)
    perm = jnp.where(col == key_row, 1.0, 0.0).astype(BF16)
    h_keys = _dot(perm, h).astype(BF16)

    qsb_ref[0] = _dot_nt(wq_ref[0], h).astype(BF16)
    qfx_ref[0] = _dot_nt(wq_ref[1], h).astype(BF16)
    gate_ref[0] = _dot_nt(wg_ref[...], h)
    ksb_ref[0] = _dot(h_keys, wk_ref[0]).astype(BF16)
    kfx_ref[0] = _dot(h_keys, wk_ref[1]).astype(BF16)
    for o_ref, w in ((vsb_ref, wv_ref[0]), (vfx_ref, wv_ref[1])):
        v_t = _dot_nt(w, h_keys).astype(BF16)
        for t in range(ts // KEY_TILE):
            o_ref[0, t] = v_t[:, t * KEY_TILE:(t + 1) * KEY_TILE]

    zf = _dot(h, wf_ref[...]) + bf_ref[...]
    log_f_parts = _split_bf16(-_softplus(-zf), N_SPLIT)

    @pl.when(pl.program_id(1) == 0)
    def _():
        carry_ref[...] = jnp.zeros_like(carry_ref)

    carry = carry_ref[...]
    upto = jnp.where(col <= row, 1.0, 0.0).astype(BF16)
    upto_keys = jnp.where(col <= key_row, 1.0, 0.0).astype(BF16)
    cum = carry
    cum_keys = carry
    for part in log_f_parts:
        cum = cum + _dot(upto, part)
        cum_keys = cum_keys + _dot(upto_keys, part)
    carry_ref[...] = cum[ts - 1:ts, :]

    ones = jnp.ones((ts, LANES), BF16)
    pieces = jnp.concatenate(_split_bf16(cum * LOG2E, N_SPLIT) + [ones], axis=1)
    pieces_keys = jnp.concatenate(_split_bf16(cum_keys * LOG2E, N_SPLIT) + [ones], axis=1)
    qaug_ref[0] = _dot_nt(pq_ref[...], pieces).astype(BF16)
    kaug_ref[0] = _dot(pieces_keys, pk_ref[...]).astype(BF16)


def _aug_constants():
    pq = np.zeros((D_GRP, (N_SPLIT + 1) * LANES), np.float32)
    pk = np.zeros(((N_SPLIT + 1) * LANES, D_GRP), np.float32)
    one_col = N_SPLIT * LANES
    for h in range(N_HEADS):
        base = (h // 2) * LANES + AUG_STRIDE * (h % 2)
        for j in range(N_SPLIT):
            pq[base + j, j * LANES + h] = 1.0
            pk[one_col, base + j] = 1.0
            pk[j * LANES + h, base + N_SPLIT + j] = -1.0
            pq[base + N_SPLIT + j, one_col] = 1.0
    return jnp.asarray(pq, BF16), jnp.asarray(pk, BF16)


def _project(x, ada_l, g_norm_l, wq, wk, wv, wg, wf, bf, consts, ts):
    b, s, d = x.shape
    pq, pk = consts
    const = lambda shape: pl.BlockSpec(shape, lambda i, j: (0,) * len(shape))
    row_major = pl.BlockSpec((1, ts, D_GRP), lambda i, j: (i, j, 0))
    feat_major = pl.BlockSpec((1, D_GRP, ts), lambda i, j: (i, 0, j))
    v_spec = pl.BlockSpec((1, ts // KEY_TILE, D_GRP, KEY_TILE), lambda i, j: (i, j, 0, 0))
    row_shape = jax.ShapeDtypeStruct((b, s, D_GRP), BF16)
    feat_shape = jax.ShapeDtypeStruct((b, D_GRP, s), BF16)
    v_shape = jax.ShapeDtypeStruct((b, s // KEY_TILE, D_GRP, KEY_TILE), BF16)
    return pl.pallas_call(
        functools.partial(_proj_kernel, d_model=d),
        grid=(b, s // ts),
        in_specs=[
            pl.BlockSpec((1, ts, d), lambda i, j: (i, j, 0)),
            pl.BlockSpec((1, 1, 3 * d), lambda i, j: (i, 0, 0)),
            const((1, d)),
            const(wq.shape), const(wk.shape), const(wv.shape), const(wg.shape), const(wf.shape),
            const(bf.shape), const(pq.shape), const(pk.shape),
        ],
        out_specs=[feat_major, row_major, v_spec, feat_major, row_major, v_spec,
                   pl.BlockSpec((1, 2 * D_GRP, ts), lambda i, j: (i, 0, j)), feat_major, row_major],
        out_shape=[feat_shape, row_shape, v_shape, feat_shape, row_shape, v_shape,
                   jax.ShapeDtypeStruct((b, 2 * D_GRP, s), F32), feat_shape, row_shape],
        scratch_shapes=[pltpu.VMEM((1, LANES), F32)],
        compiler_params=pltpu.CompilerParams(
            dimension_semantics=("parallel", "arbitrary"), vmem_limit_bytes=VMEM_LIMIT),
        name="project",
    )(x, ada_l.reshape(b, 1, 3 * d), g_norm_l.reshape(1, d), wq, wk, wv, wg, wf, bf, pq, pk)


def _tile_iotas(tq):
    shape = (KEY_RUN, SUBLANES, tq)
    run = lax.broadcasted_iota(jnp.int32, shape, 0)
    sub = lax.broadcasted_iota(jnp.int32, shape, 1)
    qry = lax.broadcasted_iota(jnp.int32, shape, 2)
    return run, sub, qry


def _head_rows(block, j, width):
    row = lax.broadcasted_iota(jnp.int32, block.shape, 0)
    return jnp.where((row >= j * width) & (row < (j + 1) * width), block, jnp.zeros_like(block))


def _shift_up(x, k, fill):
    sub = lax.broadcasted_iota(jnp.int32, x.shape, 0)
    return jnp.where(sub + k < SUBLANES, pltpu.roll(x, SUBLANES - k, axis=0), fill)


def _tile_pipeline(qt, stage_scores, stage_weights, stage_values, state, alive=None):
    clamp = lambda kt: jnp.maximum(kt, 0)
    stage_scores(1, qt)
    state = stage_weights(1, 0, state, True)
    stage_scores(0, clamp(qt - 1))

    def half(slot, kt, state):
        stage_scores(1 - slot, clamp(kt - 1))
        state = stage_values(slot, kt + 1, state, None)
        return stage_weights(slot, 1 - slot, state, False)

    n_bodies = (qt + 1) // 2
    if alive is None:
        def body(i, state):
            kt = qt - 1 - 2 * i
            return half(1, kt - 1, half(0, kt, state))

        state = lax.fori_loop(0, n_bodies, body, state)
        done = n_bodies
    else:
        def body(loop):
            i, _, state = loop
            kt = qt - 1 - 2 * i
            state = half(0, kt, state)
            go = alive(state, kt - 1)
            return i + 1, go, half(1, kt - 1, state)

        done, _, state = lax.while_loop(lambda loop: (loop[0] < n_bodies) & loop[1], body,
                                        (jnp.int32(0), alive(state, qt), state))
    last = qt - 2 * done
    return stage_values(0, clamp(last), state, last >= 0)


def _attn_scratch(tq):
    return [pltpu.VMEM((2, 2, KEY_TILE, tq), F32), pltpu.VMEM((2, 2, KEY_TILE, tq), BF16)]


def _sb_kernel(q_ref, k_ref, v_ref, o_ref, z_ref, a_ref, *, tq):
    run, sub, qry = _tile_iotas(tq)
    causal = sub * KEY_RUN + run < qry
    lax.fori_loop(0, q_ref.shape[2] // tq,
                  lambda qt, _: _sb_query_tile(qt, causal, q_ref, k_ref, v_ref, o_ref, z_ref, a_ref, tq), 0)


def _sb_query_tile(qt, causal, q_ref, k_ref, v_ref, o_ref, z_ref, a_ref, tq):
    cols = pl.ds(pl.multiple_of(qt * tq, tq), tq)
    q_heads = [_head_rows(q_ref[0, :, cols], h, HEAD_DIM) for h in range(2)]

    def stage_scores(slot, kt):
        start = pl.multiple_of(kt * KEY_TILE, KEY_TILE)
        k = k_ref[0, pl.ds(start, KEY_TILE), :]
        for h in range(2):
            z_ref[slot, h] = _dot(k, q_heads[h])

    def weights(half_z, carry, masked):
        th = jnp.tanh(half_z.reshape(KEY_RUN, SUBLANES, tq))
        rem = 0.5 - 0.5 * th
        if masked:
            rem = jnp.where(causal, rem, 1.0)
        prod = jnp.ones((SUBLANES, tq), F32)
        parts = [None] * KEY_RUN
        for r in reversed(range(KEY_RUN)):
            below = prod * rem[r]
            parts[r] = prod - below
            prod = below
        incl = prod
        for step in (1, 2, 4):
            incl = incl * _shift_up(incl, step, 1.0)
        scale = _shift_up(incl, 1, 1.0) * carry
        a = (jnp.stack(parts) * scale[None]).reshape(KEY_TILE, tq).astype(BF16)
        return a, carry * jnp.broadcast_to(incl[0:1], carry.shape)

    def stage_weights(src, dst, state, masked):
        carry, acc = state
        new_carry = []
        for h in range(2):
            a_ref[dst, h], c = weights(z_ref[src, h], carry[h], masked)
            new_carry.append(c)
        return tuple(new_carry), acc

    def stage_values(slot, kt, state, valid):
        carry, acc = state
        new_acc = []
        for h in range(2):
            av = _dot(v_ref[0, kt, h * HEAD_DIM:(h + 1) * HEAD_DIM, :], a_ref[slot, h])
            new_acc.append(acc[h] + (av if valid is None else jnp.where(valid, av, 0.0)))
        return carry, tuple(new_acc)

    state = ((jnp.ones((SUBLANES, tq), F32),) * 2, (jnp.zeros((HEAD_DIM, tq), F32),) * 2)
    alive = lambda st, kt: jnp.max(jnp.maximum(st[0][0], st[0][1])) > 0.0
    _, acc = _tile_pipeline(qt, stage_scores, stage_weights, stage_values, state, alive)
    o_ref[0, :, cols] = jnp.concatenate(acc, axis=0)
    return 0


def _sb_attention(q_t, k, v_t, tq):
    b, _, s = q_t.shape
    q_spec = pl.BlockSpec((1, LANES, s), lambda i, p: (i, p, 0))
    return pl.pallas_call(
        functools.partial(_sb_kernel, tq=tq),
        grid=(b, N_PAIRS),
        in_specs=[
            q_spec,
            pl.BlockSpec((1, s, LANES), lambda i, p: (i, 0, p)),
            pl.BlockSpec((1, s // KEY_TILE, LANES, KEY_TILE), lambda i, p: (i, 0, p, 0)),
        ],
        out_specs=q_spec,
        out_shape=jax.ShapeDtypeStruct((b, D_GRP, s), F32),
        scratch_shapes=_attn_scratch(tq),
        compiler_params=pltpu.CompilerParams(
            dimension_semantics=("parallel", "parallel"), vmem_limit_bytes=VMEM_LIMIT),
        name="sb_attention",
    )(q_t, k, v_t)


def _fox_kernel(q_ref, qa_ref, k_ref, ka_ref, v_ref, o_ref, z_ref, p_ref, *, tq):
    run, sub, qry = _tile_iotas(tq)
    causal = (sub * KEY_RUN + run <= qry).reshape(KEY_TILE, tq)

    def widen(t, k_abs):
        k = k_ref[0, pl.ds(pl.multiple_of(t * KEY_TILE, KEY_TILE), KEY_TILE), :]
        return jnp.maximum(k_abs, jnp.max(jnp.abs(k.astype(F32)), axis=0, keepdims=True))

    k_abs = lax.fori_loop(0, k_ref.shape[1] // KEY_TILE, widen, jnp.zeros((1, LANES), F32))
    k_bound = [HEAD_DIM ** 0.5 * jnp.max(k_abs[:, h * HEAD_DIM:(h + 1) * HEAD_DIM], axis=1, keepdims=True)
               for h in range(2)]
    lax.fori_loop(0, q_ref.shape[2] // tq,
                  lambda qt, _: _fox_query_tile(qt, causal, k_bound, q_ref, qa_ref, k_ref, ka_ref, v_ref,
                                                o_ref, z_ref, p_ref, tq), 0)


def _fox_query_tile(qt, causal, k_bound, q_ref, qa_ref, k_ref, ka_ref, v_ref, o_ref, z_ref, p_ref, tq):
    cols = pl.ds(pl.multiple_of(qt * tq, tq), tq)
    q_heads = [jnp.concatenate([_head_rows(q_ref[0, :, cols], h, HEAD_DIM),
                                _head_rows(qa_ref[0, :, cols], h, AUG_STRIDE)], axis=0) for h in range(2)]

    def stage_scores(slot, kt):
        start = pl.multiple_of(kt * KEY_TILE, KEY_TILE)
        k = jnp.concatenate([k_ref[0, pl.ds(start, KEY_TILE), :],
                             ka_ref[0, pl.ds(start, KEY_TILE), :]], axis=1)
        for h in range(2):
            z_ref[slot, h] = _dot(k, q_heads[h])

    def stage_weights(src, dst, state, masked):
        m, l, _, _, acc = state
        m_new, l_new, alpha = [], [], []
        for h in range(2):
            logits = z_ref[src, h]
            if masked:
                logits = jnp.where(causal, logits, -jnp.inf)
            mh = jnp.maximum(m[h], jnp.max(logits, axis=0, keepdims=True))
            p = jnp.exp2(logits - mh)
            ah = jnp.exp2(m[h] - mh)
            p_ref[dst, h] = p.astype(BF16)
            m_new.append(mh)
            alpha.append(ah)
            l_new.append(ah * l[h] + jnp.sum(p, axis=0, keepdims=True))
        return tuple(m_new), tuple(l_new), l, tuple(alpha), acc

    def stage_values(slot, kt, state, valid):
        m, l, l_prev, alpha, acc = state
        new_acc = []
        for h in range(2):
            pv = _dot(v_ref[0, kt, h * HEAD_DIM:(h + 1) * HEAD_DIM, :], p_ref[slot, h])
            upd = alpha[h] * acc[h] + pv
            new_acc.append(upd if valid is None else jnp.where(valid, upd, acc[h]))
        if valid is not None:
            l = tuple(jnp.where(valid, l[h], l_prev[h]) for h in range(2))
        return m, l, l_prev, alpha, tuple(new_acc)

    q32 = q_ref[0, :, cols].astype(F32)
    qa32 = qa_ref[0, :, cols].astype(F32)
    reach = []
    for h in range(2):
        q_norm = jnp.sqrt(jnp.sum(jnp.square(q32[h * HEAD_DIM:(h + 1) * HEAD_DIM]), axis=0, keepdims=True))
        cum_q = jnp.sum(qa32[h * AUG_STRIDE:h * AUG_STRIDE + N_SPLIT], axis=0, keepdims=True)
        reach.append(q_norm * k_bound[h] + cum_q)

    def alive(state, kt):
        start = pl.multiple_of(jnp.maximum(kt - 1, 0) * KEY_TILE + KEY_TILE - 2 * SUBLANES, 2 * SUBLANES)
        tail = ka_ref[0, pl.ds(start, 2 * SUBLANES), :].astype(F32)[2 * SUBLANES - 1:, :]
        m = state[0]
        margin = []
        for h in range(2):
            lo = h * AUG_STRIDE + N_SPLIT
            neg_cum_k = jnp.sum(tail[:, lo:lo + N_SPLIT], axis=1, keepdims=True)
            margin.append(jnp.max(reach[h] + neg_cum_k - m[h]))
        return jnp.maximum(margin[0], margin[1]) > -UNDERFLOW_BITS

    row = lambda v: (jnp.full((1, tq), v, F32),) * 2
    state = (row(-jnp.inf), row(0.0), row(0.0), row(1.0), (jnp.zeros((HEAD_DIM, tq), F32),) * 2)
    _, l, _, _, acc = _tile_pipeline(qt, stage_scores, stage_weights, stage_values, state, alive)
    o_ref[0, :, cols] = jnp.concatenate([acc[h] / l[h] for h in range(2)], axis=0)
    return 0


def _fox_attention(q_t, qaug_t, k, kaug, v_t, tq):
    b, _, s = q_t.shape
    q_spec = pl.BlockSpec((1, LANES, s), lambda i, p: (i, p, 0))
    k_spec = pl.BlockSpec((1, s, LANES), lambda i, p: (i, 0, p))
    return pl.pallas_call(
        functools.partial(_fox_kernel, tq=tq),
        grid=(b, N_PAIRS),
        in_specs=[q_spec, q_spec, k_spec, k_spec,
                  pl.BlockSpec((1, s // KEY_TILE, LANES, KEY_TILE), lambda i, p: (i, 0, p, 0))],
        out_specs=q_spec,
        out_shape=jax.ShapeDtypeStruct((b, D_GRP, s), F32),
        scratch_shapes=_attn_scratch(tq),
        compiler_params=pltpu.CompilerParams(
            dimension_semantics=("parallel", "parallel"), vmem_limit_bytes=VMEM_LIMIT),
        name="fox_attention",
    )(q_t, qaug_t, k, kaug, v_t)


def _rms_rows(x):
    return x * lax.rsqrt(jnp.mean(x * x, axis=0, keepdims=True) + EPS)


def _out_kernel(osb_ref, ofx_ref, gate_ref, x_ref, wout_ref, ada_ref, gf_ref, o_ref, *, d_model, final):
    y = jnp.concatenate([_rms_rows(osb_ref[0]), _rms_rows(ofx_ref[0])], axis=0)
    g = gate_ref[0]
    y = (y * (g / (1.0 + jnp.exp(-g)))).astype(BF16)
    gate = ada_ref[0][:, 2 * d_model:]
    out = x_ref[0] + (1.0 + gate) * _dot_tn(y, wout_ref[...])
    if final:
        out = out * lax.rsqrt(jnp.mean(out * out, axis=-1, keepdims=True) + EPS) * gf_ref[...]
    o_ref[0] = out


def _output(o_sb, o_fx, gate_t, x, wout, ada_l, g_final, ts, final):
    b, s, d = x.shape
    const = lambda shape: pl.BlockSpec(shape, lambda i, j: (0,) * len(shape))
    return pl.pallas_call(
        functools.partial(_out_kernel, d_model=d, final=final),
        grid=(b, s // ts),
        in_specs=[
            pl.BlockSpec((1, D_GRP, ts), lambda i, j: (i, 0, j)),
            pl.BlockSpec((1, D_GRP, ts), lambda i, j: (i, 0, j)),
            pl.BlockSpec((1, 2 * D_GRP, ts), lambda i, j: (i, 0, j)),
            pl.BlockSpec((1, ts, d), lambda i, j: (i, j, 0)),
            const(wout.shape),
            pl.BlockSpec((1, 1, 3 * d), lambda i, j: (i, 0, 0)),
            const((1, d)),
        ],
        out_specs=pl.BlockSpec((1, ts, d), lambda i, j: (i, j, 0)),
        out_shape=jax.ShapeDtypeStruct((b, s, d), F32),
        compiler_params=pltpu.CompilerParams(
            dimension_semantics=("parallel", "parallel"), vmem_limit_bytes=VMEM_LIMIT),
        name="output",
    )(o_sb, o_fx, gate_t, x, wout, ada_l.reshape(b, 1, 3 * d), g_final.reshape(1, d))


def kernel(x, c, w_ada, b_ada, g_norm, w_in, b_f, g_grp, w_out, g_final):
    b, s, d = x.shape
    depth = w_ada.shape[0]
    ts = min(512, s)
    tq = min(256, s)
    consts = _aug_constants()

    grp = lambda i: w_in[:, :, i * D_GRP:(i + 1) * D_GRP]
    t = lambda w: jnp.swapaxes(w, 1, 2)
    wq = jnp.stack([t(grp(0)) * (0.5 * Q_SCALE), t(grp(3)) * (LOG2E * Q_SCALE)], axis=1).astype(BF16)
    wk = jnp.stack([grp(1), grp(4)], axis=1).astype(BF16)
    wv = jnp.stack([t(grp(2)), t(grp(5))], axis=1).astype(BF16)
    wg = t(w_in[:, :, 6 * D_GRP:8 * D_GRP]).astype(BF16)
    wf = jnp.zeros((depth, d, LANES), BF16).at[:, :, :N_HEADS].set(w_in[:, :, 8 * D_GRP:].astype(BF16))
    bf = jnp.zeros((depth, 1, LANES), F32).at[:, 0, :N_HEADS].set(b_f)
    wout = (g_grp[:, :, None] * w_out).astype(BF16)

    ada = _adaln(c, w_ada, b_ada)
    for l in range(depth):
        q_sb, k_sb, v_sb, q_fx, k_fx, v_fx, gate_t, qaug, kaug = _project(
            x, ada[l], g_norm[l], wq[l], wk[l], wv[l], wg[l], wf[l], bf[l], consts, ts)
        o_sb = _sb_attention(q_sb, k_sb, v_sb, tq)
        o_fx = _fox_attention(q_fx, qaug, k_fx, kaug, v_fx, tq)
        x = _output(o_sb, o_fx, gate_t, x, wout[l], ada[l], g_final, ts, l == depth - 1)
    return x
```

```python
import functools

import numpy as np
import jax
import jax.numpy as jnp
from jax import lax
from jax.experimental import pallas as pl
from jax.experimental.pallas import tpu as pltpu

F32 = jnp.float32
BF16 = jnp.bfloat16

HEAD_DIM = 64
N_HEADS = 8
D_GRP = N_HEADS * HEAD_DIM
LANES = 128
SUBLANES = 8
N_PAIRS = D_GRP // LANES
KEY_TILE = 256
KEY_RUN = KEY_TILE // SUBLANES
AUG_STRIDE = 16
N_SPLIT = 3
EPS = 1e-6
Q_SCALE = HEAD_DIM ** -0.5
LOG2E = 1.4426950408889634
VMEM_LIMIT = 52 * 1024 * 1024
UNDERFLOW_BITS = 160.0


def _split_bf16(x, n):
    parts = []
    r = x
    for i in range(n):
        p = r.astype(BF16)
        parts.append(p)
        if i + 1 < n:
            r = r - p.astype(F32)
    return parts


def _dot(a, b):
    return jnp.dot(a, b, preferred_element_type=F32)


def _dot_nt(a, b):
    return lax.dot_general(a, b, (((1,), (1,)), ((), ())), preferred_element_type=F32)


def _dot_tn(a, b):
    return lax.dot_general(a, b, (((0,), (0,)), ((), ())), preferred_element_type=F32)


def _softplus(z):
    return jnp.maximum(z, 0.0) + jnp.log1p(jnp.exp(-jnp.abs(z)))


def _key_of_row(p):
    return (p & (SUBLANES - 1)) * KEY_RUN + (p >> 3)


def _ada_kernel(c_ref, w_ref, b_ref, o_ref):
    c = c_ref[...]
    c_act = c / (1.0 + jnp.exp(-c))
    w_parts = _split_bf16(w_ref[0], 2)
    acc = jnp.zeros(o_ref.shape[1:], F32)
    for cp in _split_bf16(c_act, N_SPLIT):
        for wp in w_parts:
            acc = acc + _dot(cp, wp)
    o_ref[0] = acc + b_ref[0]


def _adaln(c, w_ada, b_ada):
    depth, d, d3 = w_ada.shape
    b = c.shape[0]
    rows = SUBLANES
    tn = 1024
    c_pad = jnp.zeros((rows, d), F32).at[:b].set(c)
    out = pl.pallas_call(
        _ada_kernel,
        grid=(depth, d3 // tn),
        in_specs=[
            pl.BlockSpec((rows, d), lambda l, n: (0, 0)),
            pl.BlockSpec((1, d, tn), lambda l, n: (l, 0, n)),
            pl.BlockSpec((1, 1, tn), lambda l, n: (l, 0, n)),
        ],
        out_specs=pl.BlockSpec((1, rows, tn), lambda l, n: (l, 0, n)),
        out_shape=jax.ShapeDtypeStruct((depth, rows, d3), F32),
        compiler_params=pltpu.CompilerParams(
            dimension_semantics=("parallel", "parallel"), vmem_limit_bytes=VMEM_LIMIT),
        name="adaln",
    )(c_pad, w_ada, b_ada.reshape(depth, 1, d3))
    return out[:, :b]


def _proj_kernel(x_ref, ada_ref, gn_ref, wq_ref, wk_ref, wv_ref, wg_ref, wf_ref, bf_ref,
                 pq_ref, pk_ref,
                 qsb_ref, ksb_ref, vsb_ref, qfx_ref, kfx_ref, vfx_ref, gate_ref, qaug_ref, kaug_ref,
                 carry_ref, *, d_model):
    ts = x_ref.shape[1]
    x = x_ref[0]
    ada = ada_ref[0]
    shift = ada[:, :d_model]
    scale = ada[:, d_model:2 * d_model]
    r = lax.rsqrt(jnp.mean(x * x, axis=-1, keepdims=True) + EPS)
    h = ((x * r) * gn_ref[...] * (1.0 + scale) + shift).astype(BF16)

    row = lax.broadcasted_iota(jnp.int32, (ts, ts), 0)
    col = lax.broadcasted_iota(jnp.int32, (ts, ts), 1)
    in_tile = row & (KEY_TILE - 1)
    key_row = (row - in_tile) + _key_of_row(in_tile)
    perm = jnp.where(col == key_row, 1.0, 0.0).astype(BF16)
    h_keys = _dot(perm, h).astype(BF16)

    qsb_ref[0] = _dot_nt(wq_ref[0], h).astype(BF16)
    qfx_ref[0] = _dot_nt(wq_ref[1], h).astype(BF16)
    gate_ref[0] = _dot_nt(wg_ref[...], h)
    ksb_ref[0] = _dot(h_keys, wk_ref[0]).astype(BF16)
    kfx_ref[0] = _dot(h_keys, wk_ref[1]).astype(BF16)
    for o_ref, w in ((vsb_ref, wv_ref[0]), (vfx_ref, wv_ref[1])):
        v_t = _dot_nt(w, h_keys).astype(BF16)
        for t in range(ts // KEY_TILE):
            o_ref[0, t] = v_t[:, t * KEY_TILE:(t + 1) * KEY_TILE]

    zf = _dot(h, wf_ref[...]) + bf_ref[...]
    log_f_parts = _split_bf16(-_softplus(-zf), N_SPLIT)

    @pl.when(pl.program_id(1) == 0)
    def _():
        carry_ref[...] = jnp.zeros_like(carry_ref)

    carry = carry_ref[...]
    upto = jnp.where(col <= row, 1.0, 0.0).astype(BF16)
    upto_keys = jnp.where(col <= key_row, 1.0, 0.0).astype(BF16)
    cum = carry
    cum_keys = carry
    for part in log_f_parts:
        cum = cum + _dot(upto, part)
        cum_keys = cum_keys + _dot(upto_keys, part)
    carry_ref[...] = cum[ts - 1:ts, :]

    ones = jnp.ones((ts, LANES), BF16)
    pieces = jnp.concatenate(_split_bf16(cum * LOG2E, N_SPLIT) + [ones], axis=1)
    pieces_keys = jnp.concatenate(_split_bf16(cum_keys * LOG2E, N_SPLIT) + [ones], axis=1)
    qaug_ref[0] = _dot_nt(pq_ref[...], pieces).astype(BF16)
    kaug_ref[0] = _dot(pieces_keys, pk_ref[...]).astype(BF16)


def _aug_constants():
    pq = np.zeros((D_GRP, (N_SPLIT + 1) * LANES), np.float32)
    pk = np.zeros(((N_SPLIT + 1) * LANES, D_GRP), np.float32)
    one_col = N_SPLIT * LANES
    for h in range(N_HEADS):
        base = (h // 2) * LANES + AUG_STRIDE * (h % 2)
        for j in range(N_SPLIT):
            pq[base + j, j * LANES + h] = 1.0
            pk[one_col, base + j] = 1.0
            pk[j * LANES + h, base + N_SPLIT + j] = -1.0
            pq[base + N_SPLIT + j, one_col] = 1.0
    return jnp.asarray(pq, BF16), jnp.asarray(pk, BF16)


def _project(x, ada_l, g_norm_l, wq, wk, wv, wg, wf, bf, consts, ts):
    b, s, d = x.shape
    pq, pk = consts
    const = lambda shape: pl.BlockSpec(shape, lambda i, j: (0,) * len(shape))
    row_major = pl.BlockSpec((1, ts, D_GRP), lambda i, j: (i, j, 0))
    feat_major = pl.BlockSpec((1, D_GRP, ts), lambda i, j: (i, 0, j))
    v_spec = pl.BlockSpec((1, ts // KEY_TILE, D_GRP, KEY_TILE), lambda i, j: (i, j, 0, 0))
    row_shape = jax.ShapeDtypeStruct((b, s, D_GRP), BF16)
    feat_shape = jax.ShapeDtypeStruct((b, D_GRP, s), BF16)
    v_shape = jax.ShapeDtypeStruct((b, s // KEY_TILE, D_GRP, KEY_TILE), BF16)
    return pl.pallas_call(
        functools.partial(_proj_kernel, d_model=d),
        grid=(b, s // ts),
        in_specs=[
            pl.BlockSpec((1, ts, d), lambda i, j: (i, j, 0)),
            pl.BlockSpec((1, 1, 3 * d), lambda i, j: (i, 0, 0)),
            const((1, d)),
            const(wq.shape), const(wk.shape), const(wv.shape), const(wg.shape), const(wf.shape),
            const(bf.shape), const(pq.shape), const(pk.shape),
        ],
        out_specs=[feat_major, row_major, v_spec, feat_major, row_major, v_spec,
                   pl.BlockSpec((1, 2 * D_GRP, ts), lambda i, j: (i, 0, j)), feat_major, row_major],
        out_shape=[feat_shape, row_shape, v_shape, feat_shape, row_shape, v_shape,
                   jax.ShapeDtypeStruct((b, 2 * D_GRP, s), F32), feat_shape, row_shape],
        scratch_shapes=[pltpu.VMEM((1, LANES), F32)],
        compiler_params=pltpu.CompilerParams(
            dimension_semantics=("parallel", "arbitrary"), vmem_limit_bytes=VMEM_LIMIT),
        name="project",
    )(x, ada_l.reshape(b, 1, 3 * d), g_norm_l.reshape(1, d), wq, wk, wv, wg, wf, bf, pq, pk)


def _tile_iotas(tq):
    shape = (KEY_RUN, SUBLANES, tq)
    run = lax.broadcasted_iota(jnp.int32, shape, 0)
    sub = lax.broadcasted_iota(jnp.int32, shape, 1)
    qry = lax.broadcasted_iota(jnp.int32, shape, 2)
    return run, sub, qry


def _head_rows(block, j, width):
    row = lax.broadcasted_iota(jnp.int32, block.shape, 0)
    return jnp.where((row >= j * width) & (row < (j + 1) * width), block, jnp.zeros_like(block))


def _shift_up(x, k, fill):
    sub = lax.broadcasted_iota(jnp.int32, x.shape, 0)
    return jnp.where(sub + k < SUBLANES, pltpu.roll(x, SUBLANES - k, axis=0), fill)


def _tile_pipeline(qt, stage_scores, stage_weights, stage_values, state, alive=None, mask_first=True):
    clamp = lambda kt: jnp.maximum(kt, 0)
    stage_scores(1, qt)
    state = stage_weights(1, 0, state, mask_first)
    stage_scores(0, clamp(qt - 1))

    def half(slot, kt, state):
        stage_scores(1 - slot, clamp(kt - 1))
        state = stage_values(slot, kt + 1, state, None)
        return stage_weights(slot, 1 - slot, state, False)

    n_bodies = (qt + 1) // 2
    if alive is None:
        def body(i, state):
            kt = qt - 1 - 2 * i
            return half(1, kt - 1, half(0, kt, state))

        state = lax.fori_loop(0, n_bodies, body, state)
        done = n_bodies
    else:
        def body(loop):
            i, _, state = loop
            kt = qt - 1 - 2 * i
            state = half(0, kt, state)
            go = alive(state, kt - 1)
            return i + 1, go, half(1, kt - 1, state)

        done, _, state = lax.while_loop(lambda loop: (loop[0] < n_bodies) & loop[1], body,
                                        (jnp.int32(0), alive(state, qt), state))
    last = qt - 2 * done
    return stage_values(0, clamp(last), state, last >= 0)


def _attn_scratch(tq):
    return [pltpu.VMEM((2, 2, KEY_TILE, tq), F32), pltpu.VMEM((2, 2, KEY_TILE, tq), BF16)]


def _sb_kernel(q_ref, k_ref, v_ref, o_ref, z_ref, a_ref, *, tq):
    run, sub, qry = _tile_iotas(tq)
    causal = sub * KEY_RUN + run < qry
    lax.fori_loop(0, q_ref.shape[2] // tq,
                  lambda qt, _: _sb_query_tile(qt, causal, q_ref, k_ref, v_ref, o_ref, z_ref, a_ref, tq), 0)


def _sb_query_tile(qt, causal, q_ref, k_ref, v_ref, o_ref, z_ref, a_ref, tq):
    cols = pl.ds(pl.multiple_of(qt * tq, tq), tq)
    q_heads = [_head_rows(q_ref[0, :, cols], h, HEAD_DIM) for h in range(2)]

    def stage_scores(slot, kt):
        start = pl.multiple_of(kt * KEY_TILE, KEY_TILE)
        k = k_ref[0, pl.ds(start, KEY_TILE), :]
        for h in range(2):
            z_ref[slot, h] = _dot(k, q_heads[h])

    def weights(half_z, carry, masked):
        th = jnp.tanh(half_z.reshape(KEY_RUN, SUBLANES, tq))
        rem = 0.5 - 0.5 * th
        if masked:
            rem = jnp.where(causal, rem, 1.0)
        prod = jnp.ones((SUBLANES, tq), F32)
        parts = [None] * KEY_RUN
        for r in reversed(range(KEY_RUN)):
            below = prod * rem[r]
            parts[r] = prod - below
            prod = below
        incl = prod
        for step in (1, 2, 4):
            incl = incl * _shift_up(incl, step, 1.0)
        scale = _shift_up(incl, 1, 1.0) * carry
        a = (jnp.stack(parts) * scale[None]).reshape(KEY_TILE, tq).astype(BF16)
        return a, carry * jnp.broadcast_to(incl[0:1], carry.shape)

    def stage_weights(src, dst, state, masked):
        carry, acc = state
        new_carry = []
        for h in range(2):
            a_ref[dst, h], c = weights(z_ref[src, h], carry[h], masked)
            new_carry.append(c)
        return tuple(new_carry), acc

    def stage_values(slot, kt, state, valid):
        carry, acc = state
        new_acc = []
        for h in range(2):
            av = _dot(v_ref[0, kt, h * HEAD_DIM:(h + 1) * HEAD_DIM, :], a_ref[slot, h])
            new_acc.append(acc[h] + (av if valid is None else jnp.where(valid, av, 0.0)))
        return carry, tuple(new_acc)

    state = ((jnp.ones((SUBLANES, tq), F32),) * 2, (jnp.zeros((HEAD_DIM, tq), F32),) * 2)
    alive = lambda st, kt: jnp.max(jnp.maximum(st[0][0], st[0][1])) > 0.0

    before = jnp.maximum(qt - 1, 0)
    stage_scores(1, qt)
    stage_scores(0, before)
    state = stage_weights(1, 1, state, True)
    state = stage_weights(0, 0, state, False)
    state = stage_values(1, qt, state, None)
    state = stage_values(0, before, state, qt >= 1)

    def rest(state):
        return _tile_pipeline(qt - 2, stage_scores, stage_weights, stage_values, state, alive,
                              mask_first=False)

    _, acc = lax.cond((qt >= 2) & alive(state, qt - 1), rest, lambda st: st, state)
    o_ref[0, :, cols] = jnp.concatenate(acc, axis=0)
    return 0


def _sb_attention(q_t, k, v_t, tq):
    b, _, s = q_t.shape
    q_spec = pl.BlockSpec((1, LANES, s), lambda i, p: (i, p, 0))
    return pl.pallas_call(
        functools.partial(_sb_kernel, tq=tq),
        grid=(b, N_PAIRS),
        in_specs=[
            q_spec,
            pl.BlockSpec((1, s, LANES), lambda i, p: (i, 0, p)),
            pl.BlockSpec((1, s // KEY_TILE, LANES, KEY_TILE), lambda i, p: (i, 0, p, 0)),
        ],
        out_specs=q_spec,
        out_shape=jax.ShapeDtypeStruct((b, D_GRP, s), F32),
        scratch_shapes=_attn_scratch(tq),
        compiler_params=pltpu.CompilerParams(
            dimension_semantics=("parallel", "parallel"), vmem_limit_bytes=VMEM_LIMIT),
        name="sb_attention",
    )(q_t, k, v_t)


def _fox_kernel(q_ref, qa_ref, k_ref, ka_ref, v_ref, o_ref, z_ref, p_ref, *, tq):
    run, sub, qry = _tile_iotas(tq)
    causal = (sub * KEY_RUN + run <= qry).reshape(KEY_TILE, tq)

    def widen(t, k_abs):
        k = k_ref[0, pl.ds(pl.multiple_of(t * KEY_TILE, KEY_TILE), KEY_TILE), :]
        return jnp.maximum(k_abs, jnp.max(jnp.abs(k.astype(F32)), axis=0, keepdims=True))

    k_abs = lax.fori_loop(0, k_ref.shape[1] // KEY_TILE, widen, jnp.zeros((1, LANES), F32))
    k_bound = [HEAD_DIM ** 0.5 * jnp.max(k_abs[:, h * HEAD_DIM:(h + 1) * HEAD_DIM], axis=1, keepdims=True)
               for h in range(2)]
    lax.fori_loop(0, q_ref.shape[2] // tq,
                  lambda qt, _: _fox_query_tile(qt, causal, k_bound, q_ref, qa_ref, k_ref, ka_ref, v_ref,
                                                o_ref, z_ref, p_ref, tq), 0)


def _fox_query_tile(qt, causal, k_bound, q_ref, qa_ref, k_ref, ka_ref, v_ref, o_ref, z_ref, p_ref, tq):
    cols = pl.ds(pl.multiple_of(qt * tq, tq), tq)
    q_heads = [jnp.concatenate([_head_rows(q_ref[0, :, cols], h, HEAD_DIM),
                                _head_rows(qa_ref[0, :, cols], h, AUG_STRIDE)], axis=0) for h in range(2)]

    def stage_scores(slot, kt):
        start = pl.multiple_of(kt * KEY_TILE, KEY_TILE)
        k = jnp.concatenate([k_ref[0, pl.ds(start, KEY_TILE), :],
                             ka_ref[0, pl.ds(start, KEY_TILE), :]], axis=1)
        for h in range(2):
            z_ref[slot, h] = _dot(k, q_heads[h])

    def stage_weights(src, dst, state, masked):
        m, l, _, _, acc = state
        m_new, l_new, alpha = [], [], []
        for h in range(2):
            logits = z_ref[src, h]
            if masked:
                logits = jnp.where(causal, logits, -jnp.inf)
            mh = jnp.maximum(m[h], jnp.max(logits, axis=0, keepdims=True))
            p = jnp.exp2(logits - mh)
            ah = jnp.exp2(m[h] - mh)
            p_ref[dst, h] = p.astype(BF16)
            m_new.append(mh)
            alpha.append(ah)
            l_new.append(ah * l[h] + jnp.sum(p, axis=0, keepdims=True))
        return tuple(m_new), tuple(l_new), l, tuple(alpha), acc

    def stage_values(slot, kt, state, valid):
        m, l, l_prev, alpha, acc = state
        new_acc = []
        for h in range(2):
            pv = _dot(v_ref[0, kt, h * HEAD_DIM:(h + 1) * HEAD_DIM, :], p_ref[slot, h])
            upd = alpha[h] * acc[h] + pv
            new_acc.append(upd if valid is None else jnp.where(valid, upd, acc[h]))
        if valid is not None:
            l = tuple(jnp.where(valid, l[h], l_prev[h]) for h in range(2))
        return m, l, l_prev, alpha, tuple(new_acc)

    q32 = q_ref[0, :, cols].astype(F32)
    qa32 = qa_ref[0, :, cols].astype(F32)
    reach = []
    for h in range(2):
        q_norm = jnp.sqrt(jnp.sum(jnp.square(q32[h * HEAD_DIM:(h + 1) * HEAD_DIM]), axis=0, keepdims=True))
        cum_q = jnp.sum(qa32[h * AUG_STRIDE:h * AUG_STRIDE + N_SPLIT], axis=0, keepdims=True)
        reach.append(q_norm * k_bound[h] + cum_q)

    def alive(state, kt):
        start = pl.multiple_of(jnp.maximum(kt - 1, 0) * KEY_TILE + KEY_TILE - 2 * SUBLANES, 2 * SUBLANES)
        tail = ka_ref[0, pl.ds(start, 2 * SUBLANES), :].astype(F32)[2 * SUBLANES - 1:, :]
        m = state[0]
        margin = []
        for h in range(2):
            lo = h * AUG_STRIDE + N_SPLIT
            neg_cum_k = jnp.sum(tail[:, lo:lo + N_SPLIT], axis=1, keepdims=True)
            margin.append(jnp.max(reach[h] + neg_cum_k - m[h]))
        return jnp.maximum(margin[0], margin[1]) > -UNDERFLOW_BITS

    row = lambda v: (jnp.full((1, tq), v, F32),) * 2
    state = (row(-jnp.inf), row(0.0), row(0.0), row(1.0), (jnp.zeros((HEAD_DIM, tq), F32),) * 2)
    _, l, _, _, acc = _tile_pipeline(qt, stage_scores, stage_weights, stage_values, state, alive)
    o_ref[0, :, cols] = jnp.concatenate([acc[h] / l[h] for h in range(2)], axis=0)
    return 0


def _fox_attention(q_t, qaug_t, k, kaug, v_t, tq):
    b, _, s = q_t.shape
    q_spec = pl.BlockSpec((1, LANES, s), lambda i, p: (i, p, 0))
    k_spec = pl.BlockSpec((1, s, LANES), lambda i, p: (i, 0, p))
    return pl.pallas_call(
        functools.partial(_fox_kernel, tq=tq),
        grid=(b, N_PAIRS),
        in_specs=[q_spec, q_spec, k_spec, k_spec,
                  pl.BlockSpec((1, s // KEY_TILE, LANES, KEY_TILE), lambda i, p: (i, 0, p, 0))],
        out_specs=q_spec,
        out_shape=jax.ShapeDtypeStruct((b, D_GRP, s), F32),
        scratch_shapes=_attn_scratch(tq),
        compiler_params=pltpu.CompilerParams(
            dimension_semantics=("parallel", "parallel"), vmem_limit_bytes=VMEM_LIMIT),
        name="fox_attention",
    )(q_t, qaug_t, k, kaug, v_t)


def _rms_rows(x):
    return x * lax.rsqrt(jnp.mean(x * x, axis=0, keepdims=True) + EPS)


def _out_kernel(osb_ref, ofx_ref, gate_ref, x_ref, wout_ref, ada_ref, gf_ref, o_ref, *, d_model, final):
    y = jnp.concatenate([_rms_rows(osb_ref[0]), _rms_rows(ofx_ref[0])], axis=0)
    g = gate_ref[0]
    y = (y * (g / (1.0 + jnp.exp(-g)))).astype(BF16)
    gate = ada_ref[0][:, 2 * d_model:]
    out = x_ref[0] + (1.0 + gate) * _dot_tn(y, wout_ref[...])
    if final:
        out = out * lax.rsqrt(jnp.mean(out * out, axis=-1, keepdims=True) + EPS) * gf_ref[...]
    o_ref[0] = out


def _output(o_sb, o_fx, gate_t, x, wout, ada_l, g_final, ts, final):
    b, s, d = x.shape
    const = lambda shape: pl.BlockSpec(shape, lambda i, j: (0,) * len(shape))
    return pl.pallas_call(
        functools.partial(_out_kernel, d_model=d, final=final),
        grid=(b, s // ts),
        in_specs=[
            pl.BlockSpec((1, D_GRP, ts), lambda i, j: (i, 0, j)),
            pl.BlockSpec((1, D_GRP, ts), lambda i, j: (i, 0, j)),
            pl.BlockSpec((1, 2 * D_GRP, ts), lambda i, j: (i, 0, j)),
            pl.BlockSpec((1, ts, d), lambda i, j: (i, j, 0)),
            const(wout.shape),
            pl.BlockSpec((1, 1, 3 * d), lambda i, j: (i, 0, 0)),
            const((1, d)),
        ],
        out_specs=pl.BlockSpec((1, ts, d), lambda i, j: (i, j, 0)),
        out_shape=jax.ShapeDtypeStruct((b, s, d), F32),
        compiler_params=pltpu.CompilerParams(
            dimension_semantics=("parallel", "parallel"), vmem_limit_bytes=VMEM_LIMIT),
        name="output",
    )(o_sb, o_fx, gate_t, x, wout, ada_l.reshape(b, 1, 3 * d), g_final.reshape(1, d))


def kernel(x, c, w_ada, b_ada, g_norm, w_in, b_f, g_grp, w_out, g_final):
    b, s, d = x.shape
    depth = w_ada.shape[0]
    ts = min(512, s)
    tq = min(256, s)
    consts = _aug_constants()

    grp = lambda i: w_in[:, :, i * D_GRP:(i + 1) * D_GRP]
    t = lambda w: jnp.swapaxes(w, 1, 2)
    wq = jnp.stack([t(grp(0)) * (0.5 * Q_SCALE), t(grp(3)) * (LOG2E * Q_SCALE)], axis=1).astype(BF16)
    wk = jnp.stack([grp(1), grp(4)], axis=1).astype(BF16)
    wv = jnp.stack([t(grp(2)), t(grp(5))], axis=1).astype(BF16)
    wg = t(w_in[:, :, 6 * D_GRP:8 * D_GRP]).astype(BF16)
    wf = jnp.zeros((depth, d, LANES), BF16).at[:, :, :N_HEADS].set(w_in[:, :, 8 * D_GRP:].astype(BF16))
    bf = jnp.zeros((depth, 1, LANES), F32).at[:, 0, :N_HEADS].set(b_f)
    wout = (g_grp[:, :, None] * w_out).astype(BF16)

    ada = _adaln(c, w_ada, b_ada)
    for l in range(depth):
        q_sb, k_sb, v_sb, q_fx, k_fx, v_fx, gate_t, qaug, kaug = _project(
            x, ada[l], g_norm[l], wq[l], wk[l], wv[l], wg[l], wf[l], bf[l], consts, ts)
        o_sb = _sb_attention(q_sb, k_sb, v_sb, tq)
        o_fx = _fox_attention(q_fx, qaug, k_fx, kaug, v_fx, tq)
        x = _output(o_sb, o_fx, gate_t, x, wout[l], ada[l], g_final, ts, l == depth - 1)
    return x
```

```python
import functools

import numpy as np
import jax
import jax.numpy as jnp
from jax import lax
from jax.experimental import pallas as pl
from jax.experimental.pallas import tpu as pltpu

F32 = jnp.float32
BF16 = jnp.bfloat16

HEAD_DIM = 64
N_HEADS = 8
D_GRP = N_HEADS * HEAD_DIM
LANES = 128
SUBLANES = 8
N_PAIRS = D_GRP // LANES
KEY_TILE = 256
KEY_RUN = KEY_TILE // SUBLANES
AUG_STRIDE = 16
N_SPLIT = 3
EPS = 1e-6
Q_SCALE = HEAD_DIM ** -0.5
LOG2E = 1.4426950408889634
VMEM_LIMIT = 52 * 1024 * 1024
UNDERFLOW_BITS = 160.0


def _split_bf16(x, n):
    parts = []
    r = x
    for i in range(n):
        p = r.astype(BF16)
        parts.append(p)
        if i + 1 < n:
            r = r - p.astype(F32)
    return parts


def _dot(a, b):
    return jnp.dot(a, b, preferred_element_type=F32)


def _dot_nt(a, b):
    return lax.dot_general(a, b, (((1,), (1,)), ((), ())), preferred_element_type=F32)


def _dot_tn(a, b):
    return lax.dot_general(a, b, (((0,), (0,)), ((), ())), preferred_element_type=F32)


def _softplus(z):
    return jnp.maximum(z, 0.0) + jnp.log1p(jnp.exp(-jnp.abs(z)))


def _key_of_row(p):
    return (p & (SUBLANES - 1)) * KEY_RUN + (p >> 3)


def _ada_kernel(c_ref, w_ref, b_ref, o_ref):
    c = c_ref[...]
    c_act = c / (1.0 + jnp.exp(-c))
    w_parts = _split_bf16(w_ref[0], 2)
    acc = jnp.zeros(o_ref.shape[1:], F32)
    for cp in _split_bf16(c_act, N_SPLIT):
        for wp in w_parts:
            acc = acc + _dot(cp, wp)
    o_ref[0] = acc + b_ref[0]


def _adaln(c, w_ada, b_ada):
    depth, d, d3 = w_ada.shape
    b = c.shape[0]
    rows = SUBLANES
    tn = 1024
    c_pad = jnp.zeros((rows, d), F32).at[:b].set(c)
    out = pl.pallas_call(
        _ada_kernel,
        grid=(depth, d3 // tn),
        in_specs=[
            pl.BlockSpec((rows, d), lambda l, n: (0, 0)),
            pl.BlockSpec((1, d, tn), lambda l, n: (l, 0, n)),
            pl.BlockSpec((1, 1, tn), lambda l, n: (l, 0, n)),
        ],
        out_specs=pl.BlockSpec((1, rows, tn), lambda l, n: (l, 0, n)),
        out_shape=jax.ShapeDtypeStruct((depth, rows, d3), F32),
        compiler_params=pltpu.CompilerParams(
            dimension_semantics=("parallel", "parallel"), vmem_limit_bytes=VMEM_LIMIT),
        name="adaln",
    )(c_pad, w_ada, b_ada.reshape(depth, 1, d3))
    return out[:, :b]


def _proj_kernel(x_ref, ada_ref, gn_ref, wq_ref, wk_ref, wv_ref, wg_ref, wf_ref, bf_ref,
                 pq_ref, pk_ref,
                 qsb_ref, ksb_ref, vsb_ref, qfx_ref, kfx_ref, vfx_ref, gate_ref, qaug_ref, kaug_ref,
                 carry_ref, *, d_model):
    ts = x_ref.shape[1]
    x = x_ref[0]
    ada = ada_ref[0]
    shift = ada[:, :d_model]
    scale = ada[:, d_model:2 * d_model]
    r = lax.rsqrt(jnp.mean(x * x, axis=-1, keepdims=True) + EPS)
    h = ((x * r) * gn_ref[...] * (1.0 + scale) + shift).astype(BF16)

    row = lax.broadcasted_iota(jnp.int32, (ts, ts), 0)
    col = lax.broadcasted_iota(jnp.int32, (ts, ts), 1)
    in_tile = row & (KEY_TILE - 1)
    key_row = (row - in_tile) + _key_of_row(in_tile)
    perm = jnp.where(col == key_row, 1.0, 0.0).astype(BF16)
    h_keys = _dot(perm, h).astype(BF16)

    qsb_ref[0] = _dot_nt(wq_ref[0], h).astype(BF16)
    qfx_ref[0] = _dot_nt(wq_ref[1], h).astype(BF16)
    gate_ref[0] = _dot_nt(wg_ref[...], h)
    ksb_ref[0] = _dot(h_keys, wk_ref[0]).astype(BF16)
    kfx_ref[0] = _dot(h_keys, wk_ref[1]).astype(BF16)
    for o_ref, w in ((vsb_ref, wv_ref[0]), (vfx_ref, wv_ref[1])):
        v_t = _dot_nt(w, h_keys).astype(BF16)
        for t in range(ts // KEY_TILE):
            o_ref[0, t] = v_t[:, t * KEY_TILE:(t + 1) * KEY_TILE]

    zf = _dot(h, wf_ref[...]) + bf_ref[...]
    log_f_parts = _split_bf16(-_softplus(-zf), N_SPLIT)

    @pl.when(pl.program_id(1) == 0)
    def _():
        carry_ref[...] = jnp.zeros_like(carry_ref)

    carry = carry_ref[...]
    upto = jnp.where(col <= row, 1.0, 0.0).astype(BF16)
    upto_keys = jnp.where(col <= key_row, 1.0, 0.0).astype(BF16)
    cum = carry
    cum_keys = carry
    for part in log_f_parts:
        cum = cum + _dot(upto, part)
        cum_keys = cum_keys + _dot(upto_keys, part)
    carry_ref[...] = cum[ts - 1:ts, :]

    ones = jnp.ones((ts, LANES), BF16)
    pieces = jnp.concatenate(_split_bf16(cum * LOG2E, N_SPLIT) + [ones], axis=1)
    pieces_keys = jnp.concatenate(_split_bf16(cum_keys * LOG2E, N_SPLIT) + [ones], axis=1)
    qaug_ref[0] = _dot_nt(pq_ref[...], pieces).astype(BF16)
    kaug_ref[0] = _dot(pieces_keys, pk_ref[...]).astype(BF16)


def _aug_constants():
    pq = np.zeros((D_GRP, (N_SPLIT + 1) * LANES), np.float32)
    pk = np.zeros(((N_SPLIT + 1) * LANES, D_GRP), np.float32)
    one_col = N_SPLIT * LANES
    for h in range(N_HEADS):
        base = (h // 2) * LANES + AUG_STRIDE * (h % 2)
        for j in range(N_SPLIT):
            pq[base + j, j * LANES + h] = 1.0
            pk[one_col, base + j] = 1.0
            pk[j * LANES + h, base + N_SPLIT + j] = -1.0
            pq[base + N_SPLIT + j, one_col] = 1.0
    return jnp.asarray(pq, BF16), jnp.asarray(pk, BF16)


def _project(x, ada_l, g_norm_l, wq, wk, wv, wg, wf, bf, consts, ts):
    b, s, d = x.shape
    pq, pk = consts
    const = lambda shape: pl.BlockSpec(shape, lambda i, j: (0,) * len(shape))
    row_major = pl.BlockSpec((1, ts, D_GRP), lambda i, j: (i, j, 0))
    feat_major = pl.BlockSpec((1, D_GRP, ts), lambda i, j: (i, 0, j))
    v_spec = pl.BlockSpec((1, ts // KEY_TILE, D_GRP, KEY_TILE), lambda i, j: (i, j, 0, 0))
    row_shape = jax.ShapeDtypeStruct((b, s, D_GRP), BF16)
    feat_shape = jax.ShapeDtypeStruct((b, D_GRP, s), BF16)
    v_shape = jax.ShapeDtypeStruct((b, s // KEY_TILE, D_GRP, KEY_TILE), BF16)
    return pl.pallas_call(
        functools.partial(_proj_kernel, d_model=d),
        grid=(b, s // ts),
        in_specs=[
            pl.BlockSpec((1, ts, d), lambda i, j: (i, j, 0)),
            pl.BlockSpec((1, 1, 3 * d), lambda i, j: (i, 0, 0)),
            const((1, d)),
            const(wq.shape), const(wk.shape), const(wv.shape), const(wg.shape), const(wf.shape),
            const(bf.shape), const(pq.shape), const(pk.shape),
        ],
        out_specs=[feat_major, row_major, v_spec, feat_major, row_major, v_spec,
                   pl.BlockSpec((1, 2 * D_GRP, ts), lambda i, j: (i, 0, j)), feat_major, row_major],
        out_shape=[feat_shape, row_shape, v_shape, feat_shape, row_shape, v_shape,
                   jax.ShapeDtypeStruct((b, 2 * D_GRP, s), F32), feat_shape, row_shape],
        scratch_shapes=[pltpu.VMEM((1, LANES), F32)],
        compiler_params=pltpu.CompilerParams(
            dimension_semantics=("parallel", "arbitrary"), vmem_limit_bytes=VMEM_LIMIT),
        name="project",
    )(x, ada_l.reshape(b, 1, 3 * d), g_norm_l.reshape(1, d), wq, wk, wv, wg, wf, bf, pq, pk)


def _tile_iotas(tq):
    shape = (KEY_RUN, SUBLANES, tq)
    run = lax.broadcasted_iota(jnp.int32, shape, 0)
    sub = lax.broadcasted_iota(jnp.int32, shape, 1)
    qry = lax.broadcasted_iota(jnp.int32, shape, 2)
    return run, sub, qry


def _head_rows(block, j, width):
    row = lax.broadcasted_iota(jnp.int32, block.shape, 0)
    return jnp.where((row >= j * width) & (row < (j + 1) * width), block, jnp.zeros_like(block))


def _shift_up(x, k, fill):
    sub = lax.broadcasted_iota(jnp.int32, x.shape, 0)
    return jnp.where(sub + k < SUBLANES, pltpu.roll(x, SUBLANES - k, axis=0), fill)


def _tile_pipeline(qt, stage_scores, stage_weights, stage_values, state, alive=None, mask_first=True,
                   lowest=None):
    stage_scores(1, qt)
    state = stage_weights(1, 0, state, mask_first)
    stage_scores(0, jnp.maximum(qt - 1, 0))
    low = 0 if lowest is None else lowest(state)
    clamp = lambda kt: jnp.maximum(kt, low)

    def half(slot, kt, state):
        stage_scores(1 - slot, clamp(kt - 1))
        state = stage_values(slot, kt + 1, state, None)
        return stage_weights(slot, 1 - slot, state, False)

    n_bodies = (qt - low + 1) // 2
    if alive is None:
        def body(i, state):
            kt = qt - 1 - 2 * i
            return half(1, kt - 1, half(0, kt, state))

        state = lax.fori_loop(0, n_bodies, body, state)
        done = n_bodies
    else:
        def body(loop):
            i, _, state = loop
            kt = qt - 1 - 2 * i
            state = half(0, kt, state)
            go = alive(state, kt - 1)
            return i + 1, go, half(1, kt - 1, state)

        done, _, state = lax.while_loop(lambda loop: (loop[0] < n_bodies) & loop[1], body,
                                        (jnp.int32(0), alive(state, qt), state))
    last = qt - 2 * done
    return stage_values(0, clamp(last), state, last >= low)


def _attn_scratch(tq):
    return [pltpu.VMEM((2, 2, KEY_TILE, tq), F32), pltpu.VMEM((2, 2, KEY_TILE, tq), BF16)]


def _sb_kernel(q_ref, k_ref, v_ref, o_ref, z_ref, a_ref, *, tq):
    run, sub, qry = _tile_iotas(tq)
    causal = sub * KEY_RUN + run < qry
    lax.fori_loop(0, q_ref.shape[2] // tq,
                  lambda qt, _: _sb_query_tile(qt, causal, q_ref, k_ref, v_ref, o_ref, z_ref, a_ref, tq), 0)


def _sb_query_tile(qt, causal, q_ref, k_ref, v_ref, o_ref, z_ref, a_ref, tq):
    cols = pl.ds(pl.multiple_of(qt * tq, tq), tq)
    q_heads = [_head_rows(q_ref[0, :, cols], h, HEAD_DIM) for h in range(2)]

    def stage_scores(slot, kt):
        start = pl.multiple_of(kt * KEY_TILE, KEY_TILE)
        k = k_ref[0, pl.ds(start, KEY_TILE), :]
        for h in range(2):
            z_ref[slot, h] = _dot(k, q_heads[h])

    def weights(half_z, carry, masked):
        th = jnp.tanh(half_z.reshape(KEY_RUN, SUBLANES, tq))
        rem = 0.5 - 0.5 * th
        if masked:
            rem = jnp.where(causal, rem, 1.0)
        prod = jnp.ones((SUBLANES, tq), F32)
        parts = [None] * KEY_RUN
        for r in reversed(range(KEY_RUN)):
            below = prod * rem[r]
            parts[r] = prod - below
            prod = below
        incl = prod
        for step in (1, 2, 4):
            incl = incl * _shift_up(incl, step, 1.0)
        scale = _shift_up(incl, 1, 1.0) * carry
        a = (jnp.stack(parts) * scale[None]).reshape(KEY_TILE, tq).astype(BF16)
        return a, carry * jnp.broadcast_to(incl[0:1], carry.shape)

    def stage_weights(src, dst, state, masked):
        carry, acc = state
        new_carry = []
        for h in range(2):
            a_ref[dst, h], c = weights(z_ref[src, h], carry[h], masked)
            new_carry.append(c)
        return tuple(new_carry), acc

    def stage_values(slot, kt, state, valid):
        carry, acc = state
        new_acc = []
        for h in range(2):
            av = _dot(v_ref[0, kt, h * HEAD_DIM:(h + 1) * HEAD_DIM, :], a_ref[slot, h])
            new_acc.append(acc[h] + (av if valid is None else jnp.where(valid, av, 0.0)))
        return carry, tuple(new_acc)

    state = ((jnp.ones((SUBLANES, tq), F32),) * 2, (jnp.zeros((HEAD_DIM, tq), F32),) * 2)
    alive = lambda st, kt: jnp.max(jnp.maximum(st[0][0], st[0][1])) > 0.0

    before = jnp.maximum(qt - 1, 0)
    stage_scores(1, qt)
    stage_scores(0, before)
    state = stage_weights(1, 1, state, True)
    state = stage_weights(0, 0, state, False)
    state = stage_values(1, qt, state, None)
    state = stage_values(0, before, state, qt >= 1)

    def rest(state):
        return _tile_pipeline(qt - 2, stage_scores, stage_weights, stage_values, state, alive,
                              mask_first=False)

    _, acc = lax.cond((qt >= 2) & alive(state, qt - 1), rest, lambda st: st, state)
    o_ref[0, :, cols] = jnp.concatenate(acc, axis=0)
    return 0


def _sb_attention(q_t, k, v_t, tq):
    b, _, s = q_t.shape
    q_spec = pl.BlockSpec((1, LANES, s), lambda i, p: (i, p, 0))
    return pl.pallas_call(
        functools.partial(_sb_kernel, tq=tq),
        grid=(b, N_PAIRS),
        in_specs=[
            q_spec,
            pl.BlockSpec((1, s, LANES), lambda i, p: (i, 0, p)),
            pl.BlockSpec((1, s // KEY_TILE, LANES, KEY_TILE), lambda i, p: (i, 0, p, 0)),
        ],
        out_specs=q_spec,
        out_shape=jax.ShapeDtypeStruct((b, D_GRP, s), F32),
        scratch_shapes=_attn_scratch(tq),
        compiler_params=pltpu.CompilerParams(
            dimension_semantics=("parallel", "parallel"), vmem_limit_bytes=VMEM_LIMIT),
        name="sb_attention",
    )(q_t, k, v_t)


def _fox_kernel(q_ref, qa_ref, k_ref, ka_ref, v_ref, o_ref, z_ref, p_ref, tails_ref, *, tq):
    run, sub, qry = _tile_iotas(tq)
    causal = (sub * KEY_RUN + run <= qry).reshape(KEY_TILE, tq)

    n_tiles = k_ref.shape[1] // KEY_TILE
    tails_ref[...] = jnp.zeros(tails_ref.shape, F32)

    def scan_keys(t, k_abs):
        start = pl.multiple_of(t * KEY_TILE, KEY_TILE)
        k = k_ref[0, pl.ds(start, KEY_TILE), :]
        tail = ka_ref[0, pl.ds(start + KEY_TILE - 2 * SUBLANES, 2 * SUBLANES), :].astype(F32)
        tails_ref[pl.ds(t, 1), :] = tail[2 * SUBLANES - 1:, :]
        return jnp.maximum(k_abs, jnp.max(jnp.abs(k.astype(F32)), axis=0, keepdims=True))

    k_abs = lax.fori_loop(0, n_tiles, scan_keys, jnp.zeros((1, LANES), F32))
    feat = lax.broadcasted_iota(jnp.int32, (SUBLANES, LANES), 1)
    head = lax.broadcasted_iota(jnp.int32, (SUBLANES, LANES), 0)
    first = head * AUG_STRIDE + N_SPLIT
    pick = jnp.where((feat >= first) & (feat < first + N_SPLIT) & (head < 2), 1.0, 0.0).astype(BF16)
    neg_cum_rows = _dot_nt(pick, tails_ref[...].astype(BF16))
    neg_cum = [neg_cum_rows[h:h + 1, :] for h in range(2)]
    k_bound = [HEAD_DIM ** 0.5 * jnp.max(k_abs[:, h * HEAD_DIM:(h + 1) * HEAD_DIM], axis=1, keepdims=True)
               for h in range(2)]
    lax.fori_loop(0, q_ref.shape[2] // tq,
                  lambda qt, _: _fox_query_tile(qt, causal, k_bound, neg_cum, q_ref, qa_ref, k_ref, ka_ref,
                                                v_ref, o_ref, z_ref, p_ref, tq), 0)


def _fox_query_tile(qt, causal, k_bound, neg_cum, q_ref, qa_ref, k_ref, ka_ref, v_ref, o_ref, z_ref, p_ref,
                    tq):
    cols = pl.ds(pl.multiple_of(qt * tq, tq), tq)
    q_heads = [jnp.concatenate([_head_rows(q_ref[0, :, cols], h, HEAD_DIM),
                                _head_rows(qa_ref[0, :, cols], h, AUG_STRIDE)], axis=0) for h in range(2)]

    def stage_scores(slot, kt):
        start = pl.multiple_of(kt * KEY_TILE, KEY_TILE)
        k = jnp.concatenate([k_ref[0, pl.ds(start, KEY_TILE), :],
                             ka_ref[0, pl.ds(start, KEY_TILE), :]], axis=1)
        for h in range(2):
            z_ref[slot, h] = _dot(k, q_heads[h])

    def stage_weights(src, dst, state, masked):
        m, l, _, _, acc = state
        m_new, l_new, alpha = [], [], []
        for h in range(2):
            logits = z_ref[src, h]
            if masked:
                logits = jnp.where(causal, logits, -jnp.inf)
            mh = jnp.maximum(m[h], jnp.max(logits, axis=0, keepdims=True))
            p = jnp.exp2(logits - mh)
            ah = jnp.exp2(m[h] - mh)
            p_ref[dst, h] = p.astype(BF16)
            m_new.append(mh)
            alpha.append(ah)
            l_new.append(ah * l[h] + jnp.sum(p, axis=0, keepdims=True))
        return tuple(m_new), tuple(l_new), l, tuple(alpha), acc

    def stage_values(slot, kt, state, valid):
        m, l, l_prev, alpha, acc = state
        new_acc = []
        for h in range(2):
            pv = _dot(v_ref[0, kt, h * HEAD_DIM:(h + 1) * HEAD_DIM, :], p_ref[slot, h])
            upd = alpha[h] * acc[h] + pv
            new_acc.append(upd if valid is None else jnp.where(valid, upd, acc[h]))
        if valid is not None:
            l = tuple(jnp.where(valid, l[h], l_prev[h]) for h in range(2))
        return m, l, l_prev, alpha, tuple(new_acc)

    q32 = q_ref[0, :, cols].astype(F32)
    qa32 = qa_ref[0, :, cols].astype(F32)
    reach = []
    for h in range(2):
        q_norm = jnp.sqrt(jnp.sum(jnp.square(q32[h * HEAD_DIM:(h + 1) * HEAD_DIM]), axis=0, keepdims=True))
        cum_q = jnp.sum(qa32[h * AUG_STRIDE:h * AUG_STRIDE + N_SPLIT], axis=0, keepdims=True)
        reach.append(q_norm * k_bound[h] + cum_q)

    def lowest(state):
        m = state[0]
        tile = lax.broadcasted_iota(jnp.int32, (1, LANES), 1)
        first = []
        for h in range(2):
            gap = jnp.max(reach[h] - m[h], axis=1, keepdims=True)
            needed = (gap + neg_cum[h] >= -UNDERFLOW_BITS) | (tile >= qt)
            first.append(jnp.min(jnp.where(needed, tile, LANES).astype(F32), axis=1, keepdims=True))
        return jnp.minimum(first[0], first[1])[0, 0].astype(jnp.int32)

    row = lambda v: (jnp.full((1, tq), v, F32),) * 2
    state = (row(-jnp.inf), row(0.0), row(0.0), row(1.0), (jnp.zeros((HEAD_DIM, tq), F32),) * 2)
    _, l, _, _, acc = _tile_pipeline(qt, stage_scores, stage_weights, stage_values, state, lowest=lowest)
    o_ref[0, :, cols] = jnp.concatenate([acc[h] / l[h] for h in range(2)], axis=0)
    return 0


def _fox_attention(q_t, qaug_t, k, kaug, v_t, tq):
    b, _, s = q_t.shape
    assert s // KEY_TILE <= LANES
    q_spec = pl.BlockSpec((1, LANES, s), lambda i, p: (i, p, 0))
    k_spec = pl.BlockSpec((1, s, LANES), lambda i, p: (i, 0, p))
    return pl.pallas_call(
        functools.partial(_fox_kernel, tq=tq),
        grid=(b, N_PAIRS),
        in_specs=[q_spec, q_spec, k_spec, k_spec,
                  pl.BlockSpec((1, s // KEY_TILE, LANES, KEY_TILE), lambda i, p: (i, 0, p, 0))],
        out_specs=q_spec,
        out_shape=jax.ShapeDtypeStruct((b, D_GRP, s), F32),
        scratch_shapes=_attn_scratch(tq) + [pltpu.VMEM((LANES, LANES), F32)],
        compiler_params=pltpu.CompilerParams(
            dimension_semantics=("parallel", "parallel"), vmem_limit_bytes=VMEM_LIMIT),
        name="fox_attention",
    )(q_t, qaug_t, k, kaug, v_t)


def _rms_rows(x):
    return x * lax.rsqrt(jnp.mean(x * x, axis=0, keepdims=True) + EPS)


def _out_kernel(osb_ref, ofx_ref, gate_ref, x_ref, wout_ref, ada_ref, gf_ref, o_ref, *, d_model, final):
    y = jnp.concatenate([_rms_rows(osb_ref[0]), _rms_rows(ofx_ref[0])], axis=0)
    g = gate_ref[0]
    y = (y * (g / (1.0 + jnp.exp(-g)))).astype(BF16)
    gate = ada_ref[0][:, 2 * d_model:]
    out = x_ref[0] + (1.0 + gate) * _dot_tn(y, wout_ref[...])
    if final:
        out = out * lax.rsqrt(jnp.mean(out * out, axis=-1, keepdims=True) + EPS) * gf_ref[...]
    o_ref[0] = out


def _output(o_sb, o_fx, gate_t, x, wout, ada_l, g_final, ts, final):
    b, s, d = x.shape
    const = lambda shape: pl.BlockSpec(shape, lambda i, j: (0,) * len(shape))
    return pl.pallas_call(
        functools.partial(_out_kernel, d_model=d, final=final),
        grid=(b, s // ts),
        in_specs=[
            pl.BlockSpec((1, D_GRP, ts), lambda i, j: (i, 0, j)),
            pl.BlockSpec((1, D_GRP, ts), lambda i, j: (i, 0, j)),
            pl.BlockSpec((1, 2 * D_GRP, ts), lambda i, j: (i, 0, j)),
            pl.BlockSpec((1, ts, d), lambda i, j: (i, j, 0)),
            const(wout.shape),
            pl.BlockSpec((1, 1, 3 * d), lambda i, j: (i, 0, 0)),
            const((1, d)),
        ],
        out_specs=pl.BlockSpec((1, ts, d), lambda i, j: (i, j, 0)),
        out_shape=jax.ShapeDtypeStruct((b, s, d), F32),
        compiler_params=pltpu.CompilerParams(
            dimension_semantics=("parallel", "parallel"), vmem_limit_bytes=VMEM_LIMIT),
        name="output",
    )(o_sb, o_fx, gate_t, x, wout, ada_l.reshape(b, 1, 3 * d), g_final.reshape(1, d))


def kernel(x, c, w_ada, b_ada, g_norm, w_in, b_f, g_grp, w_out, g_final):
    b, s, d = x.shape
    depth = w_ada.shape[0]
    ts = min(512, s)
    tq = min(256, s)
    consts = _aug_constants()

    grp = lambda i: w_in[:, :, i * D_GRP:(i + 1) * D_GRP]
    t = lambda w: jnp.swapaxes(w, 1, 2)
    wq = jnp.stack([t(grp(0)) * (0.5 * Q_SCALE), t(grp(3)) * (LOG2E * Q_SCALE)], axis=1).astype(BF16)
    wk = jnp.stack([grp(1), grp(4)], axis=1).astype(BF16)
    wv = jnp.stack([t(grp(2)), t(grp(5))], axis=1).astype(BF16)
    wg = t(w_in[:, :, 6 * D_GRP:8 * D_GRP]).astype(BF16)
    wf = jnp.zeros((depth, d, LANES), BF16).at[:, :, :N_HEADS].set(w_in[:, :, 8 * D_GRP:].astype(BF16))
    bf = jnp.zeros((depth, 1, LANES), F32).at[:, 0, :N_HEADS].set(b_f)
    wout = (g_grp[:, :, None] * w_out).astype(BF16)

    ada = _adaln(c, w_ada, b_ada)
    for l in range(depth):
        q_sb, k_sb, v_sb, q_fx, k_fx, v_fx, gate_t, qaug, kaug = _project(
            x, ada[l], g_norm[l], wq[l], wk[l], wv[l], wg[l], wf[l], bf[l], consts, ts)
        o_sb = _sb_attention(q_sb, k_sb, v_sb, tq)
        o_fx = _fox_attention(q_fx, qaug, k_fx, kaug, v_fx, tq)
        x = _output(o_sb, o_fx, gate_t, x, wout[l], ada[l], g_final, ts, l == depth - 1)
    return x
```

```python
import functools

import numpy as np
import jax
import jax.numpy as jnp
from jax import lax
from jax.experimental import pallas as pl
from jax.experimental.pallas import tpu as pltpu

F32 = jnp.float32
BF16 = jnp.bfloat16

HEAD_DIM = 64
N_HEADS = 8
D_GRP = N_HEADS * HEAD_DIM
LANES = 128
SUBLANES = 8
N_PAIRS = D_GRP // LANES
KEY_TILE = 256
KEY_RUN = KEY_TILE // SUBLANES
AUG_STRIDE = 16
N_SPLIT = 3
EPS = 1e-6
Q_SCALE = HEAD_DIM ** -0.5
LOG2E = 1.4426950408889634
VMEM_LIMIT = 52 * 1024 * 1024
UNDERFLOW_BITS = 160.0


def _split_bf16(x, n):
    parts = []
    r = x
    for i in range(n):
        p = r.astype(BF16)
        parts.append(p)
        if i + 1 < n:
            r = r - p.astype(F32)
    return parts


def _dot(a, b):
    return jnp.dot(a, b, preferred_element_type=F32)


def _dot_nt(a, b):
    return lax.dot_general(a, b, (((1,), (1,)), ((), ())), preferred_element_type=F32)


def _dot_tn(a, b):
    return lax.dot_general(a, b, (((0,), (0,)), ((), ())), preferred_element_type=F32)


def _softplus(z):
    return jnp.maximum(z, 0.0) + jnp.log1p(jnp.exp(-jnp.abs(z)))


def _key_of_row(p):
    return (p & (SUBLANES - 1)) * KEY_RUN + (p >> 3)


def _ada_kernel(c_ref, w_ref, b_ref, o_ref):
    c = c_ref[...]
    c_act = c / (1.0 + jnp.exp(-c))
    w_parts = _split_bf16(w_ref[0], 2)
    acc = jnp.zeros(o_ref.shape[1:], F32)
    for cp in _split_bf16(c_act, N_SPLIT):
        for wp in w_parts:
            acc = acc + _dot(cp, wp)
    o_ref[0] = acc + b_ref[0]


def _adaln(c, w_ada, b_ada):
    depth, d, d3 = w_ada.shape
    b = c.shape[0]
    rows = SUBLANES
    tn = 1024
    c_pad = jnp.zeros((rows, d), F32).at[:b].set(c)
    out = pl.pallas_call(
        _ada_kernel,
        grid=(depth, d3 // tn),
        in_specs=[
            pl.BlockSpec((rows, d), lambda l, n: (0, 0)),
            pl.BlockSpec((1, d, tn), lambda l, n: (l, 0, n)),
            pl.BlockSpec((1, 1, tn), lambda l, n: (l, 0, n)),
        ],
        out_specs=pl.BlockSpec((1, rows, tn), lambda l, n: (l, 0, n)),
        out_shape=jax.ShapeDtypeStruct((depth, rows, d3), F32),
        compiler_params=pltpu.CompilerParams(
            dimension_semantics=("parallel", "parallel"), vmem_limit_bytes=VMEM_LIMIT),
        name="adaln",
    )(c_pad, w_ada, b_ada.reshape(depth, 1, d3))
    return out[:, :b]


def _proj_kernel(x_ref, ada_ref, gn_ref, wq_ref, wk_ref, wv_ref, wg_ref, wf_ref, bf_ref,
                 pq_ref, pk_ref,
                 qsb_ref, ksb_ref, vsb_ref, qfx_ref, kfx_ref, vfx_ref, gate_ref, qaug_ref, kaug_ref,
                 carry_ref, *, d_model):
    ts = x_ref.shape[1]
    x = x_ref[0]
    ada = ada_ref[0]
    shift = ada[:, :d_model]
    scale = ada[:, d_model:2 * d_model]
    r = lax.rsqrt(jnp.mean(x * x, axis=-1, keepdims=True) + EPS)
    h = ((x * r) * gn_ref[...] * (1.0 + scale) + shift).astype(BF16)

    row = lax.broadcasted_iota(jnp.int32, (ts, ts), 0)
    col = lax.broadcasted_iota(jnp.int32, (ts, ts), 1)
    in_tile = row & (KEY_TILE - 1)
    key_row = (row - in_tile) + _key_of_row(in_tile)
    tile_row = lax.broadcasted_iota(jnp.int32, (KEY_TILE, KEY_TILE), 0)
    tile_col = lax.broadcasted_iota(jnp.int32, (KEY_TILE, KEY_TILE), 1)
    perm = jnp.where(tile_col == _key_of_row(tile_row), 1.0, 0.0).astype(BF16)
    h_keys = jnp.concatenate([_dot(perm, h[t * KEY_TILE:(t + 1) * KEY_TILE])
                              for t in range(ts // KEY_TILE)], axis=0).astype(BF16)

    qsb_ref[0] = _dot_nt(wq_ref[0], h).astype(BF16)
    qfx_ref[0] = _dot_nt(wq_ref[1], h).astype(BF16)
    gate_ref[0] = _dot_nt(wg_ref[...], h).astype(BF16)
    ksb_ref[0] = _dot(h_keys, wk_ref[0]).astype(BF16)
    kfx_ref[0] = _dot(h_keys, wk_ref[1]).astype(BF16)
    for o_ref, w in ((vsb_ref, wv_ref[0]), (vfx_ref, wv_ref[1])):
        v_t = _dot_nt(w, h_keys).astype(BF16)
        for t in range(ts // KEY_TILE):
            o_ref[0, t] = v_t[:, t * KEY_TILE:(t + 1) * KEY_TILE]

    zf = _dot(h, wf_ref[...]) + bf_ref[...]
    log_f_parts = _split_bf16(-_softplus(-zf), N_SPLIT)

    @pl.when(pl.program_id(1) == 0)
    def _():
        carry_ref[...] = jnp.zeros_like(carry_ref)

    carry = carry_ref[...]
    upto = jnp.where(col <= row, 1.0, 0.0).astype(BF16)
    upto_keys = jnp.where(col <= key_row, 1.0, 0.0).astype(BF16)
    sums = _dot(jnp.concatenate([upto, upto_keys], axis=0), jnp.concatenate(log_f_parts, axis=1))
    fold = lambda x: x[:, :LANES] + x[:, LANES:2 * LANES] + x[:, 2 * LANES:]
    cum = carry + fold(sums[:ts])
    cum_keys = carry + fold(sums[ts:])
    carry_ref[...] = cum[ts - 1:ts, :]

    ones = jnp.ones((ts, LANES), BF16)
    pieces = jnp.concatenate(_split_bf16(cum * LOG2E, N_SPLIT) + [ones], axis=1)
    pieces_keys = jnp.concatenate(_split_bf16(cum_keys * LOG2E, N_SPLIT) + [ones], axis=1)
    qaug_ref[0] = _dot_nt(pq_ref[...], pieces).astype(BF16)
    kaug_ref[0] = _dot(pieces_keys, pk_ref[...]).astype(BF16)


def _aug_constants():
    pq = np.zeros((D_GRP, (N_SPLIT + 1) * LANES), np.float32)
    pk = np.zeros(((N_SPLIT + 1) * LANES, D_GRP), np.float32)
    one_col = N_SPLIT * LANES
    for h in range(N_HEADS):
        base = (h // 2) * LANES + AUG_STRIDE * (h % 2)
        for j in range(N_SPLIT):
            pq[base + j, j * LANES + h] = 1.0
            pk[one_col, base + j] = 1.0
            pk[j * LANES + h, base + N_SPLIT + j] = -1.0
            pq[base + N_SPLIT + j, one_col] = 1.0
    return jnp.asarray(pq, BF16), jnp.asarray(pk, BF16)


def _project(x, ada_l, g_norm_l, wq, wk, wv, wg, wf, bf, consts, ts):
    b, s, d = x.shape
    pq, pk = consts
    const = lambda shape: pl.BlockSpec(shape, lambda i, j: (0,) * len(shape))
    row_major = pl.BlockSpec((1, ts, D_GRP), lambda i, j: (i, j, 0))
    feat_major = pl.BlockSpec((1, D_GRP, ts), lambda i, j: (i, 0, j))
    v_spec = pl.BlockSpec((1, ts // KEY_TILE, D_GRP, KEY_TILE), lambda i, j: (i, j, 0, 0))
    row_shape = jax.ShapeDtypeStruct((b, s, D_GRP), BF16)
    feat_shape = jax.ShapeDtypeStruct((b, D_GRP, s), BF16)
    v_shape = jax.ShapeDtypeStruct((b, s // KEY_TILE, D_GRP, KEY_TILE), BF16)
    return pl.pallas_call(
        functools.partial(_proj_kernel, d_model=d),
        grid=(b, s // ts),
        in_specs=[
            pl.BlockSpec((1, ts, d), lambda i, j: (i, j, 0)),
            pl.BlockSpec((1, 1, 3 * d), lambda i, j: (i, 0, 0)),
            const((1, d)),
            const(wq.shape), const(wk.shape), const(wv.shape), const(wg.shape), const(wf.shape),
            const(bf.shape), const(pq.shape), const(pk.shape),
        ],
        out_specs=[feat_major, row_major, v_spec, feat_major, row_major, v_spec,
                   pl.BlockSpec((1, 2 * D_GRP, ts), lambda i, j: (i, 0, j)), feat_major, row_major],
        out_shape=[feat_shape, row_shape, v_shape, feat_shape, row_shape, v_shape,
                   jax.ShapeDtypeStruct((b, 2 * D_GRP, s), BF16), feat_shape, row_shape],
        scratch_shapes=[pltpu.VMEM((1, LANES), F32)],
        compiler_params=pltpu.CompilerParams(
            dimension_semantics=("parallel", "arbitrary"), vmem_limit_bytes=VMEM_LIMIT),
        name="project",
    )(x, ada_l.reshape(b, 1, 3 * d), g_norm_l.reshape(1, d), wq, wk, wv, wg, wf, bf, pq, pk)


def _tile_iotas(tq):
    shape = (KEY_RUN, SUBLANES, tq)
    run = lax.broadcasted_iota(jnp.int32, shape, 0)
    sub = lax.broadcasted_iota(jnp.int32, shape, 1)
    qry = lax.broadcasted_iota(jnp.int32, shape, 2)
    return run, sub, qry


def _head_rows(block, j, width):
    row = lax.broadcasted_iota(jnp.int32, block.shape, 0)
    return jnp.where((row >= j * width) & (row < (j + 1) * width), block, jnp.zeros_like(block))


def _shift_up(x, k, fill):
    sub = lax.broadcasted_iota(jnp.int32, x.shape, 0)
    return jnp.where(sub + k < SUBLANES, pltpu.roll(x, SUBLANES - k, axis=0), fill)


def _tile_pipeline(qt, stage_scores, stage_weights, stage_values, state, alive=None, mask_first=True,
                   lowest=None):
    stage_scores(1, qt)
    state = stage_weights(1, 0, state, mask_first)
    stage_scores(0, jnp.maximum(qt - 1, 0))
    low = 0 if lowest is None else lowest(state)
    clamp = lambda kt: jnp.maximum(kt, low)

    def half(slot, kt, state):
        stage_scores(1 - slot, clamp(kt - 1))
        state = stage_values(slot, kt + 1, state, None)
        return stage_weights(slot, 1 - slot, state, False)

    n_bodies = (qt - low + 1) // 2
    if alive is None:
        def body(i, state):
            kt = qt - 1 - 2 * i
            return half(1, kt - 1, half(0, kt, state))

        state = lax.fori_loop(0, n_bodies, body, state)
        done = n_bodies
    else:
        def body(loop):
            i, _, state = loop
            kt = qt - 1 - 2 * i
            state = half(0, kt, state)
            go = alive(state, kt - 1)
            return i + 1, go, half(1, kt - 1, state)

        done, _, state = lax.while_loop(lambda loop: (loop[0] < n_bodies) & loop[1], body,
                                        (jnp.int32(0), alive(state, qt), state))
    last = qt - 2 * done
    return stage_values(0, clamp(last), state, last >= low)


def _attn_scratch(tq):
    return [pltpu.VMEM((2, 2, KEY_TILE, tq), F32), pltpu.VMEM((2, 2, KEY_TILE, tq), BF16)]


def _sb_kernel(q_ref, k_ref, v_ref, o_ref, z_ref, a_ref, *, tq):
    run, sub, qry = _tile_iotas(tq)
    causal = sub * KEY_RUN + run < qry
    lax.fori_loop(0, q_ref.shape[2] // tq,
                  lambda qt, _: _sb_query_tile(qt, causal, q_ref, k_ref, v_ref, o_ref, z_ref, a_ref, tq), 0)


def _sb_query_tile(qt, causal, q_ref, k_ref, v_ref, o_ref, z_ref, a_ref, tq):
    cols = pl.ds(pl.multiple_of(qt * tq, tq), tq)
    q_heads = [_head_rows(q_ref[0, :, cols], h, HEAD_DIM) for h in range(2)]

    def stage_scores(slot, kt):
        start = pl.multiple_of(kt * KEY_TILE, KEY_TILE)
        k = k_ref[0, pl.ds(start, KEY_TILE), :]
        for h in range(2):
            z_ref[slot, h] = _dot(k, q_heads[h])

    def weights(half_z, carry, masked):
        th = jnp.tanh(half_z.reshape(KEY_RUN, SUBLANES, tq))
        rem = 0.5 - 0.5 * th
        if masked:
            rem = jnp.where(causal, rem, 1.0)
        prod = jnp.ones((SUBLANES, tq), F32)
        parts = [None] * KEY_RUN
        for r in reversed(range(KEY_RUN)):
            below = prod * rem[r]
            parts[r] = prod - below
            prod = below
        incl = prod
        for step in (1, 2, 4):
            incl = incl * _shift_up(incl, step, 1.0)
        scale = _shift_up(incl, 1, 1.0) * carry
        a = (jnp.stack(parts) * scale[None]).reshape(KEY_TILE, tq).astype(BF16)
        return a, carry * jnp.broadcast_to(incl[0:1], carry.shape)

    def stage_weights(src, dst, state, masked):
        carry, acc = state
        new_carry = []
        for h in range(2):
            a_ref[dst, h], c = weights(z_ref[src, h], carry[h], masked)
            new_carry.append(c)
        return tuple(new_carry), acc

    def stage_values(slot, kt, state, valid):
        carry, acc = state
        new_acc = []
        for h in range(2):
            av = _dot(v_ref[0, kt, h * HEAD_DIM:(h + 1) * HEAD_DIM, :], a_ref[slot, h])
            new_acc.append(acc[h] + (av if valid is None else jnp.where(valid, av, 0.0)))
        return carry, tuple(new_acc)

    state = ((jnp.ones((SUBLANES, tq), F32),) * 2, (jnp.zeros((HEAD_DIM, tq), F32),) * 2)
    alive = lambda st, kt: jnp.max(jnp.maximum(st[0][0], st[0][1])) > 0.0

    before = jnp.maximum(qt - 1, 0)
    stage_scores(1, qt)
    stage_scores(0, before)
    state = stage_weights(1, 1, state, True)
    state = stage_weights(0, 0, state, False)
    state = stage_values(1, qt, state, None)
    state = stage_values(0, before, state, qt >= 1)

    def rest(state):
        return _tile_pipeline(qt - 2, stage_scores, stage_weights, stage_values, state, alive,
                              mask_first=False)

    _, acc = lax.cond((qt >= 2) & alive(state, qt - 1), rest, lambda st: st, state)
    o_ref[0, :, cols] = jnp.concatenate(acc, axis=0).astype(BF16)
    return 0


def _sb_attention(q_t, k, v_t, tq):
    b, _, s = q_t.shape
    q_spec = pl.BlockSpec((1, LANES, s), lambda i, p: (i, p, 0))
    return pl.pallas_call(
        functools.partial(_sb_kernel, tq=tq),
        grid=(b, N_PAIRS),
        in_specs=[
            q_spec,
            pl.BlockSpec((1, s, LANES), lambda i, p: (i, 0, p)),
            pl.BlockSpec((1, s // KEY_TILE, LANES, KEY_TILE), lambda i, p: (i, 0, p, 0)),
        ],
        out_specs=q_spec,
        out_shape=jax.ShapeDtypeStruct((b, D_GRP, s), BF16),
        scratch_shapes=_attn_scratch(tq),
        compiler_params=pltpu.CompilerParams(
            dimension_semantics=("parallel", "parallel"), vmem_limit_bytes=VMEM_LIMIT),
        name="sb_attention",
    )(q_t, k, v_t)


def _fox_kernel(q_ref, qa_ref, k_ref, ka_ref, v_ref, o_ref, z_ref, p_ref, tails_ref, *, tq):
    run, sub, qry = _tile_iotas(tq)
    causal = (sub * KEY_RUN + run <= qry).reshape(KEY_TILE, tq)

    n_tiles = k_ref.shape[1] // KEY_TILE
    tails_ref[...] = jnp.zeros(tails_ref.shape, F32)

    def scan_keys(t, k_abs):
        start = pl.multiple_of(t * KEY_TILE, KEY_TILE)
        k = k_ref[0, pl.ds(start, KEY_TILE), :]
        tail = ka_ref[0, pl.ds(start + KEY_TILE - 2 * SUBLANES, 2 * SUBLANES), :].astype(F32)
        tails_ref[pl.ds(t, 1), :] = tail[2 * SUBLANES - 1:, :]
        return jnp.maximum(k_abs, jnp.max(jnp.abs(k.astype(F32)), axis=0, keepdims=True))

    k_abs = lax.fori_loop(0, n_tiles, scan_keys, jnp.zeros((1, LANES), F32))
    feat = lax.broadcasted_iota(jnp.int32, (SUBLANES, LANES), 1)
    head = lax.broadcasted_iota(jnp.int32, (SUBLANES, LANES), 0)
    first = head * AUG_STRIDE + N_SPLIT
    pick = jnp.where((feat >= first) & (feat < first + N_SPLIT) & (head < 2), 1.0, 0.0).astype(BF16)
    neg_cum_rows = _dot_nt(pick, tails_ref[...].astype(BF16))
    neg_cum = [neg_cum_rows[h:h + 1, :] for h in range(2)]
    k_bound = [HEAD_DIM ** 0.5 * jnp.max(k_abs[:, h * HEAD_DIM:(h + 1) * HEAD_DIM], axis=1, keepdims=True)
               for h in range(2)]
    lax.fori_loop(0, q_ref.shape[2] // tq,
                  lambda qt, _: _fox_query_tile(qt, causal, k_bound, neg_cum, q_ref, qa_ref, k_ref, ka_ref,
                                                v_ref, o_ref, z_ref, p_ref, tq), 0)


def _fox_query_tile(qt, causal, k_bound, neg_cum, q_ref, qa_ref, k_ref, ka_ref, v_ref, o_ref, z_ref, p_ref,
                    tq):
    cols = pl.ds(pl.multiple_of(qt * tq, tq), tq)
    q_heads = [jnp.concatenate([_head_rows(q_ref[0, :, cols], h, HEAD_DIM),
                                _head_rows(qa_ref[0, :, cols], h, AUG_STRIDE)], axis=0) for h in range(2)]

    def stage_scores(slot, kt):
        start = pl.multiple_of(kt * KEY_TILE, KEY_TILE)
        k = jnp.concatenate([k_ref[0, pl.ds(start, KEY_TILE), :],
                             ka_ref[0, pl.ds(start, KEY_TILE), :]], axis=1)
        for h in range(2):
            z_ref[slot, h] = _dot(k, q_heads[h])

    def stage_weights(src, dst, state, masked):
        m, l, _, _, acc = state
        m_new, l_new, alpha = [], [], []
        for h in range(2):
            logits = z_ref[src, h]
            if masked:
                logits = jnp.where(causal, logits, -jnp.inf)
            mh = jnp.maximum(m[h], jnp.max(logits, axis=0, keepdims=True))
            p = jnp.exp2(logits - mh)
            ah = jnp.exp2(m[h] - mh)
            p_ref[dst, h] = p.astype(BF16)
            m_new.append(mh)
            alpha.append(ah)
            l_new.append(ah * l[h] + jnp.sum(p, axis=0, keepdims=True))
        return tuple(m_new), tuple(l_new), l, tuple(alpha), acc

    def stage_values(slot, kt, state, valid):
        m, l, l_prev, alpha, acc = state
        new_acc = []
        for h in range(2):
            pv = _dot(v_ref[0, kt, h * HEAD_DIM:(h + 1) * HEAD_DIM, :], p_ref[slot, h])
            upd = alpha[h] * acc[h] + pv
            new_acc.append(upd if valid is None else jnp.where(valid, upd, acc[h]))
        if valid is not None:
            l = tuple(jnp.where(valid, l[h], l_prev[h]) for h in range(2))
        return m, l, l_prev, alpha, tuple(new_acc)

    q32 = q_ref[0, :, cols].astype(F32)
    qa32 = qa_ref[0, :, cols].astype(F32)
    reach = []
    for h in range(2):
        q_norm = jnp.sqrt(jnp.sum(jnp.square(q32[h * HEAD_DIM:(h + 1) * HEAD_DIM]), axis=0, keepdims=True))
        cum_q = jnp.sum(qa32[h * AUG_STRIDE:h * AUG_STRIDE + N_SPLIT], axis=0, keepdims=True)
        reach.append(q_norm * k_bound[h] + cum_q)

    def lowest(state):
        m = state[0]
        tile = lax.broadcasted_iota(jnp.int32, (1, LANES), 1)
        first = []
        for h in range(2):
            gap = jnp.max(reach[h] - m[h], axis=1, keepdims=True)
            needed = (gap + neg_cum[h] >= -UNDERFLOW_BITS) | (tile >= qt)
            first.append(jnp.min(jnp.where(needed, tile, LANES).astype(F32), axis=1, keepdims=True))
        return jnp.minimum(first[0], first[1])[0, 0].astype(jnp.int32)

    row = lambda v: (jnp.full((1, tq), v, F32),) * 2
    state = (row(-jnp.inf), row(0.0), row(0.0), row(1.0), (jnp.zeros((HEAD_DIM, tq), F32),) * 2)
    _, l, _, _, acc = _tile_pipeline(qt, stage_scores, stage_weights, stage_values, state, lowest=lowest)
    o_ref[0, :, cols] = jnp.concatenate([acc[h] / l[h] for h in range(2)], axis=0).astype(BF16)
    return 0


def _fox_attention(q_t, qaug_t, k, kaug, v_t, tq):
    b, _, s = q_t.shape
    assert s // KEY_TILE <= LANES
    q_spec = pl.BlockSpec((1, LANES, s), lambda i, p: (i, p, 0))
    k_spec = pl.BlockSpec((1, s, LANES), lambda i, p: (i, 0, p))
    return pl.pallas_call(
        functools.partial(_fox_kernel, tq=tq),
        grid=(b, N_PAIRS),
        in_specs=[q_spec, q_spec, k_spec, k_spec,
                  pl.BlockSpec((1, s // KEY_TILE, LANES, KEY_TILE), lambda i, p: (i, 0, p, 0))],
        out_specs=q_spec,
        out_shape=jax.ShapeDtypeStruct((b, D_GRP, s), BF16),
        scratch_shapes=_attn_scratch(tq) + [pltpu.VMEM((LANES, LANES), F32)],
        compiler_params=pltpu.CompilerParams(
            dimension_semantics=("parallel", "parallel"), vmem_limit_bytes=VMEM_LIMIT),
        name="fox_attention",
    )(q_t, qaug_t, k, kaug, v_t)


def _rms_rows(x):
    return x * lax.rsqrt(jnp.mean(x * x, axis=0, keepdims=True) + EPS)


def _out_kernel(osb_ref, ofx_ref, gate_ref, x_ref, wout_ref, ada_ref, gf_ref, o_ref, *, d_model, final):
    y = jnp.concatenate([_rms_rows(osb_ref[0].astype(F32)), _rms_rows(ofx_ref[0].astype(F32))], axis=0)
    g = gate_ref[0].astype(F32)
    y = (y * (g / (1.0 + jnp.exp(-g)))).astype(BF16)
    gate = ada_ref[0][:, 2 * d_model:]
    out = x_ref[0] + (1.0 + gate) * _dot_tn(y, wout_ref[...])
    if final:
        out = out * lax.rsqrt(jnp.mean(out * out, axis=-1, keepdims=True) + EPS) * gf_ref[...]
    o_ref[0] = out


def _output(o_sb, o_fx, gate_t, x, wout, ada_l, g_final, ts, final):
    b, s, d = x.shape
    const = lambda shape: pl.BlockSpec(shape, lambda i, j: (0,) * len(shape))
    return pl.pallas_call(
        functools.partial(_out_kernel, d_model=d, final=final),
        grid=(b, s // ts),
        in_specs=[
            pl.BlockSpec((1, D_GRP, ts), lambda i, j: (i, 0, j)),
            pl.BlockSpec((1, D_GRP, ts), lambda i, j: (i, 0, j)),
            pl.BlockSpec((1, 2 * D_GRP, ts), lambda i, j: (i, 0, j)),
            pl.BlockSpec((1, ts, d), lambda i, j: (i, j, 0)),
            const(wout.shape),
            pl.BlockSpec((1, 1, 3 * d), lambda i, j: (i, 0, 0)),
            const((1, d)),
        ],
        out_specs=pl.BlockSpec((1, ts, d), lambda i, j: (i, j, 0)),
        out_shape=jax.ShapeDtypeStruct((b, s, d), F32),
        compiler_params=pltpu.CompilerParams(
            dimension_semantics=("parallel", "parallel"), vmem_limit_bytes=VMEM_LIMIT),
        name="output",
    )(o_sb, o_fx, gate_t, x, wout, ada_l.reshape(b, 1, 3 * d), g_final.reshape(1, d))


def kernel(x, c, w_ada, b_ada, g_norm, w_in, b_f, g_grp, w_out, g_final):
    b, s, d = x.shape
    depth = w_ada.shape[0]
    ts = min(512, s)
    tq = min(256, s)
    consts = _aug_constants()

    grp = lambda i: w_in[:, :, i * D_GRP:(i + 1) * D_GRP]
    t = lambda w: jnp.swapaxes(w, 1, 2)
    wq = jnp.stack([t(grp(0)) * (0.5 * Q_SCALE), t(grp(3)) * (LOG2E * Q_SCALE)], axis=1).astype(BF16)
    wk = jnp.stack([grp(1), grp(4)], axis=1).astype(BF16)
    wv = jnp.stack([t(grp(2)), t(grp(5))], axis=1).astype(BF16)
    wg = t(w_in[:, :, 6 * D_GRP:8 * D_GRP]).astype(BF16)
    wf = jnp.zeros((depth, d, LANES), BF16).at[:, :, :N_HEADS].set(w_in[:, :, 8 * D_GRP:].astype(BF16))
    bf = jnp.zeros((depth, 1, LANES), F32).at[:, 0, :N_HEADS].set(b_f)
    wout = (g_grp[:, :, None] * w_out).astype(BF16)

    ada = _adaln(c, w_ada, b_ada)
    for l in range(depth):
        q_sb, k_sb, v_sb, q_fx, k_fx, v_fx, gate_t, qaug, kaug = _project(
            x, ada[l], g_norm[l], wq[l], wk[l], wv[l], wg[l], wf[l], bf[l], consts, ts)
        o_sb = _sb_attention(q_sb, k_sb, v_sb, tq)
        o_fx = _fox_attention(q_fx, qaug, k_fx, kaug, v_fx, tq)
        x = _output(o_sb, o_fx, gate_t, x, wout[l], ada[l], g_final, ts, l == depth - 1)
    return x
```

```python
import functools

import numpy as np
import jax
import jax.numpy as jnp
from jax import lax
from jax.experimental import pallas as pl
from jax.experimental.pallas import tpu as pltpu

F32 = jnp.float32
BF16 = jnp.bfloat16

HEAD_DIM = 64
N_HEADS = 8
D_GRP = N_HEADS * HEAD_DIM
LANES = 128
SUBLANES = 8
N_PAIRS = D_GRP // LANES
KEY_TILE = 256
KEY_RUN = KEY_TILE // SUBLANES
AUG_STRIDE = 16
N_SPLIT = 3
EPS = 1e-6
Q_SCALE = HEAD_DIM ** -0.5
LOG2E = 1.4426950408889634
VMEM_LIMIT = 52 * 1024 * 1024
UNDERFLOW_BITS = 160.0


def _split_bf16(x, n):
    parts = []
    r = x
    for i in range(n):
        p = r.astype(BF16)
        parts.append(p)
        if i + 1 < n:
            r = r - p.astype(F32)
    return parts


def _dot(a, b):
    return jnp.dot(a, b, preferred_element_type=F32)


def _dot_nt(a, b):
    return lax.dot_general(a, b, (((1,), (1,)), ((), ())), preferred_element_type=F32)


def _dot_tn(a, b):
    return lax.dot_general(a, b, (((0,), (0,)), ((), ())), preferred_element_type=F32)


def _softplus(z):
    return jnp.maximum(z, 0.0) + jnp.log1p(jnp.exp(-jnp.abs(z)))


def _key_of_row(p):
    return (p & (SUBLANES - 1)) * KEY_RUN + (p >> 3)


def _ada_kernel(c_ref, w_ref, b_ref, o_ref):
    c = c_ref[...]
    c_act = c / (1.0 + jnp.exp(-c))
    w_parts = _split_bf16(w_ref[0], 2)
    acc = jnp.zeros(o_ref.shape[1:], F32)
    for cp in _split_bf16(c_act, N_SPLIT):
        for wp in w_parts:
            acc = acc + _dot(cp, wp)
    o_ref[0] = acc + b_ref[0]


def _adaln(c, w_ada, b_ada):
    depth, d, d3 = w_ada.shape
    b = c.shape[0]
    rows = SUBLANES
    tn = 1024
    c_pad = jnp.zeros((rows, d), F32).at[:b].set(c)
    out = pl.pallas_call(
        _ada_kernel,
        grid=(depth, d3 // tn),
        in_specs=[
            pl.BlockSpec((rows, d), lambda l, n: (0, 0)),
            pl.BlockSpec((1, d, tn), lambda l, n: (l, 0, n)),
            pl.BlockSpec((1, 1, tn), lambda l, n: (l, 0, n)),
        ],
        out_specs=pl.BlockSpec((1, rows, tn), lambda l, n: (l, 0, n)),
        out_shape=jax.ShapeDtypeStruct((depth, rows, d3), F32),
        compiler_params=pltpu.CompilerParams(
            dimension_semantics=("parallel", "parallel"), vmem_limit_bytes=VMEM_LIMIT),
        name="adaln",
    )(c_pad, w_ada, b_ada.reshape(depth, 1, d3))
    return out[:, :b]


def _proj_kernel(x_ref, ada_ref, gn_ref, wq_ref, wk_ref, wv_ref, wg_ref, wf_ref, bf_ref,
                 pq_ref, pk_ref,
                 qsb_ref, ksb_ref, vsb_ref, qfx_ref, kfx_ref, vfx_ref, gate_ref, qaug_ref, kaug_ref,
                 carry_ref, *, d_model):
    ts = x_ref.shape[1]
    x = x_ref[0]
    ada = ada_ref[0]
    shift = ada[:, :d_model]
    scale = ada[:, d_model:2 * d_model]
    r = lax.rsqrt(jnp.mean(x * x, axis=-1, keepdims=True) + EPS)
    h = ((x * r) * gn_ref[...] * (1.0 + scale) + shift).astype(BF16)

    row = lax.broadcasted_iota(jnp.int32, (ts, ts), 0)
    col = lax.broadcasted_iota(jnp.int32, (ts, ts), 1)
    in_tile = row & (KEY_TILE - 1)
    key_row = (row - in_tile) + _key_of_row(in_tile)
    tile_row = lax.broadcasted_iota(jnp.int32, (KEY_TILE, KEY_TILE), 0)
    tile_col = lax.broadcasted_iota(jnp.int32, (KEY_TILE, KEY_TILE), 1)
    perm = jnp.where(tile_col == _key_of_row(tile_row), 1.0, 0.0).astype(BF16)
    h_keys = jnp.concatenate([_dot(perm, h[t * KEY_TILE:(t + 1) * KEY_TILE])
                              for t in range(ts // KEY_TILE)], axis=0).astype(BF16)

    qsb_ref[0] = _dot_nt(wq_ref[0], h).astype(BF16)
    qfx_ref[0] = _dot_nt(wq_ref[1], h).astype(BF16)
    gate_ref[0] = _dot_nt(wg_ref[...], h).astype(BF16)
    ksb_ref[0] = _dot(h_keys, wk_ref[0]).astype(BF16)
    kfx_ref[0] = _dot(h_keys, wk_ref[1]).astype(BF16)
    for o_ref, w in ((vsb_ref, wv_ref[0]), (vfx_ref, wv_ref[1])):
        v_t = _dot_nt(w, h_keys).astype(BF16)
        for t in range(ts // KEY_TILE):
            o_ref[0, t] = v_t[:, t * KEY_TILE:(t + 1) * KEY_TILE]

    zf = _dot(h, wf_ref[...]) + bf_ref[...]
    log_f_parts = _split_bf16(-_softplus(-zf), N_SPLIT)

    @pl.when(pl.program_id(1) == 0)
    def _():
        carry_ref[...] = jnp.zeros_like(carry_ref)

    carry = carry_ref[...]
    upto = jnp.where(col <= row, 1.0, 0.0).astype(BF16)
    upto_keys = jnp.where(col <= key_row, 1.0, 0.0).astype(BF16)
    sums = _dot(jnp.concatenate([upto, upto_keys], axis=0), jnp.concatenate(log_f_parts, axis=1))
    fold = lambda x: x[:, :LANES] + x[:, LANES:2 * LANES] + x[:, 2 * LANES:]
    cum = carry + fold(sums[:ts])
    cum_keys = carry + fold(sums[ts:])
    carry_ref[...] = cum[ts - 1:ts, :]

    ones = jnp.ones((ts, LANES), BF16)
    pieces = jnp.concatenate(_split_bf16(cum * LOG2E, N_SPLIT) + [ones], axis=1)
    pieces_keys = jnp.concatenate(_split_bf16(cum_keys * LOG2E, N_SPLIT) + [ones], axis=1)
    qaug_ref[0] = _dot_nt(pq_ref[...], pieces).astype(BF16)
    kaug_ref[0] = _dot(pieces_keys, pk_ref[...]).astype(BF16)


def _aug_constants():
    pq = np.zeros((D_GRP, (N_SPLIT + 1) * LANES), np.float32)
    pk = np.zeros(((N_SPLIT + 1) * LANES, D_GRP), np.float32)
    one_col = N_SPLIT * LANES
    for h in range(N_HEADS):
        base = (h // 2) * LANES + AUG_STRIDE * (h % 2)
        for j in range(N_SPLIT):
            pq[base + j, j * LANES + h] = 1.0
            pk[one_col, base + j] = 1.0
            pk[j * LANES + h, base + N_SPLIT + j] = -1.0
            pq[base + N_SPLIT + j, one_col] = 1.0
    return jnp.asarray(pq, BF16), jnp.asarray(pk, BF16)


def _project(x, ada_l, g_norm_l, wq, wk, wv, wg, wf, bf, consts, ts):
    b, s, d = x.shape
    pq, pk = consts
    const = lambda shape: pl.BlockSpec(shape, lambda i, j: (0,) * len(shape))
    row_major = pl.BlockSpec((1, ts, D_GRP), lambda i, j: (i, j, 0))
    feat_major = pl.BlockSpec((1, D_GRP, ts), lambda i, j: (i, 0, j))
    v_spec = pl.BlockSpec((1, ts // KEY_TILE, D_GRP, KEY_TILE), lambda i, j: (i, j, 0, 0))
    row_shape = jax.ShapeDtypeStruct((b, s, D_GRP), BF16)
    feat_shape = jax.ShapeDtypeStruct((b, D_GRP, s), BF16)
    v_shape = jax.ShapeDtypeStruct((b, s // KEY_TILE, D_GRP, KEY_TILE), BF16)
    return pl.pallas_call(
        functools.partial(_proj_kernel, d_model=d),
        grid=(b, s // ts),
        in_specs=[
            pl.BlockSpec((1, ts, d), lambda i, j: (i, j, 0)),
            pl.BlockSpec((1, 1, 3 * d), lambda i, j: (i, 0, 0)),
            const((1, d)),
            const(wq.shape), const(wk.shape), const(wv.shape), const(wg.shape), const(wf.shape),
            const(bf.shape), const(pq.shape), const(pk.shape),
        ],
        out_specs=[feat_major, row_major, v_spec, feat_major, row_major, v_spec,
                   pl.BlockSpec((1, 2 * D_GRP, ts), lambda i, j: (i, 0, j)), feat_major, row_major],
        out_shape=[feat_shape, row_shape, v_shape, feat_shape, row_shape, v_shape,
                   jax.ShapeDtypeStruct((b, 2 * D_GRP, s), BF16), feat_shape, row_shape],
        scratch_shapes=[pltpu.VMEM((1, LANES), F32)],
        compiler_params=pltpu.CompilerParams(
            dimension_semantics=("parallel", "arbitrary"), vmem_limit_bytes=VMEM_LIMIT),
        name="project",
    )(x, ada_l.reshape(b, 1, 3 * d), g_norm_l.reshape(1, d), wq, wk, wv, wg, wf, bf, pq, pk)


def _tile_iotas(tq):
    shape = (KEY_RUN, SUBLANES, tq)
    run = lax.broadcasted_iota(jnp.int32, shape, 0)
    sub = lax.broadcasted_iota(jnp.int32, shape, 1)
    qry = lax.broadcasted_iota(jnp.int32, shape, 2)
    return run, sub, qry


def _head_rows(block, j, width):
    row = lax.broadcasted_iota(jnp.int32, block.shape, 0)
    return jnp.where((row >= j * width) & (row < (j + 1) * width), block, jnp.zeros_like(block))


def _shift_up(x, k, fill):
    sub = lax.broadcasted_iota(jnp.int32, x.shape, 0)
    return jnp.where(sub + k < SUBLANES, pltpu.roll(x, SUBLANES - k, axis=0), fill)


def _tile_pipeline(qt, stage_scores, stage_weights, stage_values, state, alive=None, mask_first=True,
                   lowest=None):
    stage_scores(1, qt)
    state = stage_weights(1, 0, state, mask_first)
    stage_scores(0, jnp.maximum(qt - 1, 0))
    low = 0 if lowest is None else lowest(state)
    clamp = lambda kt: jnp.maximum(kt, low)

    def half(slot, kt, state):
        stage_scores(1 - slot, clamp(kt - 1))
        state = stage_values(slot, kt + 1, state, None)
        return stage_weights(slot, 1 - slot, state, False)

    n_bodies = (qt - low + 1) // 2
    if alive is None:
        def body(i, state):
            kt = qt - 1 - 2 * i
            return half(1, kt - 1, half(0, kt, state))

        state = lax.fori_loop(0, n_bodies, body, state)
        done = n_bodies
    else:
        def body(loop):
            i, _, state = loop
            kt = qt - 1 - 2 * i
            state = half(0, kt, state)
            go = alive(state, kt - 1)
            return i + 1, go, half(1, kt - 1, state)

        done, _, state = lax.while_loop(lambda loop: (loop[0] < n_bodies) & loop[1], body,
                                        (jnp.int32(0), alive(state, qt), state))
    last = qt - 2 * done
    return stage_values(0, clamp(last), state, last >= low)


def _attn_scratch(tq, slots=2):
    return [pltpu.VMEM((slots, 2, KEY_TILE, tq), F32), pltpu.VMEM((slots, 2, KEY_TILE, tq), BF16)]


def _sb_kernel(q_ref, k_ref, v_ref, o_ref, z_ref, a_ref, *, tq, group):
    run, sub, qry = _tile_iotas(tq)
    causal = sub * KEY_RUN + run < qry
    refs = (q_ref, k_ref, v_ref, o_ref, z_ref, a_ref)
    lax.fori_loop(0, q_ref.shape[2] // (tq * group),
                  lambda j, _: _sb_query_tiles([j * group + i for i in range(group)], causal, refs, tq), 0)


def _sb_query_tiles(qts, causal, refs, tq):
    q_ref, k_ref, v_ref, o_ref, z_ref, a_ref = refs

    def weights(half_z, carry, masked):
        th = jnp.tanh(half_z.reshape(KEY_RUN, SUBLANES, tq))
        rem = 0.5 - 0.5 * th
        if masked:
            rem = jnp.where(causal, rem, 1.0)
        prod = jnp.ones((SUBLANES, tq), F32)
        parts = [None] * KEY_RUN
        for r in reversed(range(KEY_RUN)):
            below = prod * rem[r]
            parts[r] = prod - below
            prod = below
        incl = prod
        for step in (1, 2, 4):
            incl = incl * _shift_up(incl, step, 1.0)
        scale = _shift_up(incl, 1, 1.0) * carry
        a = (jnp.stack(parts) * scale[None]).reshape(KEY_TILE, tq).astype(BF16)
        return a, carry * jnp.broadcast_to(incl[0:1], carry.shape)

    def stages(i, qt):
        base = 2 * i
        cols = pl.ds(pl.multiple_of(qt * tq, tq), tq)
        q_heads = [_head_rows(q_ref[0, :, cols], h, HEAD_DIM) for h in range(2)]

        def stage_scores(slot, kt):
            start = pl.multiple_of(kt * KEY_TILE, KEY_TILE)
            k = k_ref[0, pl.ds(start, KEY_TILE), :]
            for h in range(2):
                z_ref[base + slot, h] = _dot(k, q_heads[h])

        def stage_weights(src, dst, state, masked):
            carry, acc = state
            new_carry = []
            for h in range(2):
                a_ref[base + dst, h], c = weights(z_ref[base + src, h], carry[h], masked)
                new_carry.append(c)
            return tuple(new_carry), acc

        def stage_values(slot, kt, state, valid):
            carry, acc = state
            new_acc = []
            for h in range(2):
                av = _dot(v_ref[0, kt, h * HEAD_DIM:(h + 1) * HEAD_DIM, :], a_ref[base + slot, h])
                new_acc.append(acc[h] + (av if valid is None else jnp.where(valid, av, 0.0)))
            return carry, tuple(new_acc)

        return cols, stage_scores, stage_weights, stage_values

    alive = lambda st, kt: jnp.max(jnp.maximum(st[0][0], st[0][1])) > 0.0
    tiles = [(qt,) + stages(i, qt) for i, qt in enumerate(qts)]
    init = ((jnp.ones((SUBLANES, tq), F32),) * 2, (jnp.zeros((HEAD_DIM, tq), F32),) * 2)
    states = [init] * len(tiles)

    for qt, _, stage_scores, _, _ in tiles:
        stage_scores(1, qt)
        stage_scores(0, jnp.maximum(qt - 1, 0))
    for i, (qt, _, _, stage_weights, _) in enumerate(tiles):
        states[i] = stage_weights(0, 0, stage_weights(1, 1, states[i], True), False)
    for i, (qt, _, _, _, stage_values) in enumerate(tiles):
        states[i] = stage_values(0, jnp.maximum(qt - 1, 0), stage_values(1, qt, states[i], None), qt >= 1)

    for i, (qt, cols, stage_scores, stage_weights, stage_values) in enumerate(tiles):
        def rest(state, qt=qt, fns=(stage_scores, stage_weights, stage_values)):
            return _tile_pipeline(qt - 2, *fns, state, alive, mask_first=False)

        _, acc = lax.cond((qt >= 2) & alive(states[i], qt - 1), rest, lambda st: st, states[i])
        o_ref[0, :, cols] = jnp.concatenate(acc, axis=0).astype(BF16)
    return 0


def _sb_attention(q_t, k, v_t, tq):
    b, _, s = q_t.shape
    q_spec = pl.BlockSpec((1, LANES, s), lambda i, p: (i, p, 0))
    group = 2 if (s // tq) % 2 == 0 else 1
    return pl.pallas_call(
        functools.partial(_sb_kernel, tq=tq, group=group),
        grid=(b, N_PAIRS),
        in_specs=[
            q_spec,
            pl.BlockSpec((1, s, LANES), lambda i, p: (i, 0, p)),
            pl.BlockSpec((1, s // KEY_TILE, LANES, KEY_TILE), lambda i, p: (i, 0, p, 0)),
        ],
        out_specs=q_spec,
        out_shape=jax.ShapeDtypeStruct((b, D_GRP, s), BF16),
        scratch_shapes=_attn_scratch(tq, 2 * group),
        compiler_params=pltpu.CompilerParams(
            dimension_semantics=("parallel", "parallel"), vmem_limit_bytes=VMEM_LIMIT),
        name="sb_attention",
    )(q_t, k, v_t)


def _fox_kernel(q_ref, qa_ref, k_ref, ka_ref, v_ref, o_ref, z_ref, p_ref, tails_ref, *, tq):
    run, sub, qry = _tile_iotas(tq)
    causal = (sub * KEY_RUN + run <= qry).reshape(KEY_TILE, tq)

    n_tiles = k_ref.shape[1] // KEY_TILE
    tails_ref[...] = jnp.zeros(tails_ref.shape, F32)

    def scan_keys(t, k_abs):
        start = pl.multiple_of(t * KEY_TILE, KEY_TILE)
        k = k_ref[0, pl.ds(start, KEY_TILE), :]
        tail = ka_ref[0, pl.ds(start + KEY_TILE - 2 * SUBLANES, 2 * SUBLANES), :].astype(F32)
        tails_ref[pl.ds(t, 1), :] = tail[2 * SUBLANES - 1:, :]
        return jnp.maximum(k_abs, jnp.max(jnp.abs(k.astype(F32)), axis=0, keepdims=True))

    k_abs = lax.fori_loop(0, n_tiles, scan_keys, jnp.zeros((1, LANES), F32))
    feat = lax.broadcasted_iota(jnp.int32, (SUBLANES, LANES), 1)
    head = lax.broadcasted_iota(jnp.int32, (SUBLANES, LANES), 0)
    first = head * AUG_STRIDE + N_SPLIT
    pick = jnp.where((feat >= first) & (feat < first + N_SPLIT) & (head < 2), 1.0, 0.0).astype(BF16)
    neg_cum_rows = _dot_nt(pick, tails_ref[...].astype(BF16))
    neg_cum = [neg_cum_rows[h:h + 1, :] for h in range(2)]
    k_bound = [HEAD_DIM ** 0.5 * jnp.max(k_abs[:, h * HEAD_DIM:(h + 1) * HEAD_DIM], axis=1, keepdims=True)
               for h in range(2)]
    lax.fori_loop(0, q_ref.shape[2] // tq,
                  lambda qt, _: _fox_query_tile(qt, causal, k_bound, neg_cum, q_ref, qa_ref, k_ref, ka_ref,
                                                v_ref, o_ref, z_ref, p_ref, tq), 0)


def _fox_query_tile(qt, causal, k_bound, neg_cum, q_ref, qa_ref, k_ref, ka_ref, v_ref, o_ref, z_ref, p_ref,
                    tq):
    cols = pl.ds(pl.multiple_of(qt * tq, tq), tq)
    q_heads = [jnp.concatenate([_head_rows(q_ref[0, :, cols], h, HEAD_DIM),
                                _head_rows(qa_ref[0, :, cols], h, AUG_STRIDE)], axis=0) for h in range(2)]

    def stage_scores(slot, kt):
        start = pl.multiple_of(kt * KEY_TILE, KEY_TILE)
        k = jnp.concatenate([k_ref[0, pl.ds(start, KEY_TILE), :],
                             ka_ref[0, pl.ds(start, KEY_TILE), :]], axis=1)
        for h in range(2):
            z_ref[slot, h] = _dot(k, q_heads[h])

    def stage_weights(src, dst, state, masked):
        m, l, _, _, acc = state
        m_new, l_new, alpha = [], [], []
        for h in range(2):
            logits = z_ref[src, h]
            if masked:
                logits = jnp.where(causal, logits, -jnp.inf)
            mh = jnp.maximum(m[h], jnp.max(logits, axis=0, keepdims=True))
            p = jnp.exp2(logits - mh)
            ah = jnp.exp2(m[h] - mh)
            p_ref[dst, h] = p.astype(BF16)
            m_new.append(mh)
            alpha.append(ah)
            l_new.append(ah * l[h] + jnp.sum(p, axis=0, keepdims=True))
        return tuple(m_new), tuple(l_new), l, tuple(alpha), acc

    def stage_values(slot, kt, state, valid):
        m, l, l_prev, alpha, acc = state
        new_acc = []
        for h in range(2):
            pv = _dot(v_ref[0, kt, h * HEAD_DIM:(h + 1) * HEAD_DIM, :], p_ref[slot, h])
            upd = alpha[h] * acc[h] + pv
            new_acc.append(upd if valid is None else jnp.where(valid, upd, acc[h]))
        if valid is not None:
            l = tuple(jnp.where(valid, l[h], l_prev[h]) for h in range(2))
        return m, l, l_prev, alpha, tuple(new_acc)

    q32 = q_ref[0, :, cols].astype(F32)
    qa32 = qa_ref[0, :, cols].astype(F32)
    reach = []
    for h in range(2):
        q_norm = jnp.sqrt(jnp.sum(jnp.square(q32[h * HEAD_DIM:(h + 1) * HEAD_DIM]), axis=0, keepdims=True))
        cum_q = jnp.sum(qa32[h * AUG_STRIDE:h * AUG_STRIDE + N_SPLIT], axis=0, keepdims=True)
        reach.append(q_norm * k_bound[h] + cum_q)

    def lowest(state):
        m = state[0]
        tile = lax.broadcasted_iota(jnp.int32, (1, LANES), 1)
        first = []
        for h in range(2):
            gap = jnp.max(reach[h] - m[h], axis=1, keepdims=True)
            needed = (gap + neg_cum[h] >= -UNDERFLOW_BITS) | (tile >= qt)
            first.append(jnp.min(jnp.where(needed, tile, LANES).astype(F32), axis=1, keepdims=True))
        return jnp.minimum(first[0], first[1])[0, 0].astype(jnp.int32)

    row = lambda v: (jnp.full((1, tq), v, F32),) * 2
    state = (row(-jnp.inf), row(0.0), row(0.0), row(1.0), (jnp.zeros((HEAD_DIM, tq), F32),) * 2)
    _, l, _, _, acc = _tile_pipeline(qt, stage_scores, stage_weights, stage_values, state, lowest=lowest)
    o_ref[0, :, cols] = jnp.concatenate([acc[h] / l[h] for h in range(2)], axis=0).astype(BF16)
    return 0


def _fox_attention(q_t, qaug_t, k, kaug, v_t, tq):
    b, _, s = q_t.shape
    assert s // KEY_TILE <= LANES
    q_spec = pl.BlockSpec((1, LANES, s), lambda i, p: (i, p, 0))
    k_spec = pl.BlockSpec((1, s, LANES), lambda i, p: (i, 0, p))
    return pl.pallas_call(
        functools.partial(_fox_kernel, tq=tq),
        grid=(b, N_PAIRS),
        in_specs=[q_spec, q_spec, k_spec, k_spec,
                  pl.BlockSpec((1, s // KEY_TILE, LANES, KEY_TILE), lambda i, p: (i, 0, p, 0))],
        out_specs=q_spec,
        out_shape=jax.ShapeDtypeStruct((b, D_GRP, s), BF16),
        scratch_shapes=_attn_scratch(tq) + [pltpu.VMEM((LANES, LANES), F32)],
        compiler_params=pltpu.CompilerParams(
            dimension_semantics=("parallel", "parallel"), vmem_limit_bytes=VMEM_LIMIT),
        name="fox_attention",
    )(q_t, qaug_t, k, kaug, v_t)


def _rms_rows(x):
    return x * lax.rsqrt(jnp.mean(x * x, axis=0, keepdims=True) + EPS)


def _out_kernel(osb_ref, ofx_ref, gate_ref, x_ref, wout_ref, ada_ref, gf_ref, o_ref, *, d_model, final):
    y = jnp.concatenate([_rms_rows(osb_ref[0].astype(F32)), _rms_rows(ofx_ref[0].astype(F32))], axis=0)
    g = gate_ref[0].astype(F32)
    y = (y * (g / (1.0 + jnp.exp(-g)))).astype(BF16)
    gate = ada_ref[0][:, 2 * d_model:]
    out = x_ref[0] + (1.0 + gate) * _dot_tn(y, wout_ref[...])
    if final:
        out = out * lax.rsqrt(jnp.mean(out * out, axis=-1, keepdims=True) + EPS) * gf_ref[...]
    o_ref[0] = out


def _output(o_sb, o_fx, gate_t, x, wout, ada_l, g_final, ts, final):
    b, s, d = x.shape
    const = lambda shape: pl.BlockSpec(shape, lambda i, j: (0,) * len(shape))
    return pl.pallas_call(
        functools.partial(_out_kernel, d_model=d, final=final),
        grid=(b, s // ts),
        in_specs=[
            pl.BlockSpec((1, D_GRP, ts), lambda i, j: (i, 0, j)),
            pl.BlockSpec((1, D_GRP, ts), lambda i, j: (i, 0, j)),
            pl.BlockSpec((1, 2 * D_GRP, ts), lambda i, j: (i, 0, j)),
            pl.BlockSpec((1, ts, d), lambda i, j: (i, j, 0)),
            const(wout.shape),
            pl.BlockSpec((1, 1, 3 * d), lambda i, j: (i, 0, 0)),
            const((1, d)),
        ],
        out_specs=pl.BlockSpec((1, ts, d), lambda i, j: (i, j, 0)),
        out_shape=jax.ShapeDtypeStruct((b, s, d), F32),
        compiler_params=pltpu.CompilerParams(
            dimension_semantics=("parallel", "parallel"), vmem_limit_bytes=VMEM_LIMIT),
        name="output",
    )(o_sb, o_fx, gate_t, x, wout, ada_l.reshape(b, 1, 3 * d), g_final.reshape(1, d))


def kernel(x, c, w_ada, b_ada, g_norm, w_in, b_f, g_grp, w_out, g_final):
    b, s, d = x.shape
    depth = w_ada.shape[0]
    ts = min(512, s)
    tq = min(256, s)
    consts = _aug_constants()

    grp = lambda i: w_in[:, :, i * D_GRP:(i + 1) * D_GRP]
    t = lambda w: jnp.swapaxes(w, 1, 2)
    wq = jnp.stack([t(grp(0)) * (0.5 * Q_SCALE), t(grp(3)) * (LOG2E * Q_SCALE)], axis=1).astype(BF16)
    wk = jnp.stack([grp(1), grp(4)], axis=1).astype(BF16)
    wv = jnp.stack([t(grp(2)), t(grp(5))], axis=1).astype(BF16)
    wg = t(w_in[:, :, 6 * D_GRP:8 * D_GRP]).astype(BF16)
    wf = jnp.zeros((depth, d, LANES), BF16).at[:, :, :N_HEADS].set(w_in[:, :, 8 * D_GRP:].astype(BF16))
    bf = jnp.zeros((depth, 1, LANES), F32).at[:, 0, :N_HEADS].set(b_f)
    wout = (g_grp[:, :, None] * w_out).astype(BF16)

    ada = _adaln(c, w_ada, b_ada)
    for l in range(depth):
        q_sb, k_sb, v_sb, q_fx, k_fx, v_fx, gate_t, qaug, kaug = _project(
            x, ada[l], g_norm[l], wq[l], wk[l], wv[l], wg[l], wf[l], bf[l], consts, ts)
        o_sb = _sb_attention(q_sb, k_sb, v_sb, tq)
        o_fx = _fox_attention(q_fx, qaug, k_fx, kaug, v_fx, tq)
        x = _output(o_sb, o_fx, gate_t, x, wout[l], ada[l], g_final, ts, l == depth - 1)
    return x
```

```python
import functools

import numpy as np
import jax
import jax.numpy as jnp
from jax import lax
from jax.experimental import pallas as pl
from jax.experimental.pallas import tpu as pltpu

F32 = jnp.float32
BF16 = jnp.bfloat16

HEAD_DIM = 64
N_HEADS = 8
D_GRP = N_HEADS * HEAD_DIM
LANES = 128
SUBLANES = 8
N_PAIRS = D_GRP // LANES
KEY_TILE = 256
KEY_RUN = KEY_TILE // SUBLANES
AUG_STRIDE = 16
N_SPLIT = 3
EPS = 1e-6
Q_SCALE = HEAD_DIM ** -0.5
LOG2E = 1.4426950408889634
VMEM_LIMIT = 52 * 1024 * 1024
UNDERFLOW_BITS = 160.0


def _split_bf16(x, n):
    parts = []
    r = x
    for i in range(n):
        p = r.astype(BF16)
        parts.append(p)
        if i + 1 < n:
            r = r - p.astype(F32)
    return parts


def _dot(a, b):
    return jnp.dot(a, b, preferred_element_type=F32)


def _dot_nt(a, b):
    return lax.dot_general(a, b, (((1,), (1,)), ((), ())), preferred_element_type=F32)


def _dot_tn(a, b):
    return lax.dot_general(a, b, (((0,), (0,)), ((), ())), preferred_element_type=F32)


def _softplus(z):
    return jnp.maximum(z, 0.0) + jnp.log1p(jnp.exp(-jnp.abs(z)))


def _key_of_row(p):
    return (p & (SUBLANES - 1)) * KEY_RUN + (p >> 3)


def _ada_kernel(c_ref, w_ref, b_ref, o_ref):
    c = c_ref[...]
    c_act = c / (1.0 + jnp.exp(-c))
    w_parts = _split_bf16(w_ref[0], 2)
    acc = jnp.zeros(o_ref.shape[1:], F32)
    for cp in _split_bf16(c_act, N_SPLIT):
        for wp in w_parts:
            acc = acc + _dot(cp, wp)
    o_ref[0] = acc + b_ref[0]


def _adaln(c, w_ada, b_ada):
    depth, d, d3 = w_ada.shape
    b = c.shape[0]
    rows = SUBLANES
    tn = 1024
    c_pad = jnp.zeros((rows, d), F32).at[:b].set(c)
    out = pl.pallas_call(
        _ada_kernel,
        grid=(depth, d3 // tn),
        in_specs=[
            pl.BlockSpec((rows, d), lambda l, n: (0, 0)),
            pl.BlockSpec((1, d, tn), lambda l, n: (l, 0, n)),
            pl.BlockSpec((1, 1, tn), lambda l, n: (l, 0, n)),
        ],
        out_specs=pl.BlockSpec((1, rows, tn), lambda l, n: (l, 0, n)),
        out_shape=jax.ShapeDtypeStruct((depth, rows, d3), F32),
        compiler_params=pltpu.CompilerParams(
            dimension_semantics=("parallel", "parallel"), vmem_limit_bytes=VMEM_LIMIT),
        name="adaln",
    )(c_pad, w_ada, b_ada.reshape(depth, 1, d3))
    return out[:, :b]


def _proj_kernel(x_ref, ada_ref, gn_ref, wq_ref, wk_ref, wv_ref, wg_ref, wf_ref, bf_ref,
                 pq_ref, pk_ref,
                 qsb_ref, ksb_ref, vsb_ref, qfx_ref, kfx_ref, vfx_ref, gate_ref, qaug_ref, kaug_ref,
                 carry_ref, *, d_model):
    ts = x_ref.shape[1]
    x = x_ref[0]
    ada = ada_ref[0]
    shift = ada[:, :d_model]
    scale = ada[:, d_model:2 * d_model]
    r = lax.rsqrt(jnp.mean(x * x, axis=-1, keepdims=True) + EPS)
    h = ((x * r) * gn_ref[...] * (1.0 + scale) + shift).astype(BF16)

    row = lax.broadcasted_iota(jnp.int32, (ts, ts), 0)
    col = lax.broadcasted_iota(jnp.int32, (ts, ts), 1)
    in_tile = row & (KEY_TILE - 1)
    key_row = (row - in_tile) + _key_of_row(in_tile)
    tile_row = lax.broadcasted_iota(jnp.int32, (KEY_TILE, KEY_TILE), 0)
    tile_col = lax.broadcasted_iota(jnp.int32, (KEY_TILE, KEY_TILE), 1)
    perm = jnp.where(tile_col == _key_of_row(tile_row), 1.0, 0.0).astype(BF16)
    h_keys = jnp.concatenate([_dot(perm, h[t * KEY_TILE:(t + 1) * KEY_TILE])
                              for t in range(ts // KEY_TILE)], axis=0).astype(BF16)

    qsb_ref[0] = _dot_nt(wq_ref[0], h).astype(BF16)
    qfx_ref[0] = _dot_nt(wq_ref[1], h).astype(BF16)
    gate_ref[0] = _dot_nt(wg_ref[...], h).astype(BF16)
    ksb_ref[0] = _dot(h_keys, wk_ref[0]).astype(BF16)
    kfx_ref[0] = _dot(h_keys, wk_ref[1]).astype(BF16)
    for o_ref, w in ((vsb_ref, wv_ref[0]), (vfx_ref, wv_ref[1])):
        v_t = _dot_nt(w, h_keys).astype(BF16)
        for t in range(ts // KEY_TILE):
            o_ref[0, t] = v_t[:, t * KEY_TILE:(t + 1) * KEY_TILE]

    zf = _dot(h, wf_ref[...]) + bf_ref[...]
    log_f_parts = _split_bf16(-_softplus(-zf), N_SPLIT)

    @pl.when(pl.program_id(1) == 0)
    def _():
        carry_ref[...] = jnp.zeros_like(carry_ref)

    carry = carry_ref[...]
    upto = jnp.where(col <= row, 1.0, 0.0).astype(BF16)
    upto_keys = jnp.where(col <= key_row, 1.0, 0.0).astype(BF16)
    sums = _dot(jnp.concatenate([upto, upto_keys], axis=0), jnp.concatenate(log_f_parts, axis=1))
    fold = lambda x: x[:, :LANES] + x[:, LANES:2 * LANES] + x[:, 2 * LANES:]
    cum = carry + fold(sums[:ts])
    cum_keys = carry + fold(sums[ts:])
    carry_ref[...] = cum[ts - 1:ts, :]

    ones = jnp.ones((ts, LANES), BF16)
    pieces = jnp.concatenate(_split_bf16(cum * LOG2E, N_SPLIT) + [ones], axis=1)
    pieces_keys = jnp.concatenate(_split_bf16(cum_keys * LOG2E, N_SPLIT) + [ones], axis=1)
    qaug_ref[0] = _dot_nt(pq_ref[...], pieces).astype(BF16)
    kaug_ref[0] = _dot(pieces_keys, pk_ref[...]).astype(BF16)


def _aug_constants():
    pq = np.zeros((D_GRP, (N_SPLIT + 1) * LANES), np.float32)
    pk = np.zeros(((N_SPLIT + 1) * LANES, D_GRP), np.float32)
    one_col = N_SPLIT * LANES
    for h in range(N_HEADS):
        base = (h // 2) * LANES + AUG_STRIDE * (h % 2)
        for j in range(N_SPLIT):
            pq[base + j, j * LANES + h] = 1.0
            pk[one_col, base + j] = 1.0
            pk[j * LANES + h, base + N_SPLIT + j] = -1.0
            pq[base + N_SPLIT + j, one_col] = 1.0
    return jnp.asarray(pq, BF16), jnp.asarray(pk, BF16)


def _project(x, ada_l, g_norm_l, wq, wk, wv, wg, wf, bf, consts, ts):
    b, s, d = x.shape
    pq, pk = consts
    const = lambda shape: pl.BlockSpec(shape, lambda i, j: (0,) * len(shape))
    row_major = pl.BlockSpec((1, ts, D_GRP), lambda i, j: (i, j, 0))
    feat_major = pl.BlockSpec((1, D_GRP, ts), lambda i, j: (i, 0, j))
    v_spec = pl.BlockSpec((1, ts // KEY_TILE, D_GRP, KEY_TILE), lambda i, j: (i, j, 0, 0))
    row_shape = jax.ShapeDtypeStruct((b, s, D_GRP), BF16)
    feat_shape = jax.ShapeDtypeStruct((b, D_GRP, s), BF16)
    v_shape = jax.ShapeDtypeStruct((b, s // KEY_TILE, D_GRP, KEY_TILE), BF16)
    return pl.pallas_call(
        functools.partial(_proj_kernel, d_model=d),
        grid=(b, s // ts),
        in_specs=[
            pl.BlockSpec((1, ts, d), lambda i, j: (i, j, 0)),
            pl.BlockSpec((1, 1, 3 * d), lambda i, j: (i, 0, 0)),
            const((1, d)),
            const(wq.shape), const(wk.shape), const(wv.shape), const(wg.shape), const(wf.shape),
            const(bf.shape), const(pq.shape), const(pk.shape),
        ],
        out_specs=[feat_major, row_major, v_spec, feat_major, row_major, v_spec,
                   pl.BlockSpec((1, 2 * D_GRP, ts), lambda i, j: (i, 0, j)), feat_major, row_major],
        out_shape=[feat_shape, row_shape, v_shape, feat_shape, row_shape, v_shape,
                   jax.ShapeDtypeStruct((b, 2 * D_GRP, s), BF16), feat_shape, row_shape],
        scratch_shapes=[pltpu.VMEM((1, LANES), F32)],
        compiler_params=pltpu.CompilerParams(
            dimension_semantics=("parallel", "arbitrary"), vmem_limit_bytes=VMEM_LIMIT),
        name="project",
    )(x, ada_l.reshape(b, 1, 3 * d), g_norm_l.reshape(1, d), wq, wk, wv, wg, wf, bf, pq, pk)


def _tile_iotas(tq):
    shape = (KEY_RUN, SUBLANES, tq)
    run = lax.broadcasted_iota(jnp.int32, shape, 0)
    sub = lax.broadcasted_iota(jnp.int32, shape, 1)
    qry = lax.broadcasted_iota(jnp.int32, shape, 2)
    return run, sub, qry


def _head_rows(block, j, width):
    row = lax.broadcasted_iota(jnp.int32, block.shape, 0)
    return jnp.where((row >= j * width) & (row < (j + 1) * width), block, jnp.zeros_like(block))


def _shift_up(x, k, fill):
    sub = lax.broadcasted_iota(jnp.int32, x.shape, 0)
    return jnp.where(sub + k < SUBLANES, pltpu.roll(x, SUBLANES - k, axis=0), fill)


def _pipeline_start(qt, stages, state, mask_first=True, lowest=None):
    stage_scores, stage_weights, _ = stages
    stage_scores(1, qt)
    state = stage_weights(1, 0, state, mask_first)
    stage_scores(0, jnp.maximum(qt - 1, 0))
    return state, (0 if lowest is None else lowest(state))


def _pipeline_loop(qt, stages, state, low, alive=None):
    stage_scores, stage_weights, stage_values = stages

    def half(slot, kt, state):
        stage_scores(1 - slot, jnp.maximum(kt - 1, low))
        state = stage_values(slot, kt + 1, state, None)
        return stage_weights(slot, 1 - slot, state, False)

    n_bodies = (qt - low + 1) // 2
    if alive is None:
        def body(i, state):
            kt = qt - 1 - 2 * i
            return half(1, kt - 1, half(0, kt, state))

        return lax.fori_loop(0, n_bodies, body, state), n_bodies

    def body(loop):
        i, _, state = loop
        kt = qt - 1 - 2 * i
        state = half(0, kt, state)
        go = alive(state, kt - 1)
        return i + 1, go, half(1, kt - 1, state)

    done, _, state = lax.while_loop(lambda loop: (loop[0] < n_bodies) & loop[1], body,
                                    (jnp.int32(0), alive(state, qt), state))
    return state, done


def _pipeline_finish(qt, stages, state, low, done):
    last = qt - 2 * done
    return stages[2](0, jnp.maximum(last, low), state, last >= low)


def _tile_pipeline(qt, stages, state, alive=None, mask_first=True, lowest=None):
    state, low = _pipeline_start(qt, stages, state, mask_first, lowest)
    state, done = _pipeline_loop(qt, stages, state, low, alive)
    return _pipeline_finish(qt, stages, state, low, done)


def _attn_scratch(tq, slots=2):
    return [pltpu.VMEM((slots, 2, KEY_TILE, tq), F32), pltpu.VMEM((slots, 2, KEY_TILE, tq), BF16)]


def _sb_kernel(q_ref, k_ref, v_ref, o_ref, z_ref, a_ref, *, tq, group):
    run, sub, qry = _tile_iotas(tq)
    causal = sub * KEY_RUN + run < qry
    refs = (q_ref, k_ref, v_ref, o_ref, z_ref, a_ref)
    lax.fori_loop(0, q_ref.shape[2] // (tq * group),
                  lambda j, _: _sb_query_tiles([j * group + i for i in range(group)], causal, refs, tq), 0)


def _sb_query_tiles(qts, causal, refs, tq):
    q_ref, k_ref, v_ref, o_ref, z_ref, a_ref = refs

    def weights(half_z, carry, masked):
        th = jnp.tanh(half_z.reshape(KEY_RUN, SUBLANES, tq))
        rem = 0.5 - 0.5 * th
        if masked:
            rem = jnp.where(causal, rem, 1.0)
        prod = jnp.ones((SUBLANES, tq), F32)
        parts = [None] * KEY_RUN
        for r in reversed(range(KEY_RUN)):
            below = prod * rem[r]
            parts[r] = prod - below
            prod = below
        incl = prod
        for step in (1, 2, 4):
            incl = incl * _shift_up(incl, step, 1.0)
        scale = _shift_up(incl, 1, 1.0) * carry
        a = (jnp.stack(parts) * scale[None]).reshape(KEY_TILE, tq).astype(BF16)
        return a, carry * jnp.broadcast_to(incl[0:1], carry.shape)

    def stages(i, qt):
        base = 2 * i
        cols = pl.ds(pl.multiple_of(qt * tq, tq), tq)
        q_heads = [_head_rows(q_ref[0, :, cols], h, HEAD_DIM) for h in range(2)]

        def stage_scores(slot, kt):
            start = pl.multiple_of(kt * KEY_TILE, KEY_TILE)
            k = k_ref[0, pl.ds(start, KEY_TILE), :]
            for h in range(2):
                z_ref[base + slot, h] = _dot(k, q_heads[h])

        def stage_weights(src, dst, state, masked):
            carry, acc = state
            new_carry = []
            for h in range(2):
                a_ref[base + dst, h], c = weights(z_ref[base + src, h], carry[h], masked)
                new_carry.append(c)
            return tuple(new_carry), acc

        def stage_values(slot, kt, state, valid):
            carry, acc = state
            new_acc = []
            for h in range(2):
                av = _dot(v_ref[0, kt, h * HEAD_DIM:(h + 1) * HEAD_DIM, :], a_ref[base + slot, h])
                new_acc.append(acc[h] + (av if valid is None else jnp.where(valid, av, 0.0)))
            return carry, tuple(new_acc)

        return cols, stage_scores, stage_weights, stage_values

    alive = lambda st, kt: jnp.max(jnp.maximum(st[0][0], st[0][1])) > 0.0
    tiles = [(qt,) + stages(i, qt) for i, qt in enumerate(qts)]
    init = ((jnp.ones((SUBLANES, tq), F32),) * 2, (jnp.zeros((HEAD_DIM, tq), F32),) * 2)
    states = [init] * len(tiles)

    for qt, _, stage_scores, _, _ in tiles:
        stage_scores(1, qt)
        stage_scores(0, jnp.maximum(qt - 1, 0))
    for i, (qt, _, _, stage_weights, _) in enumerate(tiles):
        states[i] = stage_weights(0, 0, stage_weights(1, 1, states[i], True), False)
    for i, (qt, _, _, _, stage_values) in enumerate(tiles):
        states[i] = stage_values(0, jnp.maximum(qt - 1, 0), stage_values(1, qt, states[i], None), qt >= 1)

    for i, (qt, cols, stage_scores, stage_weights, stage_values) in enumerate(tiles):
        def rest(state, qt=qt, fns=(stage_scores, stage_weights, stage_values)):
            return _tile_pipeline(qt - 2, fns, state, alive, mask_first=False)

        _, acc = lax.cond((qt >= 2) & alive(states[i], qt - 1), rest, lambda st: st, states[i])
        o_ref[0, :, cols] = jnp.concatenate(acc, axis=0).astype(BF16)
    return 0


def _sb_attention(q_t, k, v_t, tq):
    b, _, s = q_t.shape
    q_spec = pl.BlockSpec((1, LANES, s), lambda i, p: (i, p, 0))
    group = 2 if (s // tq) % 2 == 0 else 1
    return pl.pallas_call(
        functools.partial(_sb_kernel, tq=tq, group=group),
        grid=(b, N_PAIRS),
        in_specs=[
            q_spec,
            pl.BlockSpec((1, s, LANES), lambda i, p: (i, 0, p)),
            pl.BlockSpec((1, s // KEY_TILE, LANES, KEY_TILE), lambda i, p: (i, 0, p, 0)),
        ],
        out_specs=q_spec,
        out_shape=jax.ShapeDtypeStruct((b, D_GRP, s), BF16),
        scratch_shapes=_attn_scratch(tq, 2 * group),
        compiler_params=pltpu.CompilerParams(
            dimension_semantics=("parallel", "parallel"), vmem_limit_bytes=VMEM_LIMIT),
        name="sb_attention",
    )(q_t, k, v_t)


def _fox_kernel(q_ref, qa_ref, k_ref, ka_ref, v_ref, o_ref, z_ref, p_ref, tails_ref, *, tq, group):
    run, sub, qry = _tile_iotas(tq)
    causal = (sub * KEY_RUN + run <= qry).reshape(KEY_TILE, tq)

    n_tiles = k_ref.shape[1] // KEY_TILE
    tails_ref[...] = jnp.zeros(tails_ref.shape, F32)

    def scan_keys(t, k_abs):
        start = pl.multiple_of(t * KEY_TILE, KEY_TILE)
        k = k_ref[0, pl.ds(start, KEY_TILE), :]
        tail = ka_ref[0, pl.ds(start + KEY_TILE - 2 * SUBLANES, 2 * SUBLANES), :].astype(F32)
        tails_ref[pl.ds(t, 1), :] = tail[2 * SUBLANES - 1:, :]
        return jnp.maximum(k_abs, jnp.max(jnp.abs(k.astype(F32)), axis=0, keepdims=True))

    k_abs = lax.fori_loop(0, n_tiles, scan_keys, jnp.zeros((1, LANES), F32))
    feat = lax.broadcasted_iota(jnp.int32, (SUBLANES, LANES), 1)
    head = lax.broadcasted_iota(jnp.int32, (SUBLANES, LANES), 0)
    first = head * AUG_STRIDE + N_SPLIT
    pick = jnp.where((feat >= first) & (feat < first + N_SPLIT) & (head < 2), 1.0, 0.0).astype(BF16)
    neg_cum_rows = _dot_nt(pick, tails_ref[...].astype(BF16))
    neg_cum = [neg_cum_rows[h:h + 1, :] for h in range(2)]
    k_bound = [HEAD_DIM ** 0.5 * jnp.max(k_abs[:, h * HEAD_DIM:(h + 1) * HEAD_DIM], axis=1, keepdims=True)
               for h in range(2)]
    refs = (q_ref, qa_ref, k_ref, ka_ref, v_ref, o_ref, z_ref, p_ref)
    lax.fori_loop(0, q_ref.shape[2] // (tq * group),
                  lambda j, _: _fox_query_tiles([j * group + i for i in range(group)], causal, k_bound,
                                                neg_cum, refs, tq), 0)


def _fox_query_tiles(qts, causal, k_bound, neg_cum, refs, tq):
    q_ref, qa_ref, k_ref, ka_ref, v_ref, o_ref, z_ref, p_ref = refs

    def stages(i, qt):
        base = 2 * i
        cols = pl.ds(pl.multiple_of(qt * tq, tq), tq)
        q_heads = [jnp.concatenate([_head_rows(q_ref[0, :, cols], h, HEAD_DIM),
                                    _head_rows(qa_ref[0, :, cols], h, AUG_STRIDE)], axis=0)
                   for h in range(2)]

        def stage_scores(slot, kt):
            start = pl.multiple_of(kt * KEY_TILE, KEY_TILE)
            k = jnp.concatenate([k_ref[0, pl.ds(start, KEY_TILE), :],
                                 ka_ref[0, pl.ds(start, KEY_TILE), :]], axis=1)
            for h in range(2):
                z_ref[base + slot, h] = _dot(k, q_heads[h])

        def stage_weights(src, dst, state, masked):
            m, l, _, _, acc = state
            m_new, l_new, alpha = [], [], []
            for h in range(2):
                logits = z_ref[base + src, h]
                if masked:
                    logits = jnp.where(causal, logits, -jnp.inf)
                mh = jnp.maximum(m[h], jnp.max(logits, axis=0, keepdims=True))
                p = jnp.exp2(logits - mh)
                ah = jnp.exp2(m[h] - mh)
                p_ref[base + dst, h] = p.astype(BF16)
                m_new.append(mh)
                alpha.append(ah)
                l_new.append(ah * l[h] + jnp.sum(p, axis=0, keepdims=True))
            return tuple(m_new), tuple(l_new), l, tuple(alpha), acc

        def stage_values(slot, kt, state, valid):
            m, l, l_prev, alpha, acc = state
            new_acc = []
            for h in range(2):
                pv = _dot(v_ref[0, kt, h * HEAD_DIM:(h + 1) * HEAD_DIM, :], p_ref[base + slot, h])
                upd = alpha[h] * acc[h] + pv
                new_acc.append(upd if valid is None else jnp.where(valid, upd, acc[h]))
            if valid is not None:
                l = tuple(jnp.where(valid, l[h], l_prev[h]) for h in range(2))
            return m, l, l_prev, alpha, tuple(new_acc)

        q32 = q_ref[0, :, cols].astype(F32)
        qa32 = qa_ref[0, :, cols].astype(F32)
        reach = []
        for h in range(2):
            q_norm = jnp.sqrt(jnp.sum(jnp.square(q32[h * HEAD_DIM:(h + 1) * HEAD_DIM]), axis=0, keepdims=True))
            cum_q = jnp.sum(qa32[h * AUG_STRIDE:h * AUG_STRIDE + N_SPLIT], axis=0, keepdims=True)
            reach.append(q_norm * k_bound[h] + cum_q)

        def lowest(state):
            m = state[0]
            tile = lax.broadcasted_iota(jnp.int32, (1, LANES), 1)
            first = []
            for h in range(2):
                gap = jnp.max(reach[h] - m[h], axis=1, keepdims=True)
                needed = (gap + neg_cum[h] >= -UNDERFLOW_BITS) | (tile >= qt)
                first.append(jnp.min(jnp.where(needed, tile, LANES).astype(F32), axis=1, keepdims=True))
            return jnp.minimum(first[0], first[1])[0, 0].astype(jnp.int32)

        return cols, (stage_scores, stage_weights, stage_values), lowest

    row = lambda v: (jnp.full((1, tq), v, F32),) * 2
    init = (row(-jnp.inf), row(0.0), row(0.0), row(1.0), (jnp.zeros((HEAD_DIM, tq), F32),) * 2)
    tiles = [(qt,) + stages(i, qt) for i, qt in enumerate(qts)]
    started = [_pipeline_start(qt, fns, init, True, lowest) for qt, _, fns, lowest in tiles]
    looped = [_pipeline_loop(qt, fns, state, low) for (qt, _, fns, _), (state, low) in zip(tiles, started)]
    for (qt, cols, fns, _), (_, low), (state, done) in zip(tiles, started, looped):
        _, l, _, _, acc = _pipeline_finish(qt, fns, state, low, done)
        o_ref[0, :, cols] = jnp.concatenate([acc[h] / l[h] for h in range(2)], axis=0).astype(BF16)
    return 0


def _fox_attention(q_t, qaug_t, k, kaug, v_t, tq):
    b, _, s = q_t.shape
    assert s // KEY_TILE <= LANES
    q_spec = pl.BlockSpec((1, LANES, s), lambda i, p: (i, p, 0))
    k_spec = pl.BlockSpec((1, s, LANES), lambda i, p: (i, 0, p))
    group = 2 if (s // tq) % 2 == 0 else 1
    return pl.pallas_call(
        functools.partial(_fox_kernel, tq=tq, group=group),
        grid=(b, N_PAIRS),
        in_specs=[q_spec, q_spec, k_spec, k_spec,
                  pl.BlockSpec((1, s // KEY_TILE, LANES, KEY_TILE), lambda i, p: (i, 0, p, 0))],
        out_specs=q_spec,
        out_shape=jax.ShapeDtypeStruct((b, D_GRP, s), BF16),
        scratch_shapes=_attn_scratch(tq, 2 * group) + [pltpu.VMEM((LANES, LANES), F32)],
        compiler_params=pltpu.CompilerParams(
            dimension_semantics=("parallel", "parallel"), vmem_limit_bytes=VMEM_LIMIT),
        name="fox_attention",
    )(q_t, qaug_t, k, kaug, v_t)


def _rms_rows(x):
    return x * lax.rsqrt(jnp.mean(x * x, axis=0, keepdims=True) + EPS)


def _out_kernel(osb_ref, ofx_ref, gate_ref, x_ref, wout_ref, ada_ref, gf_ref, o_ref, *, d_model, final):
    y = jnp.concatenate([_rms_rows(osb_ref[0].astype(F32)), _rms_rows(ofx_ref[0].astype(F32))], axis=0)
    g = gate_ref[0].astype(F32)
    y = (y * (g / (1.0 + jnp.exp(-g)))).astype(BF16)
    gate = ada_ref[0][:, 2 * d_model:]
    out = x_ref[0] + (1.0 + gate) * _dot_tn(y, wout_ref[...])
    if final:
        out = out * lax.rsqrt(jnp.mean(out * out, axis=-1, keepdims=True) + EPS) * gf_ref[...]
    o_ref[0] = out


def _output(o_sb, o_fx, gate_t, x, wout, ada_l, g_final, ts, final):
    b, s, d = x.shape
    const = lambda shape: pl.BlockSpec(shape, lambda i, j: (0,) * len(shape))
    return pl.pallas_call(
        functools.partial(_out_kernel, d_model=d, final=final),
        grid=(b, s // ts),
        in_specs=[
            pl.BlockSpec((1, D_GRP, ts), lambda i, j: (i, 0, j)),
            pl.BlockSpec((1, D_GRP, ts), lambda i, j: (i, 0, j)),
            pl.BlockSpec((1, 2 * D_GRP, ts), lambda i, j: (i, 0, j)),
            pl.BlockSpec((1, ts, d), lambda i, j: (i, j, 0)),
            const(wout.shape),
            pl.BlockSpec((1, 1, 3 * d), lambda i, j: (i, 0, 0)),
            const((1, d)),
        ],
        out_specs=pl.BlockSpec((1, ts, d), lambda i, j: (i, j, 0)),
        out_shape=jax.ShapeDtypeStruct((b, s, d), F32),
        compiler_params=pltpu.CompilerParams(
            dimension_semantics=("parallel", "parallel"), vmem_limit_bytes=VMEM_LIMIT),
        name="output",
    )(o_sb, o_fx, gate_t, x, wout, ada_l.reshape(b, 1, 3 * d), g_final.reshape(1, d))


def kernel(x, c, w_ada, b_ada, g_norm, w_in, b_f, g_grp, w_out, g_final):
    b, s, d = x.shape
    depth = w_ada.shape[0]
    ts = min(512, s)
    tq = min(256, s)
    consts = _aug_constants()

    grp = lambda i: w_in[:, :, i * D_GRP:(i + 1) * D_GRP]
    t = lambda w: jnp.swapaxes(w, 1, 2)
    wq = jnp.stack([t(grp(0)) * (0.5 * Q_SCALE), t(grp(3)) * (LOG2E * Q_SCALE)], axis=1).astype(BF16)
    wk = jnp.stack([grp(1), grp(4)], axis=1).astype(BF16)
    wv = jnp.stack([t(grp(2)), t(grp(5))], axis=1).astype(BF16)
    wg = t(w_in[:, :, 6 * D_GRP:8 * D_GRP]).astype(BF16)
    wf = jnp.zeros((depth, d, LANES), BF16).at[:, :, :N_HEADS].set(w_in[:, :, 8 * D_GRP:].astype(BF16))
    bf = jnp.zeros((depth, 1, LANES), F32).at[:, 0, :N_HEADS].set(b_f)
    wout = (g_grp[:, :, None] * w_out).astype(BF16)

    ada = _adaln(c, w_ada, b_ada)
    for l in range(depth):
        q_sb, k_sb, v_sb, q_fx, k_fx, v_fx, gate_t, qaug, kaug = _project(
            x, ada[l], g_norm[l], wq[l], wk[l], wv[l], wg[l], wf[l], bf[l], consts, ts)
        o_sb = _sb_attention(q_sb, k_sb, v_sb, tq)
        o_fx = _fox_attention(q_fx, qaug, k_fx, kaug, v_fx, tq)
        x = _output(o_sb, o_fx, gate_t, x, wout[l], ada[l], g_final, ts, l == depth - 1)
    return x
```

```python
import functools

import numpy as np
import jax
import jax.numpy as jnp
from jax import lax
from jax.experimental import pallas as pl
from jax.experimental.pallas import tpu as pltpu

F32 = jnp.float32
BF16 = jnp.bfloat16

HEAD_DIM = 64
N_HEADS = 8
D_GRP = N_HEADS * HEAD_DIM
LANES = 128
SUBLANES = 8
N_PAIRS = D_GRP // LANES
KEY_TILE = 256
KEY_RUN = KEY_TILE // SUBLANES
AUG_STRIDE = 16
N_SPLIT = 3
EPS = 1e-6
Q_SCALE = HEAD_DIM ** -0.5
LOG2E = 1.4426950408889634
VMEM_LIMIT = 52 * 1024 * 1024
UNDERFLOW_BITS = 160.0


def _split_bf16(x, n):
    parts = []
    r = x
    for i in range(n):
        p = r.astype(BF16)
        parts.append(p)
        if i + 1 < n:
            r = r - p.astype(F32)
    return parts


def _dot(a, b):
    return jnp.dot(a, b, preferred_element_type=F32)


def _dot_nt(a, b):
    return lax.dot_general(a, b, (((1,), (1,)), ((), ())), preferred_element_type=F32)


def _dot_tn(a, b):
    return lax.dot_general(a, b, (((0,), (0,)), ((), ())), preferred_element_type=F32)


def _softplus(z):
    return jnp.maximum(z, 0.0) + jnp.log1p(jnp.exp(-jnp.abs(z)))


def _key_of_row(p):
    return (p & (SUBLANES - 1)) * KEY_RUN + (p >> 3)


def _ada_kernel(c_ref, w_ref, b_ref, o_ref):
    c = c_ref[...]
    c_act = c / (1.0 + jnp.exp(-c))
    w_parts = _split_bf16(w_ref[0], 2)
    acc = jnp.zeros(o_ref.shape[1:], F32)
    for cp in _split_bf16(c_act, N_SPLIT):
        for wp in w_parts:
            acc = acc + _dot(cp, wp)
    o_ref[0] = acc + b_ref[0]


def _adaln(c, w_ada, b_ada):
    depth, d, d3 = w_ada.shape
    b = c.shape[0]
    rows = SUBLANES
    tn = 1024
    c_pad = jnp.zeros((rows, d), F32).at[:b].set(c)
    out = pl.pallas_call(
        _ada_kernel,
        grid=(depth, d3 // tn),
        in_specs=[
            pl.BlockSpec((rows, d), lambda l, n: (0, 0)),
            pl.BlockSpec((1, d, tn), lambda l, n: (l, 0, n)),
            pl.BlockSpec((1, 1, tn), lambda l, n: (l, 0, n)),
        ],
        out_specs=pl.BlockSpec((1, rows, tn), lambda l, n: (l, 0, n)),
        out_shape=jax.ShapeDtypeStruct((depth, rows, d3), F32),
        compiler_params=pltpu.CompilerParams(
            dimension_semantics=("parallel", "parallel"), vmem_limit_bytes=VMEM_LIMIT),
        name="adaln",
    )(c_pad, w_ada, b_ada.reshape(depth, 1, d3))
    return out[:, :b]


def _proj_kernel(x_ref, ada_ref, gn_ref, wq_ref, wk_ref, wv_ref, wg_ref, wf_ref, bf_ref,
                 pq_ref, pk_ref,
                 qsb_ref, ksb_ref, vsb_ref, qfx_ref, kfx_ref, vfx_ref, gate_ref, qaug_ref, kaug_ref,
                 carry_ref, *, d_model):
    ts = x_ref.shape[1]
    x = x_ref[0]
    ada = ada_ref[0]
    shift = ada[:, :d_model]
    scale = ada[:, d_model:2 * d_model]
    r = lax.rsqrt(jnp.mean(x * x, axis=-1, keepdims=True) + EPS)
    h = ((x * r) * gn_ref[...] * (1.0 + scale) + shift).astype(BF16)

    row = lax.broadcasted_iota(jnp.int32, (ts, ts), 0)
    col = lax.broadcasted_iota(jnp.int32, (ts, ts), 1)
    in_tile = row & (KEY_TILE - 1)
    key_row = (row - in_tile) + _key_of_row(in_tile)
    tile_row = lax.broadcasted_iota(jnp.int32, (KEY_TILE, KEY_TILE), 0)
    tile_col = lax.broadcasted_iota(jnp.int32, (KEY_TILE, KEY_TILE), 1)
    perm = jnp.where(tile_col == _key_of_row(tile_row), 1.0, 0.0).astype(BF16)
    h_keys = jnp.concatenate([_dot(perm, h[t * KEY_TILE:(t + 1) * KEY_TILE])
                              for t in range(ts // KEY_TILE)], axis=0).astype(BF16)

    qsb_ref[0] = _dot_nt(wq_ref[0], h).astype(BF16)
    qfx_ref[0] = _dot_nt(wq_ref[1], h).astype(BF16)
    gate_ref[0] = _dot_nt(wg_ref[...], h).astype(BF16)
    ksb_ref[0] = _dot(h_keys, wk_ref[0]).astype(BF16)
    kfx_ref[0] = _dot(h_keys, wk_ref[1]).astype(BF16)
    for o_ref, w in ((vsb_ref, wv_ref[0]), (vfx_ref, wv_ref[1])):
        v_t = _dot_nt(w, h_keys).astype(BF16)
        for t in range(ts // KEY_TILE):
            o_ref[0, t] = v_t[:, t * KEY_TILE:(t + 1) * KEY_TILE]

    zf = _dot(h, wf_ref[...]) + bf_ref[...]
    log_f_parts = _split_bf16(-_softplus(-zf), N_SPLIT)

    @pl.when(pl.program_id(1) == 0)
    def _():
        carry_ref[...] = jnp.zeros_like(carry_ref)

    carry = carry_ref[...]
    upto = jnp.where(col <= row, 1.0, 0.0).astype(BF16)
    upto_keys = jnp.where(col <= key_row, 1.0, 0.0).astype(BF16)
    sums = _dot(jnp.concatenate([upto, upto_keys], axis=0), jnp.concatenate(log_f_parts, axis=1))
    fold = lambda x: x[:, :LANES] + x[:, LANES:2 * LANES] + x[:, 2 * LANES:]
    cum = carry + fold(sums[:ts])
    cum_keys = carry + fold(sums[ts:])
    carry_ref[...] = cum[ts - 1:ts, :]

    ones = jnp.ones((ts, LANES), BF16)
    pieces = jnp.concatenate(_split_bf16(cum * LOG2E, N_SPLIT) + [ones], axis=1)
    pieces_keys = jnp.concatenate(_split_bf16(cum_keys * LOG2E, N_SPLIT) + [ones], axis=1)
    qaug_ref[0] = _dot_nt(pq_ref[...], pieces).astype(BF16)
    kaug_ref[0] = _dot(pieces_keys, pk_ref[...]).astype(BF16)


def _aug_constants():
    pq = np.zeros((D_GRP, (N_SPLIT + 1) * LANES), np.float32)
    pk = np.zeros(((N_SPLIT + 1) * LANES, D_GRP), np.float32)
    one_col = N_SPLIT * LANES
    for h in range(N_HEADS):
        base = (h // 2) * LANES + AUG_STRIDE * (h % 2)
        for j in range(N_SPLIT):
            pq[base + j, j * LANES + h] = 1.0
            pk[one_col, base + j] = 1.0
            pk[j * LANES + h, base + N_SPLIT + j] = -1.0
            pq[base + N_SPLIT + j, one_col] = 1.0
    return jnp.asarray(pq, BF16), jnp.asarray(pk, BF16)


def _project(x, ada_l, g_norm_l, wq, wk, wv, wg, wf, bf, consts, ts):
    b, s, d = x.shape
    pq, pk = consts
    const = lambda shape: pl.BlockSpec(shape, lambda i, j: (0,) * len(shape))
    row_major = pl.BlockSpec((1, ts, D_GRP), lambda i, j: (i, j, 0))
    feat_major = pl.BlockSpec((1, D_GRP, ts), lambda i, j: (i, 0, j))
    v_spec = pl.BlockSpec((1, ts // KEY_TILE, D_GRP, KEY_TILE), lambda i, j: (i, j, 0, 0))
    row_shape = jax.ShapeDtypeStruct((b, s, D_GRP), BF16)
    feat_shape = jax.ShapeDtypeStruct((b, D_GRP, s), BF16)
    v_shape = jax.ShapeDtypeStruct((b, s // KEY_TILE, D_GRP, KEY_TILE), BF16)
    return pl.pallas_call(
        functools.partial(_proj_kernel, d_model=d),
        grid=(b, s // ts),
        in_specs=[
            pl.BlockSpec((1, ts, d), lambda i, j: (i, j, 0)),
            pl.BlockSpec((1, 1, 3 * d), lambda i, j: (i, 0, 0)),
            const((1, d)),
            const(wq.shape), const(wk.shape), const(wv.shape), const(wg.shape), const(wf.shape),
            const(bf.shape), const(pq.shape), const(pk.shape),
        ],
        out_specs=[feat_major, row_major, v_spec, feat_major, row_major, v_spec,
                   pl.BlockSpec((1, 2 * D_GRP, ts), lambda i, j: (i, 0, j)), feat_major, row_major],
        out_shape=[feat_shape, row_shape, v_shape, feat_shape, row_shape, v_shape,
                   jax.ShapeDtypeStruct((b, 2 * D_GRP, s), BF16), feat_shape, row_shape],
        scratch_shapes=[pltpu.VMEM((1, LANES), F32)],
        compiler_params=pltpu.CompilerParams(
            dimension_semantics=("parallel", "arbitrary"), vmem_limit_bytes=VMEM_LIMIT),
        name="project",
    )(x, ada_l.reshape(b, 1, 3 * d), g_norm_l.reshape(1, d), wq, wk, wv, wg, wf, bf, pq, pk)


def _tile_iotas(tq):
    shape = (KEY_RUN, SUBLANES, tq)
    run = lax.broadcasted_iota(jnp.int32, shape, 0)
    sub = lax.broadcasted_iota(jnp.int32, shape, 1)
    qry = lax.broadcasted_iota(jnp.int32, shape, 2)
    return run, sub, qry


def _head_rows(block, j, width):
    row = lax.broadcasted_iota(jnp.int32, block.shape, 0)
    return jnp.where((row >= j * width) & (row < (j + 1) * width), block, jnp.zeros_like(block))


def _shift_up(x, k, fill):
    sub = lax.broadcasted_iota(jnp.int32, x.shape, 0)
    return jnp.where(sub + k < SUBLANES, pltpu.roll(x, SUBLANES - k, axis=0), fill)


def _pipeline_start(qt, stages, state, mask_first=True, lowest=None):
    stage_scores, stage_weights, _ = stages
    stage_scores(1, qt)
    state = stage_weights(1, 0, state, mask_first)
    stage_scores(0, jnp.maximum(qt - 1, 0))
    return state, (0 if lowest is None else lowest(state))


def _pipeline_loop(qt, stages, state, low, alive=None):
    stage_scores, stage_weights, stage_values = stages

    def half(slot, kt, state):
        stage_scores(1 - slot, jnp.maximum(kt - 1, low))
        state = stage_values(slot, kt + 1, state, None)
        return stage_weights(slot, 1 - slot, state, False)

    n_bodies = (qt - low + 1) // 2
    if alive is None:
        def body(i, state):
            kt = qt - 1 - 2 * i
            return half(1, kt - 1, half(0, kt, state))

        return lax.fori_loop(0, n_bodies, body, state), n_bodies

    def body(loop):
        i, _, state = loop
        kt = qt - 1 - 2 * i
        state = half(0, kt, state)
        go = alive(state, kt - 1)
        return i + 1, go, half(1, kt - 1, state)

    done, _, state = lax.while_loop(lambda loop: (loop[0] < n_bodies) & loop[1], body,
                                    (jnp.int32(0), alive(state, qt), state))
    return state, done


def _pipeline_finish(qt, stages, state, low, done):
    last = qt - 2 * done
    return stages[2](0, jnp.maximum(last, low), state, last >= low)


def _tile_pipeline(qt, stages, state, alive=None, mask_first=True, lowest=None):
    state, low = _pipeline_start(qt, stages, state, mask_first, lowest)
    state, done = _pipeline_loop(qt, stages, state, low, alive)
    return _pipeline_finish(qt, stages, state, low, done)


def _attn_scratch(tq, slots=2):
    return [pltpu.VMEM((slots, 2, KEY_TILE, tq), F32), pltpu.VMEM((slots, 2, KEY_TILE, tq), BF16)]


def _sb_kernel(q_ref, k_ref, v_ref, o_ref, z_ref, a_ref, *, tq, group):
    run, sub, qry = _tile_iotas(tq)
    causal = sub * KEY_RUN + run < qry
    refs = (q_ref, k_ref, v_ref, o_ref, z_ref, a_ref)
    lax.fori_loop(0, q_ref.shape[2] // (tq * group),
                  lambda j, _: _sb_query_tiles([j * group + i for i in range(group)], causal, refs, tq), 0)


def _sb_query_tiles(qts, causal, refs, tq):
    q_ref, k_ref, v_ref, o_ref, z_ref, a_ref = refs

    def weights(half_z, carry, masked):
        th = jnp.tanh(half_z.reshape(KEY_RUN, SUBLANES, tq))
        rem = 0.5 - 0.5 * th
        if masked:
            rem = jnp.where(causal, rem, 1.0)
        prod = jnp.ones((SUBLANES, tq), F32)
        parts = [None] * KEY_RUN
        for r in reversed(range(KEY_RUN)):
            below = prod * rem[r]
            parts[r] = prod - below
            prod = below
        incl = prod
        for step in (1, 2, 4):
            incl = incl * _shift_up(incl, step, 1.0)
        scale = _shift_up(incl, 1, 1.0) * carry
        a = (jnp.stack(parts) * scale[None]).reshape(KEY_TILE, tq).astype(BF16)
        return a, carry * jnp.broadcast_to(incl[0:1], carry.shape)

    def stages(i, qt):
        base = 2 * i
        cols = pl.ds(pl.multiple_of(qt * tq, tq), tq)
        q_heads = [_head_rows(q_ref[0, :, cols], h, HEAD_DIM) for h in range(2)]

        def stage_scores(slot, kt):
            start = pl.multiple_of(kt * KEY_TILE, KEY_TILE)
            k = k_ref[0, pl.ds(start, KEY_TILE), :]
            for h in range(2):
                z_ref[base + slot, h] = _dot(k, q_heads[h])

        def stage_weights(src, dst, state, masked):
            carry, acc = state
            new_carry = []
            for h in range(2):
                a_ref[base + dst, h], c = weights(z_ref[base + src, h], carry[h], masked)
                new_carry.append(c)
            return tuple(new_carry), acc

        def stage_values(slot, kt, state, valid):
            carry, acc = state
            new_acc = []
            for h in range(2):
                av = _dot(v_ref[0, kt, h * HEAD_DIM:(h + 1) * HEAD_DIM, :], a_ref[base + slot, h])
                new_acc.append(acc[h] + (av if valid is None else jnp.where(valid, av, 0.0)))
            return carry, tuple(new_acc)

        return cols, stage_scores, stage_weights, stage_values

    alive = lambda st, kt: jnp.max(jnp.maximum(st[0][0], st[0][1])) > 0.0
    tiles = [(qt,) + stages(i, qt) for i, qt in enumerate(qts)]
    init = ((jnp.ones((SUBLANES, tq), F32),) * 2, (jnp.zeros((HEAD_DIM, tq), F32),) * 2)
    states = [init] * len(tiles)

    for qt, _, stage_scores, _, _ in tiles:
        stage_scores(1, qt)
        stage_scores(0, jnp.maximum(qt - 1, 0))
    for i, (qt, _, _, stage_weights, _) in enumerate(tiles):
        states[i] = stage_weights(0, 0, stage_weights(1, 1, states[i], True), False)
    for i, (qt, _, _, _, stage_values) in enumerate(tiles):
        states[i] = stage_values(0, jnp.maximum(qt - 1, 0), stage_values(1, qt, states[i], None), qt >= 1)

    for i, (qt, cols, stage_scores, stage_weights, stage_values) in enumerate(tiles):
        def rest(state, qt=qt, fns=(stage_scores, stage_weights, stage_values)):
            return _tile_pipeline(qt - 2, fns, state, alive, mask_first=False)

        _, acc = lax.cond((qt >= 2) & alive(states[i], qt - 1), rest, lambda st: st, states[i])
        o_ref[0, :, cols] = jnp.concatenate(acc, axis=0).astype(BF16)
    return 0


def _sb_attention(q_t, k, v_t, tq):
    b, _, s = q_t.shape
    q_spec = pl.BlockSpec((1, LANES, s), lambda i, p: (i, p, 0))
    group = 4 if (s // tq) % 4 == 0 else 1
    return pl.pallas_call(
        functools.partial(_sb_kernel, tq=tq, group=group),
        grid=(b, N_PAIRS),
        in_specs=[
            q_spec,
            pl.BlockSpec((1, s, LANES), lambda i, p: (i, 0, p)),
            pl.BlockSpec((1, s // KEY_TILE, LANES, KEY_TILE), lambda i, p: (i, 0, p, 0)),
        ],
        out_specs=q_spec,
        out_shape=jax.ShapeDtypeStruct((b, D_GRP, s), BF16),
        scratch_shapes=_attn_scratch(tq, 2 * group),
        compiler_params=pltpu.CompilerParams(
            dimension_semantics=("parallel", "parallel"), vmem_limit_bytes=VMEM_LIMIT),
        name="sb_attention",
    )(q_t, k, v_t)


def _fox_kernel(q_ref, qa_ref, k_ref, ka_ref, v_ref, o_ref, z_ref, p_ref, tails_ref, *, tq, group):
    run, sub, qry = _tile_iotas(tq)
    causal = (sub * KEY_RUN + run <= qry).reshape(KEY_TILE, tq)

    n_tiles = k_ref.shape[1] // KEY_TILE
    tails_ref[...] = jnp.zeros(tails_ref.shape, F32)

    def scan_keys(t, k_abs):
        start = pl.multiple_of(t * KEY_TILE, KEY_TILE)
        k = k_ref[0, pl.ds(start, KEY_TILE), :]
        tail = ka_ref[0, pl.ds(start + KEY_TILE - 2 * SUBLANES, 2 * SUBLANES), :].astype(F32)
        tails_ref[pl.ds(t, 1), :] = tail[2 * SUBLANES - 1:, :]
        return jnp.maximum(k_abs, jnp.max(jnp.abs(k.astype(F32)), axis=0, keepdims=True))

    k_abs = lax.fori_loop(0, n_tiles, scan_keys, jnp.zeros((1, LANES), F32))
    feat = lax.broadcasted_iota(jnp.int32, (SUBLANES, LANES), 1)
    head = lax.broadcasted_iota(jnp.int32, (SUBLANES, LANES), 0)
    first = head * AUG_STRIDE + N_SPLIT
    pick = jnp.where((feat >= first) & (feat < first + N_SPLIT) & (head < 2), 1.0, 0.0).astype(BF16)
    neg_cum_rows = _dot_nt(pick, tails_ref[...].astype(BF16))
    neg_cum = [neg_cum_rows[h:h + 1, :] for h in range(2)]
    k_bound = [HEAD_DIM ** 0.5 * jnp.max(k_abs[:, h * HEAD_DIM:(h + 1) * HEAD_DIM], axis=1, keepdims=True)
               for h in range(2)]
    refs = (q_ref, qa_ref, k_ref, ka_ref, v_ref, o_ref, z_ref, p_ref)
    lax.fori_loop(0, q_ref.shape[2] // (tq * group),
                  lambda j, _: _fox_query_tiles([j * group + i for i in range(group)], causal, k_bound,
                                                neg_cum, refs, tq), 0)


def _fox_query_tiles(qts, causal, k_bound, neg_cum, refs, tq):
    q_ref, qa_ref, k_ref, ka_ref, v_ref, o_ref, z_ref, p_ref = refs

    def stages(i, qt):
        base = 2 * i
        cols = pl.ds(pl.multiple_of(qt * tq, tq), tq)
        q_heads = [jnp.concatenate([_head_rows(q_ref[0, :, cols], h, HEAD_DIM),
                                    _head_rows(qa_ref[0, :, cols], h, AUG_STRIDE)], axis=0)
                   for h in range(2)]

        def stage_scores(slot, kt):
            start = pl.multiple_of(kt * KEY_TILE, KEY_TILE)
            k = jnp.concatenate([k_ref[0, pl.ds(start, KEY_TILE), :],
                                 ka_ref[0, pl.ds(start, KEY_TILE), :]], axis=1)
            for h in range(2):
                z_ref[base + slot, h] = _dot(k, q_heads[h])

        def stage_weights(src, dst, state, masked):
            m, l, _, _, acc = state
            m_new, l_new, alpha = [], [], []
            for h in range(2):
                logits = z_ref[base + src, h]
                if masked:
                    logits = jnp.where(causal, logits, -jnp.inf)
                mh = jnp.maximum(m[h], jnp.max(logits, axis=0, keepdims=True))
                p = jnp.exp2(logits - mh)
                ah = jnp.exp2(m[h] - mh)
                p_ref[base + dst, h] = p.astype(BF16)
                m_new.append(mh)
                alpha.append(ah)
                l_new.append(ah * l[h] + jnp.sum(p, axis=0, keepdims=True))
            return tuple(m_new), tuple(l_new), l, tuple(alpha), acc

        def stage_values(slot, kt, state, valid):
            m, l, l_prev, alpha, acc = state
            new_acc = []
            for h in range(2):
                pv = _dot(v_ref[0, kt, h * HEAD_DIM:(h + 1) * HEAD_DIM, :], p_ref[base + slot, h])
                upd = alpha[h] * acc[h] + pv
                new_acc.append(upd if valid is None else jnp.where(valid, upd, acc[h]))
            if valid is not None:
                l = tuple(jnp.where(valid, l[h], l_prev[h]) for h in range(2))
            return m, l, l_prev, alpha, tuple(new_acc)

        q32 = q_ref[0, :, cols].astype(F32)
        qa32 = qa_ref[0, :, cols].astype(F32)
        reach = []
        for h in range(2):
            q_norm = jnp.sqrt(jnp.sum(jnp.square(q32[h * HEAD_DIM:(h + 1) * HEAD_DIM]), axis=0, keepdims=True))
            cum_q = jnp.sum(qa32[h * AUG_STRIDE:h * AUG_STRIDE + N_SPLIT], axis=0, keepdims=True)
            reach.append(q_norm * k_bound[h] + cum_q)

        def lowest(state):
            m = state[0]
            tile = lax.broadcasted_iota(jnp.int32, (1, LANES), 1)
            first = []
            for h in range(2):
                gap = jnp.max(reach[h] - m[h], axis=1, keepdims=True)
                needed = (gap + neg_cum[h] >= -UNDERFLOW_BITS) | (tile >= qt)
                first.append(jnp.min(jnp.where(needed, tile, LANES).astype(F32), axis=1, keepdims=True))
            return jnp.minimum(first[0], first[1])[0, 0].astype(jnp.int32)

        return cols, (stage_scores, stage_weights, stage_values), lowest

    row = lambda v: (jnp.full((1, tq), v, F32),) * 2
    init = (row(-jnp.inf), row(0.0), row(0.0), row(1.0), (jnp.zeros((HEAD_DIM, tq), F32),) * 2)
    tiles = [(qt,) + stages(i, qt) for i, qt in enumerate(qts)]
    started = [_pipeline_start(qt, fns, init, True, lowest) for qt, _, fns, lowest in tiles]
    looped = [_pipeline_loop(qt, fns, state, low) for (qt, _, fns, _), (state, low) in zip(tiles, started)]
    for (qt, cols, fns, _), (_, low), (state, done) in zip(tiles, started, looped):
        _, l, _, _, acc = _pipeline_finish(qt, fns, state, low, done)
        o_ref[0, :, cols] = jnp.concatenate([acc[h] / l[h] for h in range(2)], axis=0).astype(BF16)
    return 0


def _fox_attention(q_t, qaug_t, k, kaug, v_t, tq):
    b, _, s = q_t.shape
    assert s // KEY_TILE <= LANES
    q_spec = pl.BlockSpec((1, LANES, s), lambda i, p: (i, p, 0))
    k_spec = pl.BlockSpec((1, s, LANES), lambda i, p: (i, 0, p))
    group = 4 if (s // tq) % 4 == 0 else 1
    return pl.pallas_call(
        functools.partial(_fox_kernel, tq=tq, group=group),
        grid=(b, N_PAIRS),
        in_specs=[q_spec, q_spec, k_spec, k_spec,
                  pl.BlockSpec((1, s // KEY_TILE, LANES, KEY_TILE), lambda i, p: (i, 0, p, 0))],
        out_specs=q_spec,
        out_shape=jax.ShapeDtypeStruct((b, D_GRP, s), BF16),
        scratch_shapes=_attn_scratch(tq, 2 * group) + [pltpu.VMEM((LANES, LANES), F32)],
        compiler_params=pltpu.CompilerParams(
            dimension_semantics=("parallel", "parallel"), vmem_limit_bytes=VMEM_LIMIT),
        name="fox_attention",
    )(q_t, qaug_t, k, kaug, v_t)


def _rms_rows(x):
    return x * lax.rsqrt(jnp.mean(x * x, axis=0, keepdims=True) + EPS)


def _out_kernel(osb_ref, ofx_ref, gate_ref, x_ref, wout_ref, ada_ref, gf_ref, o_ref, *, d_model, final):
    y = jnp.concatenate([_rms_rows(osb_ref[0].astype(F32)), _rms_rows(ofx_ref[0].astype(F32))], axis=0)
    g = gate_ref[0].astype(F32)
    y = (y * (g / (1.0 + jnp.exp(-g)))).astype(BF16)
    gate = ada_ref[0][:, 2 * d_model:]
    out = x_ref[0] + (1.0 + gate) * _dot_tn(y, wout_ref[...])
    if final:
        out = out * lax.rsqrt(jnp.mean(out * out, axis=-1, keepdims=True) + EPS) * gf_ref[...]
    o_ref[0] = out


def _output(o_sb, o_fx, gate_t, x, wout, ada_l, g_final, ts, final):
    b, s, d = x.shape
    const = lambda shape: pl.BlockSpec(shape, lambda i, j: (0,) * len(shape))
    return pl.pallas_call(
        functools.partial(_out_kernel, d_model=d, final=final),
        grid=(b, s // ts),
        in_specs=[
            pl.BlockSpec((1, D_GRP, ts), lambda i, j: (i, 0, j)),
            pl.BlockSpec((1, D_GRP, ts), lambda i, j: (i, 0, j)),
            pl.BlockSpec((1, 2 * D_GRP, ts), lambda i, j: (i, 0, j)),
            pl.BlockSpec((1, ts, d), lambda i, j: (i, j, 0)),
            const(wout.shape),
            pl.BlockSpec((1, 1, 3 * d), lambda i, j: (i, 0, 0)),
            const((1, d)),
        ],
        out_specs=pl.BlockSpec((1, ts, d), lambda i, j: (i, j, 0)),
        out_shape=jax.ShapeDtypeStruct((b, s, d), F32),
        compiler_params=pltpu.CompilerParams(
            dimension_semantics=("parallel", "parallel"), vmem_limit_bytes=VMEM_LIMIT),
        name="output",
    )(o_sb, o_fx, gate_t, x, wout, ada_l.reshape(b, 1, 3 * d), g_final.reshape(1, d))


def kernel(x, c, w_ada, b_ada, g_norm, w_in, b_f, g_grp, w_out, g_final):
    b, s, d = x.shape
    depth = w_ada.shape[0]
    ts = min(512, s)
    tq = min(256, s)
    consts = _aug_constants()

    grp = lambda i: w_in[:, :, i * D_GRP:(i + 1) * D_GRP]
    t = lambda w: jnp.swapaxes(w, 1, 2)
    wq = jnp.stack([t(grp(0)) * (0.5 * Q_SCALE), t(grp(3)) * (LOG2E * Q_SCALE)], axis=1).astype(BF16)
    wk = jnp.stack([grp(1), grp(4)], axis=1).astype(BF16)
    wv = jnp.stack([t(grp(2)), t(grp(5))], axis=1).astype(BF16)
    wg = t(w_in[:, :, 6 * D_GRP:8 * D_GRP]).astype(BF16)
    wf = jnp.zeros((depth, d, LANES), BF16).at[:, :, :N_HEADS].set(w_in[:, :, 8 * D_GRP:].astype(BF16))
    bf = jnp.zeros((depth, 1, LANES), F32).at[:, 0, :N_HEADS].set(b_f)
    wout = (g_grp[:, :, None] * w_out).astype(BF16)

    ada = _adaln(c, w_ada, b_ada)
    for l in range(depth):
        q_sb, k_sb, v_sb, q_fx, k_fx, v_fx, gate_t, qaug, kaug = _project(
            x, ada[l], g_norm[l], wq[l], wk[l], wv[l], wg[l], wf[l], bf[l], consts, ts)
        o_sb = _sb_attention(q_sb, k_sb, v_sb, tq)
        o_fx = _fox_attention(q_fx, qaug, k_fx, kaug, v_fx, tq)
        x = _output(o_sb, o_fx, gate_t, x, wout[l], ada[l], g_final, ts, l == depth - 1)
    return x
```

```python
import functools

import numpy as np
import jax
import jax.numpy as jnp
from jax import lax
from jax.experimental import pallas as pl
from jax.experimental.pallas import tpu as pltpu

F32 = jnp.float32
BF16 = jnp.bfloat16

HEAD_DIM = 64
N_HEADS = 8
D_GRP = N_HEADS * HEAD_DIM
LANES = 128
SUBLANES = 8
N_PAIRS = D_GRP // LANES
KEY_TILE = 256
KEY_RUN = KEY_TILE // SUBLANES
AUG_STRIDE = 16
N_SPLIT = 3
FORGET_ROWS = 2 * SUBLANES
EPS = 1e-6
Q_SCALE = HEAD_DIM ** -0.5
LOG2E = 1.4426950408889634
VMEM_LIMIT = 52 * 1024 * 1024
UNDERFLOW_BITS = 160.0


def _split_bf16(x, n):
    parts = []
    r = x
    for i in range(n):
        p = r.astype(BF16)
        parts.append(p)
        if i + 1 < n:
            r = r - p.astype(F32)
    return parts


def _dot(a, b):
    return jnp.dot(a, b, preferred_element_type=F32)


def _dot_nt(a, b):
    return lax.dot_general(a, b, (((1,), (1,)), ((), ())), preferred_element_type=F32)


def _dot_tn(a, b):
    return lax.dot_general(a, b, (((0,), (0,)), ((), ())), preferred_element_type=F32)


def _softplus(z):
    return jnp.maximum(z, 0.0) + jnp.log1p(jnp.exp(-jnp.abs(z)))


def _key_of_row(p):
    return (p & (SUBLANES - 1)) * KEY_RUN + (p >> 3)


def _ada_kernel(c_ref, w_ref, b_ref, o_ref):
    c = c_ref[...]
    c_act = c / (1.0 + jnp.exp(-c))
    w_parts = _split_bf16(w_ref[0], 2)
    acc = jnp.zeros(o_ref.shape[1:], F32)
    for cp in _split_bf16(c_act, N_SPLIT):
        for wp in w_parts:
            acc = acc + _dot(cp, wp)
    o_ref[0] = acc + b_ref[0]


def _adaln(c, w_ada, b_ada):
    depth, d, d3 = w_ada.shape
    b = c.shape[0]
    rows = SUBLANES
    tn = 1024
    c_pad = jnp.zeros((rows, d), F32).at[:b].set(c)
    out = pl.pallas_call(
        _ada_kernel,
        grid=(depth, d3 // tn),
        in_specs=[
            pl.BlockSpec((rows, d), lambda l, n: (0, 0)),
            pl.BlockSpec((1, d, tn), lambda l, n: (l, 0, n)),
            pl.BlockSpec((1, 1, tn), lambda l, n: (l, 0, n)),
        ],
        out_specs=pl.BlockSpec((1, rows, tn), lambda l, n: (l, 0, n)),
        out_shape=jax.ShapeDtypeStruct((depth, rows, d3), F32),
        compiler_params=pltpu.CompilerParams(
            dimension_semantics=("parallel", "parallel"), vmem_limit_bytes=VMEM_LIMIT),
        name="adaln",
    )(c_pad, w_ada, b_ada.reshape(depth, 1, d3))
    return out[:, :b]


def _proj_kernel(x_ref, ada_ref, gn_ref, wq_ref, wk_ref, wv_ref, wg_ref, wf_ref, bf_ref,
                 pq_ref, pk_ref,
                 qsb_ref, ksb_ref, vsb_ref, qfx_ref, kfx_ref, vfx_ref, gate_ref, qaug_ref, kaug_ref,
                 carry_ref, *, d_model):
    ts = x_ref.shape[1]
    x = x_ref[0]
    ada = ada_ref[0]
    shift = ada[:, :d_model]
    scale = ada[:, d_model:2 * d_model]
    r = lax.rsqrt(jnp.mean(x * x, axis=-1, keepdims=True) + EPS)
    h = ((x * r) * gn_ref[...] * (1.0 + scale) + shift).astype(BF16)

    row = lax.broadcasted_iota(jnp.int32, (ts, ts), 0)
    col = lax.broadcasted_iota(jnp.int32, (ts, ts), 1)
    tile_row = lax.broadcasted_iota(jnp.int32, (KEY_TILE, KEY_TILE), 0)
    tile_col = lax.broadcasted_iota(jnp.int32, (KEY_TILE, KEY_TILE), 1)
    perm = jnp.where(tile_col == _key_of_row(tile_row), 1.0, 0.0).astype(BF16)
    h_keys = jnp.concatenate([_dot(perm, h[t * KEY_TILE:(t + 1) * KEY_TILE])
                              for t in range(ts // KEY_TILE)], axis=0).astype(BF16)

    qsb_ref[0] = _dot_nt(wq_ref[0], h).astype(BF16)
    qfx_ref[0] = _dot_nt(wq_ref[1], h).astype(BF16)
    gate_ref[0] = _dot_nt(wg_ref[...], h).astype(BF16)
    ksb_ref[0] = _dot(h_keys, wk_ref[0]).astype(BF16)
    kfx_ref[0] = _dot(h_keys, wk_ref[1]).astype(BF16)
    for o_ref, w in ((vsb_ref, wv_ref[0]), (vfx_ref, wv_ref[1])):
        v_t = _dot_nt(w, h_keys).astype(BF16)
        for t in range(ts // KEY_TILE):
            o_ref[0, t] = v_t[:, t * KEY_TILE:(t + 1) * KEY_TILE]

    tile_lanes = lambda a: jnp.concatenate([a] * (ts // LANES), axis=1)
    zf = _dot_nt(wf_ref[...], h) + tile_lanes(bf_ref[...])
    log_f_parts = jnp.concatenate(_split_bf16(-_softplus(-zf), N_SPLIT), axis=0)

    @pl.when(pl.program_id(1) == 0)
    def _():
        carry_ref[...] = jnp.zeros_like(carry_ref)

    carry = tile_lanes(carry_ref[...])
    key_col = (col - (col & (KEY_TILE - 1))) + _key_of_row(col & (KEY_TILE - 1))
    upto = jnp.where(row <= col, 1.0, 0.0).astype(BF16)
    upto_keys = jnp.where(row <= key_col, 1.0, 0.0).astype(BF16)
    sums = _dot(log_f_parts, jnp.concatenate([upto, upto_keys], axis=1))
    fold = lambda x: x[:FORGET_ROWS] + x[FORGET_ROWS:2 * FORGET_ROWS] + x[2 * FORGET_ROWS:]
    cum = carry + fold(sums[:, :ts])
    cum_keys = carry + fold(sums[:, ts:])
    carry_ref[...] = jnp.broadcast_to(cum[:, ts - 1:], carry_ref.shape)

    ones = jnp.ones((FORGET_ROWS, ts), BF16)
    pieces = jnp.concatenate(_split_bf16(cum * LOG2E, N_SPLIT) + [ones], axis=0)
    pieces_keys = jnp.concatenate(_split_bf16(cum_keys * LOG2E, N_SPLIT) + [ones], axis=0)
    qaug_ref[0] = _dot(pq_ref[...], pieces).astype(BF16)
    kaug_ref[0] = _dot_tn(pieces_keys, pk_ref[...]).astype(BF16)


def _aug_constants():
    pq = np.zeros((D_GRP, (N_SPLIT + 1) * FORGET_ROWS), np.float32)
    pk = np.zeros(((N_SPLIT + 1) * FORGET_ROWS, D_GRP), np.float32)
    one_row = N_SPLIT * FORGET_ROWS
    for h in range(N_HEADS):
        base = (h // 2) * LANES + AUG_STRIDE * (h % 2)
        for j in range(N_SPLIT):
            pq[base + j, j * FORGET_ROWS + h] = 1.0
            pk[one_row, base + j] = 1.0
            pk[j * FORGET_ROWS + h, base + N_SPLIT + j] = -1.0
            pq[base + N_SPLIT + j, one_row] = 1.0
    return jnp.asarray(pq, BF16), jnp.asarray(pk, BF16)


def _project(x, ada_l, g_norm_l, wq, wk, wv, wg, wf, bf, consts, ts):
    b, s, d = x.shape
    pq, pk = consts
    const = lambda shape: pl.BlockSpec(shape, lambda i, j: (0,) * len(shape))
    row_major = pl.BlockSpec((1, ts, D_GRP), lambda i, j: (i, j, 0))
    feat_major = pl.BlockSpec((1, D_GRP, ts), lambda i, j: (i, 0, j))
    v_spec = pl.BlockSpec((1, ts // KEY_TILE, D_GRP, KEY_TILE), lambda i, j: (i, j, 0, 0))
    row_shape = jax.ShapeDtypeStruct((b, s, D_GRP), BF16)
    feat_shape = jax.ShapeDtypeStruct((b, D_GRP, s), BF16)
    v_shape = jax.ShapeDtypeStruct((b, s // KEY_TILE, D_GRP, KEY_TILE), BF16)
    return pl.pallas_call(
        functools.partial(_proj_kernel, d_model=d),
        grid=(b, s // ts),
        in_specs=[
            pl.BlockSpec((1, ts, d), lambda i, j: (i, j, 0)),
            pl.BlockSpec((1, 1, 3 * d), lambda i, j: (i, 0, 0)),
            const((1, d)),
            const(wq.shape), const(wk.shape), const(wv.shape), const(wg.shape), const(wf.shape),
            const(bf.shape), const(pq.shape), const(pk.shape),
        ],
        out_specs=[feat_major, row_major, v_spec, feat_major, row_major, v_spec,
                   pl.BlockSpec((1, 2 * D_GRP, ts), lambda i, j: (i, 0, j)), feat_major, row_major],
        out_shape=[feat_shape, row_shape, v_shape, feat_shape, row_shape, v_shape,
                   jax.ShapeDtypeStruct((b, 2 * D_GRP, s), BF16), feat_shape, row_shape],
        scratch_shapes=[pltpu.VMEM((FORGET_ROWS, LANES), F32)],
        compiler_params=pltpu.CompilerParams(
            dimension_semantics=("parallel", "arbitrary"), vmem_limit_bytes=VMEM_LIMIT),
        name="project",
    )(x, ada_l.reshape(b, 1, 3 * d), g_norm_l.reshape(1, d), wq, wk, wv, wg, wf, bf, pq, pk)


def _tile_iotas(tq):
    shape = (KEY_RUN, SUBLANES, tq)
    run = lax.broadcasted_iota(jnp.int32, shape, 0)
    sub = lax.broadcasted_iota(jnp.int32, shape, 1)
    qry = lax.broadcasted_iota(jnp.int32, shape, 2)
    return run, sub, qry


def _head_rows(block, j, width):
    row = lax.broadcasted_iota(jnp.int32, block.shape, 0)
    return jnp.where((row >= j * width) & (row < (j + 1) * width), block, jnp.zeros_like(block))


def _shift_up(x, k, fill):
    sub = lax.broadcasted_iota(jnp.int32, x.shape, 0)
    return jnp.where(sub + k < SUBLANES, pltpu.roll(x, SUBLANES - k, axis=0), fill)


def _pipeline_start(qt, stages, state, mask_first=True, lowest=None):
    stage_scores, stage_weights, _ = stages
    stage_scores(1, qt)
    state = stage_weights(1, 0, state, mask_first)
    stage_scores(0, jnp.maximum(qt - 1, 0))
    return state, (0 if lowest is None else lowest(state))


def _pipeline_loop(qt, stages, state, low, alive=None):
    stage_scores, stage_weights, stage_values = stages

    def half(slot, kt, state):
        stage_scores(1 - slot, jnp.maximum(kt - 1, low))
        state = stage_values(slot, kt + 1, state, None)
        return stage_weights(slot, 1 - slot, state, False)

    n_bodies = (qt - low + 1) // 2
    if alive is None:
        def body(i, state):
            kt = qt - 1 - 2 * i
            return half(1, kt - 1, half(0, kt, state))

        return lax.fori_loop(0, n_bodies, body, state), n_bodies

    def body(loop):
        i, _, state = loop
        kt = qt - 1 - 2 * i
        state = half(0, kt, state)
        go = alive(state, kt - 1)
        return i + 1, go, half(1, kt - 1, state)

    done, _, state = lax.while_loop(lambda loop: (loop[0] < n_bodies) & loop[1], body,
                                    (jnp.int32(0), alive(state, qt), state))
    return state, done


def _pipeline_finish(qt, stages, state, low, done):
    last = qt - 2 * done
    return stages[2](0, jnp.maximum(last, low), state, last >= low)


def _tile_pipeline(qt, stages, state, alive=None, mask_first=True, lowest=None):
    state, low = _pipeline_start(qt, stages, state, mask_first, lowest)
    state, done = _pipeline_loop(qt, stages, state, low, alive)
    return _pipeline_finish(qt, stages, state, low, done)


def _attn_scratch(tq, slots=2):
    return [pltpu.VMEM((slots, 2, KEY_TILE, tq), F32), pltpu.VMEM((slots, 2, KEY_TILE, tq), BF16)]


def _sb_kernel(q_ref, k_ref, v_ref, o_ref, z_ref, a_ref, *, tq, group):
    run, sub, qry = _tile_iotas(tq)
    causal = sub * KEY_RUN + run < qry
    refs = (q_ref, k_ref, v_ref, o_ref, z_ref, a_ref)
    lax.fori_loop(0, q_ref.shape[2] // (tq * group),
                  lambda j, _: _sb_query_tiles([j * group + i for i in range(group)], causal, refs, tq), 0)


def _sb_query_tiles(qts, causal, refs, tq):
    q_ref, k_ref, v_ref, o_ref, z_ref, a_ref = refs

    def weights(half_z, carry, masked):
        th = jnp.tanh(half_z.reshape(KEY_RUN, SUBLANES, tq))
        rem = 0.5 - 0.5 * th
        if masked:
            rem = jnp.where(causal, rem, 1.0)
        prod = jnp.ones((SUBLANES, tq), F32)
        parts = [None] * KEY_RUN
        for r in reversed(range(KEY_RUN)):
            below = prod * rem[r]
            parts[r] = prod - below
            prod = below
        incl = prod
        for step in (1, 2, 4):
            incl = incl * _shift_up(incl, step, 1.0)
        scale = _shift_up(incl, 1, 1.0) * carry
        a = (jnp.stack(parts) * scale[None]).reshape(KEY_TILE, tq).astype(BF16)
        return a, carry * jnp.broadcast_to(incl[0:1], carry.shape)

    def stages(i, qt):
        base = 2 * i
        cols = pl.ds(pl.multiple_of(qt * tq, tq), tq)
        q_heads = [_head_rows(q_ref[0, :, cols], h, HEAD_DIM) for h in range(2)]

        def stage_scores(slot, kt):
            start = pl.multiple_of(kt * KEY_TILE, KEY_TILE)
            k = k_ref[0, pl.ds(start, KEY_TILE), :]
            for h in range(2):
                z_ref[base + slot, h] = _dot(k, q_heads[h])

        def stage_weights(src, dst, state, masked):
            carry, acc = state
            new_carry = []
            for h in range(2):
                a_ref[base + dst, h], c = weights(z_ref[base + src, h], carry[h], masked)
                new_carry.append(c)
            return tuple(new_carry), acc

        def stage_values(slot, kt, state, valid):
            carry, acc = state
            new_acc = []
            for h in range(2):
                av = _dot(v_ref[0, kt, h * HEAD_DIM:(h + 1) * HEAD_DIM, :], a_ref[base + slot, h])
                new_acc.append(acc[h] + (av if valid is None else jnp.where(valid, av, 0.0)))
            return carry, tuple(new_acc)

        return cols, stage_scores, stage_weights, stage_values

    alive = lambda st, kt: jnp.max(jnp.maximum(st[0][0], st[0][1])) > 0.0
    tiles = [(qt,) + stages(i, qt) for i, qt in enumerate(qts)]
    init = ((jnp.ones((SUBLANES, tq), F32),) * 2, (jnp.zeros((HEAD_DIM, tq), F32),) * 2)
    states = [init] * len(tiles)

    for qt, _, stage_scores, _, _ in tiles:
        stage_scores(1, qt)
        stage_scores(0, jnp.maximum(qt - 1, 0))
    for i, (qt, _, _, stage_weights, _) in enumerate(tiles):
        states[i] = stage_weights(0, 0, stage_weights(1, 1, states[i], True), False)
    for i, (qt, _, _, _, stage_values) in enumerate(tiles):
        states[i] = stage_values(0, jnp.maximum(qt - 1, 0), stage_values(1, qt, states[i], None), qt >= 1)

    for i, (qt, cols, stage_scores, stage_weights, stage_values) in enumerate(tiles):
        def rest(state, qt=qt, fns=(stage_scores, stage_weights, stage_values)):
            return _tile_pipeline(qt - 2, fns, state, alive, mask_first=False)

        _, acc = lax.cond((qt >= 2) & alive(states[i], qt - 1), rest, lambda st: st, states[i])
        o_ref[0, :, cols] = jnp.concatenate(acc, axis=0).astype(BF16)
    return 0


def _sb_attention(q_t, k, v_t, tq):
    b, _, s = q_t.shape
    q_spec = pl.BlockSpec((1, LANES, s), lambda i, p: (i, p, 0))
    group = 4 if (s // tq) % 4 == 0 else 1
    return pl.pallas_call(
        functools.partial(_sb_kernel, tq=tq, group=group),
        grid=(b, N_PAIRS),
        in_specs=[
            q_spec,
            pl.BlockSpec((1, s, LANES), lambda i, p: (i, 0, p)),
            pl.BlockSpec((1, s // KEY_TILE, LANES, KEY_TILE), lambda i, p: (i, 0, p, 0)),
        ],
        out_specs=q_spec,
        out_shape=jax.ShapeDtypeStruct((b, D_GRP, s), BF16),
        scratch_shapes=_attn_scratch(tq, 2 * group),
        compiler_params=pltpu.CompilerParams(
            dimension_semantics=("parallel", "parallel"), vmem_limit_bytes=VMEM_LIMIT),
        name="sb_attention",
    )(q_t, k, v_t)


def _fox_kernel(q_ref, qa_ref, k_ref, ka_ref, v_ref, o_ref, z_ref, p_ref, tails_ref, *, tq, group):
    run, sub, qry = _tile_iotas(tq)
    causal = (sub * KEY_RUN + run <= qry).reshape(KEY_TILE, tq)

    n_tiles = k_ref.shape[1] // KEY_TILE
    tails_ref[...] = jnp.zeros(tails_ref.shape, F32)

    def scan_keys(t, k_abs):
        start = pl.multiple_of(t * KEY_TILE, KEY_TILE)
        k = k_ref[0, pl.ds(start, KEY_TILE), :]
        tail = ka_ref[0, pl.ds(start + KEY_TILE - 2 * SUBLANES, 2 * SUBLANES), :].astype(F32)
        tails_ref[pl.ds(t, 1), :] = tail[2 * SUBLANES - 1:, :]
        return jnp.maximum(k_abs, jnp.max(jnp.abs(k.astype(F32)), axis=0, keepdims=True))

    k_abs = lax.fori_loop(0, n_tiles, scan_keys, jnp.zeros((1, LANES), F32))
    feat = lax.broadcasted_iota(jnp.int32, (SUBLANES, LANES), 1)
    head = lax.broadcasted_iota(jnp.int32, (SUBLANES, LANES), 0)
    first = head * AUG_STRIDE + N_SPLIT
    pick = jnp.where((feat >= first) & (feat < first + N_SPLIT) & (head < 2), 1.0, 0.0).astype(BF16)
    neg_cum_rows = _dot_nt(pick, tails_ref[...].astype(BF16))
    neg_cum = [neg_cum_rows[h:h + 1, :] for h in range(2)]
    k_bound = [HEAD_DIM ** 0.5 * jnp.max(k_abs[:, h * HEAD_DIM:(h + 1) * HEAD_DIM], axis=1, keepdims=True)
               for h in range(2)]
    refs = (q_ref, qa_ref, k_ref, ka_ref, v_ref, o_ref, z_ref, p_ref)
    lax.fori_loop(0, q_ref.shape[2] // (tq * group),
                  lambda j, _: _fox_query_tiles([j * group + i for i in range(group)], causal, k_bound,
                                                neg_cum, refs, tq), 0)


def _fox_query_tiles(qts, causal, k_bound, neg_cum, refs, tq):
    q_ref, qa_ref, k_ref, ka_ref, v_ref, o_ref, z_ref, p_ref = refs

    def stages(i, qt):
        base = 2 * i
        cols = pl.ds(pl.multiple_of(qt * tq, tq), tq)
        q_heads = [jnp.concatenate([_head_rows(q_ref[0, :, cols], h, HEAD_DIM),
                                    _head_rows(qa_ref[0, :, cols], h, AUG_STRIDE)], axis=0)
                   for h in range(2)]

        def stage_scores(slot, kt):
            start = pl.multiple_of(kt * KEY_TILE, KEY_TILE)
            k = jnp.concatenate([k_ref[0, pl.ds(start, KEY_TILE), :],
                                 ka_ref[0, pl.ds(start, KEY_TILE), :]], axis=1)
            for h in range(2):
                z_ref[base + slot, h] = _dot(k, q_heads[h])

        def stage_weights(src, dst, state, masked):
            m, l, _, _, acc = state
            m_new, l_new, alpha = [], [], []
            for h in range(2):
                logits = z_ref[base + src, h]
                if masked:
                    logits = jnp.where(causal, logits, -jnp.inf)
                mh = jnp.maximum(m[h], jnp.max(logits, axis=0, keepdims=True))
                p = jnp.exp2(logits - mh)
                ah = jnp.exp2(m[h] - mh)
                p_ref[base + dst, h] = p.astype(BF16)
                m_new.append(mh)
                alpha.append(ah)
                l_new.append(ah * l[h] + jnp.sum(p, axis=0, keepdims=True))
            return tuple(m_new), tuple(l_new), l, tuple(alpha), acc

        def stage_values(slot, kt, state, valid):
            m, l, l_prev, alpha, acc = state
            new_acc = []
            for h in range(2):
                pv = _dot(v_ref[0, kt, h * HEAD_DIM:(h + 1) * HEAD_DIM, :], p_ref[base + slot, h])
                upd = alpha[h] * acc[h] + pv
                new_acc.append(upd if valid is None else jnp.where(valid, upd, acc[h]))
            if valid is not None:
                l = tuple(jnp.where(valid, l[h], l_prev[h]) for h in range(2))
            return m, l, l_prev, alpha, tuple(new_acc)

        q32 = q_ref[0, :, cols].astype(F32)
        qa32 = qa_ref[0, :, cols].astype(F32)
        reach = []
        for h in range(2):
            q_norm = jnp.sqrt(jnp.sum(jnp.square(q32[h * HEAD_DIM:(h + 1) * HEAD_DIM]), axis=0, keepdims=True))
            cum_q = jnp.sum(qa32[h * AUG_STRIDE:h * AUG_STRIDE + N_SPLIT], axis=0, keepdims=True)
            reach.append(q_norm * k_bound[h] + cum_q)

        def lowest(state):
            m = state[0]
            tile = lax.broadcasted_iota(jnp.int32, (1, LANES), 1)
            first = []
            for h in range(2):
                gap = jnp.max(reach[h] - m[h], axis=1, keepdims=True)
                needed = (gap + neg_cum[h] >= -UNDERFLOW_BITS) | (tile >= qt)
                first.append(jnp.min(jnp.where(needed, tile, LANES).astype(F32), axis=1, keepdims=True))
            return jnp.minimum(first[0], first[1])[0, 0].astype(jnp.int32)

        return cols, (stage_scores, stage_weights, stage_values), lowest

    row = lambda v: (jnp.full((1, tq), v, F32),) * 2
    init = (row(-jnp.inf), row(0.0), row(0.0), row(1.0), (jnp.zeros((HEAD_DIM, tq), F32),) * 2)
    tiles = [(qt,) + stages(i, qt) for i, qt in enumerate(qts)]
    started = [_pipeline_start(qt, fns, init, True, lowest) for qt, _, fns, lowest in tiles]
    looped = [_pipeline_loop(qt, fns, state, low) for (qt, _, fns, _), (state, low) in zip(tiles, started)]
    for (qt, cols, fns, _), (_, low), (state, done) in zip(tiles, started, looped):
        _, l, _, _, acc = _pipeline_finish(qt, fns, state, low, done)
        o_ref[0, :, cols] = jnp.concatenate([acc[h] / l[h] for h in range(2)], axis=0).astype(BF16)
    return 0


def _fox_attention(q_t, qaug_t, k, kaug, v_t, tq):
    b, _, s = q_t.shape
    assert s // KEY_TILE <= LANES
    q_spec = pl.BlockSpec((1, LANES, s), lambda i, p: (i, p, 0))
    k_spec = pl.BlockSpec((1, s, LANES), lambda i, p: (i, 0, p))
    group = 4 if (s // tq) % 4 == 0 else 1
    return pl.pallas_call(
        functools.partial(_fox_kernel, tq=tq, group=group),
        grid=(b, N_PAIRS),
        in_specs=[q_spec, q_spec, k_spec, k_spec,
                  pl.BlockSpec((1, s // KEY_TILE, LANES, KEY_TILE), lambda i, p: (i, 0, p, 0))],
        out_specs=q_spec,
        out_shape=jax.ShapeDtypeStruct((b, D_GRP, s), BF16),
        scratch_shapes=_attn_scratch(tq, 2 * group) + [pltpu.VMEM((LANES, LANES), F32)],
        compiler_params=pltpu.CompilerParams(
            dimension_semantics=("parallel", "parallel"), vmem_limit_bytes=VMEM_LIMIT),
        name="fox_attention",
    )(q_t, qaug_t, k, kaug, v_t)


def _rms_rows(x):
    return x * lax.rsqrt(jnp.mean(x * x, axis=0, keepdims=True) + EPS)


def _out_kernel(osb_ref, ofx_ref, gate_ref, x_ref, wout_ref, ada_ref, gf_ref, o_ref, *, d_model, final):
    y = jnp.concatenate([_rms_rows(osb_ref[0].astype(F32)), _rms_rows(ofx_ref[0].astype(F32))], axis=0)
    g = gate_ref[0].astype(F32)
    y = (y * (g / (1.0 + jnp.exp(-g)))).astype(BF16)
    gate = ada_ref[0][:, 2 * d_model:]
    out = x_ref[0] + (1.0 + gate) * _dot_tn(y, wout_ref[...])
    if final:
        out = out * lax.rsqrt(jnp.mean(out * out, axis=-1, keepdims=True) + EPS) * gf_ref[...]
    o_ref[0] = out


def _output(o_sb, o_fx, gate_t, x, wout, ada_l, g_final, ts, final):
    b, s, d = x.shape
    const = lambda shape: pl.BlockSpec(shape, lambda i, j: (0,) * len(shape))
    return pl.pallas_call(
        functools.partial(_out_kernel, d_model=d, final=final),
        grid=(b, s // ts),
        in_specs=[
            pl.BlockSpec((1, D_GRP, ts), lambda i, j: (i, 0, j)),
            pl.BlockSpec((1, D_GRP, ts), lambda i, j: (i, 0, j)),
            pl.BlockSpec((1, 2 * D_GRP, ts), lambda i, j: (i, 0, j)),
            pl.BlockSpec((1, ts, d), lambda i, j: (i, j, 0)),
            const(wout.shape),
            pl.BlockSpec((1, 1, 3 * d), lambda i, j: (i, 0, 0)),
            const((1, d)),
        ],
        out_specs=pl.BlockSpec((1, ts, d), lambda i, j: (i, j, 0)),
        out_shape=jax.ShapeDtypeStruct((b, s, d), F32),
        compiler_params=pltpu.CompilerParams(
            dimension_semantics=("parallel", "parallel"), vmem_limit_bytes=VMEM_LIMIT),
        name="output",
    )(o_sb, o_fx, gate_t, x, wout, ada_l.reshape(b, 1, 3 * d), g_final.reshape(1, d))


def kernel(x, c, w_ada, b_ada, g_norm, w_in, b_f, g_grp, w_out, g_final):
    b, s, d = x.shape
    depth = w_ada.shape[0]
    ts = min(512, s)
    tq = min(256, s)
    consts = _aug_constants()

    grp = lambda i: w_in[:, :, i * D_GRP:(i + 1) * D_GRP]
    t = lambda w: jnp.swapaxes(w, 1, 2)
    wq = jnp.stack([t(grp(0)) * (0.5 * Q_SCALE), t(grp(3)) * (LOG2E * Q_SCALE)], axis=1).astype(BF16)
    wk = jnp.stack([grp(1), grp(4)], axis=1).astype(BF16)
    wv = jnp.stack([t(grp(2)), t(grp(5))], axis=1).astype(BF16)
    wg = t(w_in[:, :, 6 * D_GRP:8 * D_GRP]).astype(BF16)
    wf = jnp.zeros((depth, FORGET_ROWS, d), BF16).at[:, :N_HEADS].set(t(w_in[:, :, 8 * D_GRP:]).astype(BF16))
    bf = jnp.zeros((depth, FORGET_ROWS, LANES), F32).at[:, :N_HEADS].set(b_f[:, :, None])
    wout = (g_grp[:, :, None] * w_out).astype(BF16)

    ada = _adaln(c, w_ada, b_ada)
    for l in range(depth):
        q_sb, k_sb, v_sb, q_fx, k_fx, v_fx, gate_t, qaug, kaug = _project(
            x, ada[l], g_norm[l], wq[l], wk[l], wv[l], wg[l], wf[l], bf[l], consts, ts)
        o_sb = _sb_attention(q_sb, k_sb, v_sb, tq)
        o_fx = _fox_attention(q_fx, qaug, k_fx, kaug, v_fx, tq)
        x = _output(o_sb, o_fx, gate_t, x, wout[l], ada[l], g_final, ts, l == depth - 1)
    return x
```

```python
import functools

import numpy as np
import jax
import jax.numpy as jnp
from jax import lax
from jax.experimental import pallas as pl
from jax.experimental.pallas import tpu as pltpu

F32 = jnp.float32
BF16 = jnp.bfloat16

HEAD_DIM = 64
N_HEADS = 8
D_GRP = N_HEADS * HEAD_DIM
LANES = 128
SUBLANES = 8
N_PAIRS = D_GRP // LANES
KEY_TILE = 256
KEY_RUN = KEY_TILE // SUBLANES
AUG_STRIDE = 16
N_SPLIT = 3
FORGET_ROWS = 2 * SUBLANES
EPS = 1e-6
Q_SCALE = HEAD_DIM ** -0.5
LOG2E = 1.4426950408889634
VMEM_LIMIT = 52 * 1024 * 1024
UNDERFLOW_BITS = 160.0


def _split_bf16(x, n):
    parts = []
    r = x
    for i in range(n):
        p = r.astype(BF16)
        parts.append(p)
        if i + 1 < n:
            r = r - p.astype(F32)
    return parts


def _dot(a, b):
    return jnp.dot(a, b, preferred_element_type=F32)


def _dot_nt(a, b):
    return lax.dot_general(a, b, (((1,), (1,)), ((), ())), preferred_element_type=F32)


def _dot_tn(a, b):
    return lax.dot_general(a, b, (((0,), (0,)), ((), ())), preferred_element_type=F32)


def _softplus(z):
    return jnp.maximum(z, 0.0) + jnp.log1p(jnp.exp(-jnp.abs(z)))


def _key_of_row(p):
    return (p & (SUBLANES - 1)) * KEY_RUN + (p >> 3)


def _ada_kernel(c_ref, w_ref, b_ref, o_ref):
    c = c_ref[...]
    c_act = c / (1.0 + jnp.exp(-c))
    w_parts = _split_bf16(w_ref[0], 2)
    acc = jnp.zeros(o_ref.shape[1:], F32)
    for cp in _split_bf16(c_act, N_SPLIT):
        for wp in w_parts:
            acc = acc + _dot(cp, wp)
    o_ref[0] = acc + b_ref[0]


def _adaln(c, w_ada, b_ada):
    depth, d, d3 = w_ada.shape
    b = c.shape[0]
    rows = SUBLANES
    tn = 1024
    c_pad = jnp.zeros((rows, d), F32).at[:b].set(c)
    out = pl.pallas_call(
        _ada_kernel,
        grid=(depth, d3 // tn),
        in_specs=[
            pl.BlockSpec((rows, d), lambda l, n: (0, 0)),
            pl.BlockSpec((1, d, tn), lambda l, n: (l, 0, n)),
            pl.BlockSpec((1, 1, tn), lambda l, n: (l, 0, n)),
        ],
        out_specs=pl.BlockSpec((1, rows, tn), lambda l, n: (l, 0, n)),
        out_shape=jax.ShapeDtypeStruct((depth, rows, d3), F32),
        compiler_params=pltpu.CompilerParams(
            dimension_semantics=("parallel", "parallel"), vmem_limit_bytes=VMEM_LIMIT),
        name="adaln",
    )(c_pad, w_ada, b_ada.reshape(depth, 1, d3))
    return out[:, :b]


def _proj_kernel(x_ref, ada_ref, gn_ref, wq_ref, wk_ref, wv_ref, wg_ref, wf_ref, bf_ref,
                 pq_ref, pk_ref,
                 qsb_ref, ksb_ref, vsb_ref, qfx_ref, kfx_ref, vfx_ref, gate_ref, qaug_ref, kaug_ref,
                 carry_ref, *, d_model):
    ts = x_ref.shape[1]
    x = x_ref[0]
    ada = ada_ref[0]
    shift = ada[:, :d_model]
    scale = ada[:, d_model:2 * d_model]
    r = lax.rsqrt(jnp.mean(x * x, axis=-1, keepdims=True) + EPS)
    h = ((x * r) * gn_ref[...] * (1.0 + scale) + shift).astype(BF16)

    row = lax.broadcasted_iota(jnp.int32, (ts, ts), 0)
    col = lax.broadcasted_iota(jnp.int32, (ts, ts), 1)
    tile_row = lax.broadcasted_iota(jnp.int32, (KEY_TILE, KEY_TILE), 0)
    tile_col = lax.broadcasted_iota(jnp.int32, (KEY_TILE, KEY_TILE), 1)
    perm = jnp.where(tile_col == _key_of_row(tile_row), 1.0, 0.0).astype(BF16)
    h_keys = jnp.concatenate([_dot(perm, h[t * KEY_TILE:(t + 1) * KEY_TILE])
                              for t in range(ts // KEY_TILE)], axis=0).astype(BF16)

    qsb_ref[0] = _dot_nt(wq_ref[0], h).astype(BF16)
    qfx_ref[0] = _dot_nt(wq_ref[1], h).astype(BF16)
    gate_ref[0] = _dot_nt(wg_ref[...], h).astype(BF16)
    ksb_ref[0] = _dot(h_keys, wk_ref[0]).astype(BF16)
    kfx_ref[0] = _dot(h_keys, wk_ref[1]).astype(BF16)
    for o_ref, w in ((vsb_ref, wv_ref[0]), (vfx_ref, wv_ref[1])):
        v_t = _dot_nt(w, h_keys).astype(BF16)
        for t in range(ts // KEY_TILE):
            o_ref[0, t] = v_t[:, t * KEY_TILE:(t + 1) * KEY_TILE]

    tile_lanes = lambda a: jnp.concatenate([a] * (ts // LANES), axis=1)
    zf = _dot_nt(wf_ref[...], h) + tile_lanes(bf_ref[...])
    log_f_parts = jnp.concatenate(_split_bf16(-_softplus(-zf), N_SPLIT), axis=0)

    @pl.when(pl.program_id(1) == 0)
    def _():
        carry_ref[...] = jnp.zeros_like(carry_ref)

    carry = tile_lanes(carry_ref[...])
    key_col = (col - (col & (KEY_TILE - 1))) + _key_of_row(col & (KEY_TILE - 1))
    upto = jnp.where(row <= col, 1.0, 0.0).astype(BF16)
    upto_keys = jnp.where(row <= key_col, 1.0, 0.0).astype(BF16)
    sums = _dot(log_f_parts, jnp.concatenate([upto, upto_keys], axis=1))
    fold = lambda x: x[:FORGET_ROWS] + x[FORGET_ROWS:2 * FORGET_ROWS] + x[2 * FORGET_ROWS:]
    cum = carry + fold(sums[:, :ts])
    cum_keys = carry + fold(sums[:, ts:])
    carry_ref[...] = jnp.broadcast_to(cum[:, ts - 1:], carry_ref.shape)

    ones = jnp.ones((FORGET_ROWS, ts), BF16)
    pieces = jnp.concatenate(_split_bf16(cum * LOG2E, N_SPLIT) + [ones], axis=0)
    pieces_keys = jnp.concatenate(_split_bf16(cum_keys * LOG2E, N_SPLIT) + [ones], axis=0)
    qaug_ref[0] = _dot(pq_ref[...], pieces).astype(BF16)
    kaug_ref[0] = _dot_tn(pieces_keys, pk_ref[...]).astype(BF16)


def _aug_constants():
    pq = np.zeros((D_GRP, (N_SPLIT + 1) * FORGET_ROWS), np.float32)
    pk = np.zeros(((N_SPLIT + 1) * FORGET_ROWS, D_GRP), np.float32)
    one_row = N_SPLIT * FORGET_ROWS
    for h in range(N_HEADS):
        base = (h // 2) * LANES + AUG_STRIDE * (h % 2)
        for j in range(N_SPLIT):
            pq[base + j, j * FORGET_ROWS + h] = 1.0
            pk[one_row, base + j] = 1.0
            pk[j * FORGET_ROWS + h, base + N_SPLIT + j] = -1.0
            pq[base + N_SPLIT + j, one_row] = 1.0
    return jnp.asarray(pq, BF16), jnp.asarray(pk, BF16)


def _project(x, ada_l, g_norm_l, wq, wk, wv, wg, wf, bf, consts, ts):
    b, s, d = x.shape
    pq, pk = consts
    const = lambda shape: pl.BlockSpec(shape, lambda i, j: (0,) * len(shape))
    row_major = pl.BlockSpec((1, ts, D_GRP), lambda i, j: (i, j, 0))
    feat_major = pl.BlockSpec((1, D_GRP, ts), lambda i, j: (i, 0, j))
    v_spec = pl.BlockSpec((1, ts // KEY_TILE, D_GRP, KEY_TILE), lambda i, j: (i, j, 0, 0))
    row_shape = jax.ShapeDtypeStruct((b, s, D_GRP), BF16)
    feat_shape = jax.ShapeDtypeStruct((b, D_GRP, s), BF16)
    v_shape = jax.ShapeDtypeStruct((b, s // KEY_TILE, D_GRP, KEY_TILE), BF16)
    return pl.pallas_call(
        functools.partial(_proj_kernel, d_model=d),
        grid=(b, s // ts),
        in_specs=[
            pl.BlockSpec((1, ts, d), lambda i, j: (i, j, 0)),
            pl.BlockSpec((1, 1, 3 * d), lambda i, j: (i, 0, 0)),
            const((1, d)),
            const(wq.shape), const(wk.shape), const(wv.shape), const(wg.shape), const(wf.shape),
            const(bf.shape), const(pq.shape), const(pk.shape),
        ],
        out_specs=[feat_major, row_major, v_spec, feat_major, row_major, v_spec,
                   pl.BlockSpec((1, 2 * D_GRP, ts), lambda i, j: (i, 0, j)), feat_major, row_major],
        out_shape=[feat_shape, row_shape, v_shape, feat_shape, row_shape, v_shape,
                   jax.ShapeDtypeStruct((b, 2 * D_GRP, s), BF16), feat_shape, row_shape],
        scratch_shapes=[pltpu.VMEM((FORGET_ROWS, LANES), F32)],
        compiler_params=pltpu.CompilerParams(
            dimension_semantics=("parallel", "arbitrary"), vmem_limit_bytes=VMEM_LIMIT),
        name="project",
    )(x, ada_l.reshape(b, 1, 3 * d), g_norm_l.reshape(1, d), wq, wk, wv, wg, wf, bf, pq, pk)


def _tile_iotas(tq):
    shape = (KEY_RUN, SUBLANES, tq)
    run = lax.broadcasted_iota(jnp.int32, shape, 0)
    sub = lax.broadcasted_iota(jnp.int32, shape, 1)
    qry = lax.broadcasted_iota(jnp.int32, shape, 2)
    return run, sub, qry


def _head_rows(block, j, width):
    row = lax.broadcasted_iota(jnp.int32, block.shape, 0)
    return jnp.where((row >= j * width) & (row < (j + 1) * width), block, jnp.zeros_like(block))


def _shift_up(x, k, fill):
    sub = lax.broadcasted_iota(jnp.int32, x.shape, 0)
    return jnp.where(sub + k < SUBLANES, pltpu.roll(x, SUBLANES - k, axis=0), fill)


def _pipeline_start(qt, stages, state, mask_first=True, lowest=None):
    stage_scores, stage_weights, _ = stages
    stage_scores(1, qt)
    state = stage_weights(1, 0, state, mask_first)
    stage_scores(0, jnp.maximum(qt - 1, 0))
    return state, (0 if lowest is None else lowest(state))


def _pipeline_loop(qt, stages, state, low, alive=None):
    stage_scores, stage_weights, stage_values = stages

    def half(slot, kt, state):
        stage_scores(1 - slot, jnp.maximum(kt - 1, low))
        state = stage_values(slot, kt + 1, state, None)
        return stage_weights(slot, 1 - slot, state, False)

    n_bodies = (qt - low + 1) // 2
    if alive is None:
        def body(i, state):
            kt = qt - 1 - 2 * i
            return half(1, kt - 1, half(0, kt, state))

        return lax.fori_loop(0, n_bodies, body, state), n_bodies

    def body(loop):
        i, _, state = loop
        kt = qt - 1 - 2 * i
        state = half(0, kt, state)
        go = alive(state, kt - 1)
        return i + 1, go, half(1, kt - 1, state)

    done, _, state = lax.while_loop(lambda loop: (loop[0] < n_bodies) & loop[1], body,
                                    (jnp.int32(0), alive(state, qt), state))
    return state, done


def _pipeline_finish(qt, stages, state, low, done):
    last = qt - 2 * done
    return stages[2](0, jnp.maximum(last, low), state, last >= low)


def _tile_pipeline(qt, stages, state, alive=None, mask_first=True, lowest=None):
    state, low = _pipeline_start(qt, stages, state, mask_first, lowest)
    state, done = _pipeline_loop(qt, stages, state, low, alive)
    return _pipeline_finish(qt, stages, state, low, done)


def _attn_scratch(tq, slots=2):
    return [pltpu.VMEM((slots, 2, KEY_TILE, tq), F32), pltpu.VMEM((slots, 2, KEY_TILE, tq), BF16)]


def _sb_kernel(q_ref, k_ref, v_ref, o_ref, z_ref, a_ref, *, tq, group):
    run, sub, qry = _tile_iotas(tq)
    causal = sub * KEY_RUN + run < qry
    refs = (q_ref, k_ref, v_ref, o_ref, z_ref, a_ref)
    lax.fori_loop(0, q_ref.shape[2] // (tq * group),
                  lambda j, _: _sb_query_tiles([j * group + i for i in range(group)], causal, refs, tq), 0)


def _sb_query_tiles(qts, causal, refs, tq):
    q_ref, k_ref, v_ref, o_ref, z_ref, a_ref = refs

    def weights(half_z, carry, masked):
        th = jnp.tanh(half_z.reshape(KEY_RUN, SUBLANES, tq))
        rem = 0.5 - 0.5 * th
        if masked:
            rem = jnp.where(causal, rem, 1.0)
        prod = jnp.ones((SUBLANES, tq), F32)
        parts = [None] * KEY_RUN
        for r in reversed(range(KEY_RUN)):
            below = prod * rem[r]
            parts[r] = prod - below
            prod = below
        incl = prod
        for step in (1, 2, 4):
            incl = incl * _shift_up(incl, step, 1.0)
        scale = _shift_up(incl, 1, 1.0) * carry
        a = (jnp.stack(parts) * scale[None]).reshape(KEY_TILE, tq).astype(BF16)
        return a, carry * jnp.broadcast_to(incl[0:1], carry.shape)

    def stages(i, qt):
        base = 2 * i
        cols = pl.ds(pl.multiple_of(qt * tq, tq), tq)
        q_heads = [_head_rows(q_ref[0, :, cols], h, HEAD_DIM) for h in range(2)]

        def stage_scores(slot, kt):
            start = pl.multiple_of(kt * KEY_TILE, KEY_TILE)
            k = k_ref[0, pl.ds(start, KEY_TILE), :]
            for h in range(2):
                z_ref[base + slot, h] = _dot(k, q_heads[h])

        def stage_weights(src, dst, state, masked):
            carry, acc = state
            new_carry = []
            for h in range(2):
                a_ref[base + dst, h], c = weights(z_ref[base + src, h], carry[h], masked)
                new_carry.append(c)
            return tuple(new_carry), acc

        def stage_values(slot, kt, state, valid):
            carry, acc = state
            new_acc = []
            for h in range(2):
                av = _dot(v_ref[0, kt, h * HEAD_DIM:(h + 1) * HEAD_DIM, :], a_ref[base + slot, h])
                new_acc.append(acc[h] + (av if valid is None else jnp.where(valid, av, 0.0)))
            return carry, tuple(new_acc)

        return cols, stage_scores, stage_weights, stage_values

    alive = lambda st, kt: jnp.max(jnp.maximum(st[0][0], st[0][1])) > 0.0
    tiles = [(qt,) + stages(i, qt) for i, qt in enumerate(qts)]
    init = ((jnp.ones((SUBLANES, tq), F32),) * 2, (jnp.zeros((HEAD_DIM, tq), F32),) * 2)
    states = [init] * len(tiles)

    for qt, _, stage_scores, _, _ in tiles:
        stage_scores(1, qt)
        stage_scores(0, jnp.maximum(qt - 1, 0))
    for i, (qt, _, _, stage_weights, _) in enumerate(tiles):
        states[i] = stage_weights(0, 0, stage_weights(1, 1, states[i], True), False)
    for i, (qt, _, _, _, stage_values) in enumerate(tiles):
        states[i] = stage_values(0, jnp.maximum(qt - 1, 0), stage_values(1, qt, states[i], None), qt >= 1)

    for i, (qt, cols, stage_scores, stage_weights, stage_values) in enumerate(tiles):
        def rest(state, qt=qt, fns=(stage_scores, stage_weights, stage_values)):
            return _tile_pipeline(qt - 2, fns, state, alive, mask_first=False)

        _, acc = lax.cond((qt >= 2) & alive(states[i], qt - 1), rest, lambda st: st, states[i])
        o_ref[0, :, cols] = jnp.concatenate(acc, axis=0).astype(BF16)
    return 0


def _sb_attention(q_t, k, v_t, tq):
    b, _, s = q_t.shape
    q_spec = pl.BlockSpec((1, LANES, s), lambda i, p: (i, p, 0))
    group = 8 if (s // tq) % 8 == 0 else 1
    return pl.pallas_call(
        functools.partial(_sb_kernel, tq=tq, group=group),
        grid=(b, N_PAIRS),
        in_specs=[
            q_spec,
            pl.BlockSpec((1, s, LANES), lambda i, p: (i, 0, p)),
            pl.BlockSpec((1, s // KEY_TILE, LANES, KEY_TILE), lambda i, p: (i, 0, p, 0)),
        ],
        out_specs=q_spec,
        out_shape=jax.ShapeDtypeStruct((b, D_GRP, s), BF16),
        scratch_shapes=_attn_scratch(tq, 2 * group),
        compiler_params=pltpu.CompilerParams(
            dimension_semantics=("parallel", "parallel"), vmem_limit_bytes=VMEM_LIMIT),
        name="sb_attention",
    )(q_t, k, v_t)


def _fox_kernel(q_ref, qa_ref, k_ref, ka_ref, v_ref, o_ref, z_ref, p_ref, tails_ref, *, tq, group):
    run, sub, qry = _tile_iotas(tq)
    causal = (sub * KEY_RUN + run <= qry).reshape(KEY_TILE, tq)

    n_tiles = k_ref.shape[1] // KEY_TILE
    tails_ref[...] = jnp.zeros(tails_ref.shape, F32)

    def scan_keys(t, k_abs):
        start = pl.multiple_of(t * KEY_TILE, KEY_TILE)
        k = k_ref[0, pl.ds(start, KEY_TILE), :]
        tail = ka_ref[0, pl.ds(start + KEY_TILE - 2 * SUBLANES, 2 * SUBLANES), :].astype(F32)
        tails_ref[pl.ds(t, 1), :] = tail[2 * SUBLANES - 1:, :]
        return jnp.maximum(k_abs, jnp.max(jnp.abs(k.astype(F32)), axis=0, keepdims=True))

    k_abs = lax.fori_loop(0, n_tiles, scan_keys, jnp.zeros((1, LANES), F32))
    feat = lax.broadcasted_iota(jnp.int32, (SUBLANES, LANES), 1)
    head = lax.broadcasted_iota(jnp.int32, (SUBLANES, LANES), 0)
    first = head * AUG_STRIDE + N_SPLIT
    pick = jnp.where((feat >= first) & (feat < first + N_SPLIT) & (head < 2), 1.0, 0.0).astype(BF16)
    neg_cum_rows = _dot_nt(pick, tails_ref[...].astype(BF16))
    neg_cum = [neg_cum_rows[h:h + 1, :] for h in range(2)]
    k_bound = [HEAD_DIM ** 0.5 * jnp.max(k_abs[:, h * HEAD_DIM:(h + 1) * HEAD_DIM], axis=1, keepdims=True)
               for h in range(2)]
    refs = (q_ref, qa_ref, k_ref, ka_ref, v_ref, o_ref, z_ref, p_ref)
    lax.fori_loop(0, q_ref.shape[2] // (tq * group),
                  lambda j, _: _fox_query_tiles([j * group + i for i in range(group)], causal, k_bound,
                                                neg_cum, refs, tq), 0)


def _fox_query_tiles(qts, causal, k_bound, neg_cum, refs, tq):
    q_ref, qa_ref, k_ref, ka_ref, v_ref, o_ref, z_ref, p_ref = refs

    def stages(i, qt):
        base = 2 * i
        cols = pl.ds(pl.multiple_of(qt * tq, tq), tq)
        q_heads = [jnp.concatenate([_head_rows(q_ref[0, :, cols], h, HEAD_DIM),
                                    _head_rows(qa_ref[0, :, cols], h, AUG_STRIDE)], axis=0)
                   for h in range(2)]

        def stage_scores(slot, kt):
            start = pl.multiple_of(kt * KEY_TILE, KEY_TILE)
            k = jnp.concatenate([k_ref[0, pl.ds(start, KEY_TILE), :],
                                 ka_ref[0, pl.ds(start, KEY_TILE), :]], axis=1)
            for h in range(2):
                z_ref[base + slot, h] = _dot(k, q_heads[h])

        def stage_weights(src, dst, state, masked):
            m, l, _, _, acc = state
            m_new, l_new, alpha = [], [], []
            for h in range(2):
                logits = z_ref[base + src, h]
                if masked:
                    logits = jnp.where(causal, logits, -jnp.inf)
                mh = jnp.maximum(m[h], jnp.max(logits, axis=0, keepdims=True))
                p = jnp.exp2(logits - mh)
                ah = jnp.exp2(m[h] - mh)
                p_ref[base + dst, h] = p.astype(BF16)
                m_new.append(mh)
                alpha.append(ah)
                l_new.append(ah * l[h] + jnp.sum(p, axis=0, keepdims=True))
            return tuple(m_new), tuple(l_new), l, tuple(alpha), acc

        def stage_values(slot, kt, state, valid):
            m, l, l_prev, alpha, acc = state
            new_acc = []
            for h in range(2):
                pv = _dot(v_ref[0, kt, h * HEAD_DIM:(h + 1) * HEAD_DIM, :], p_ref[base + slot, h])
                upd = alpha[h] * acc[h] + pv
                new_acc.append(upd if valid is None else jnp.where(valid, upd, acc[h]))
            if valid is not None:
                l = tuple(jnp.where(valid, l[h], l_prev[h]) for h in range(2))
            return m, l, l_prev, alpha, tuple(new_acc)

        q32 = q_ref[0, :, cols].astype(F32)
        qa32 = qa_ref[0, :, cols].astype(F32)
        reach = []
        for h in range(2):
            q_norm = jnp.sqrt(jnp.sum(jnp.square(q32[h * HEAD_DIM:(h + 1) * HEAD_DIM]), axis=0, keepdims=True))
            cum_q = jnp.sum(qa32[h * AUG_STRIDE:h * AUG_STRIDE + N_SPLIT], axis=0, keepdims=True)
            reach.append(q_norm * k_bound[h] + cum_q)

        def lowest(state):
            m = state[0]
            tile = lax.broadcasted_iota(jnp.int32, (1, LANES), 1)
            first = []
            for h in range(2):
                gap = jnp.max(reach[h] - m[h], axis=1, keepdims=True)
                needed = (gap + neg_cum[h] >= -UNDERFLOW_BITS) | (tile >= qt)
                first.append(jnp.min(jnp.where(needed, tile, LANES).astype(F32), axis=1, keepdims=True))
            return jnp.minimum(first[0], first[1])[0, 0].astype(jnp.int32)

        return cols, (stage_scores, stage_weights, stage_values), lowest

    row = lambda v: (jnp.full((1, tq), v, F32),) * 2
    init = (row(-jnp.inf), row(0.0), row(0.0), row(1.0), (jnp.zeros((HEAD_DIM, tq), F32),) * 2)
    tiles = [(qt,) + stages(i, qt) for i, qt in enumerate(qts)]
    started = [_pipeline_start(qt, fns, init, True, lowest) for qt, _, fns, lowest in tiles]
    looped = [_pipeline_loop(qt, fns, state, low) for (qt, _, fns, _), (state, low) in zip(tiles, started)]
    for (qt, cols, fns, _), (_, low), (state, done) in zip(tiles, started, looped):
        _, l, _, _, acc = _pipeline_finish(qt, fns, state, low, done)
        o_ref[0, :, cols] = jnp.concatenate([acc[h] / l[h] for h in range(2)], axis=0).astype(BF16)
    return 0


def _fox_attention(q_t, qaug_t, k, kaug, v_t, tq):
    b, _, s = q_t.shape
    assert s // KEY_TILE <= LANES
    q_spec = pl.BlockSpec((1, LANES, s), lambda i, p: (i, p, 0))
    k_spec = pl.BlockSpec((1, s, LANES), lambda i, p: (i, 0, p))
    group = 8 if (s // tq) % 8 == 0 else 1
    return pl.pallas_call(
        functools.partial(_fox_kernel, tq=tq, group=group),
        grid=(b, N_PAIRS),
        in_specs=[q_spec, q_spec, k_spec, k_spec,
                  pl.BlockSpec((1, s // KEY_TILE, LANES, KEY_TILE), lambda i, p: (i, 0, p, 0))],
        out_specs=q_spec,
        out_shape=jax.ShapeDtypeStruct((b, D_GRP, s), BF16),
        scratch_shapes=_attn_scratch(tq, 2 * group) + [pltpu.VMEM((LANES, LANES), F32)],
        compiler_params=pltpu.CompilerParams(
            dimension_semantics=("parallel", "parallel"), vmem_limit_bytes=VMEM_LIMIT),
        name="fox_attention",
    )(q_t, qaug_t, k, kaug, v_t)


def _rms_rows(x):
    return x * lax.rsqrt(jnp.mean(x * x, axis=0, keepdims=True) + EPS)


def _out_kernel(osb_ref, ofx_ref, gate_ref, x_ref, wout_ref, ada_ref, gf_ref, o_ref, *, d_model, final):
    y = jnp.concatenate([_rms_rows(osb_ref[0].astype(F32)), _rms_rows(ofx_ref[0].astype(F32))], axis=0)
    g = gate_ref[0].astype(F32)
    y = (y * (g / (1.0 + jnp.exp(-g)))).astype(BF16)
    gate = ada_ref[0][:, 2 * d_model:]
    out = x_ref[0] + (1.0 + gate) * _dot_tn(y, wout_ref[...])
    if final:
        out = out * lax.rsqrt(jnp.mean(out * out, axis=-1, keepdims=True) + EPS) * gf_ref[...]
    o_ref[0] = out


def _output(o_sb, o_fx, gate_t, x, wout, ada_l, g_final, ts, final):
    b, s, d = x.shape
    const = lambda shape: pl.BlockSpec(shape, lambda i, j: (0,) * len(shape))
    return pl.pallas_call(
        functools.partial(_out_kernel, d_model=d, final=final),
        grid=(b, s // ts),
        in_specs=[
            pl.BlockSpec((1, D_GRP, ts), lambda i, j: (i, 0, j)),
            pl.BlockSpec((1, D_GRP, ts), lambda i, j: (i, 0, j)),
            pl.BlockSpec((1, 2 * D_GRP, ts), lambda i, j: (i, 0, j)),
            pl.BlockSpec((1, ts, d), lambda i, j: (i, j, 0)),
            const(wout.shape),
            pl.BlockSpec((1, 1, 3 * d), lambda i, j: (i, 0, 0)),
            const((1, d)),
        ],
        out_specs=pl.BlockSpec((1, ts, d), lambda i, j: (i, j, 0)),
        out_shape=jax.ShapeDtypeStruct((b, s, d), F32),
        compiler_params=pltpu.CompilerParams(
            dimension_semantics=("parallel", "parallel"), vmem_limit_bytes=VMEM_LIMIT),
        name="output",
    )(o_sb, o_fx, gate_t, x, wout, ada_l.reshape(b, 1, 3 * d), g_final.reshape(1, d))


def kernel(x, c, w_ada, b_ada, g_norm, w_in, b_f, g_grp, w_out, g_final):
    b, s, d = x.shape
    depth = w_ada.shape[0]
    ts = min(512, s)
    tq = min(256, s)
    consts = _aug_constants()

    grp = lambda i: w_in[:, :, i * D_GRP:(i + 1) * D_GRP]
    t = lambda w: jnp.swapaxes(w, 1, 2)
    wq = jnp.stack([t(grp(0)) * (0.5 * Q_SCALE), t(grp(3)) * (LOG2E * Q_SCALE)], axis=1).astype(BF16)
    wk = jnp.stack([grp(1), grp(4)], axis=1).astype(BF16)
    wv = jnp.stack([t(grp(2)), t(grp(5))], axis=1).astype(BF16)
    wg = t(w_in[:, :, 6 * D_GRP:8 * D_GRP]).astype(BF16)
    wf = jnp.zeros((depth, FORGET_ROWS, d), BF16).at[:, :N_HEADS].set(t(w_in[:, :, 8 * D_GRP:]).astype(BF16))
    bf = jnp.zeros((depth, FORGET_ROWS, LANES), F32).at[:, :N_HEADS].set(b_f[:, :, None])
    wout = (g_grp[:, :, None] * w_out).astype(BF16)

    ada = _adaln(c, w_ada, b_ada)
    for l in range(depth):
        q_sb, k_sb, v_sb, q_fx, k_fx, v_fx, gate_t, qaug, kaug = _project(
            x, ada[l], g_norm[l], wq[l], wk[l], wv[l], wg[l], wf[l], bf[l], consts, ts)
        o_sb = _sb_attention(q_sb, k_sb, v_sb, tq)
        o_fx = _fox_attention(q_fx, qaug, k_fx, kaug, v_fx, tq)
        x = _output(o_sb, o_fx, gate_t, x, wout[l], ada[l], g_final, ts, l == depth - 1)
    return x
```

```python
import functools

import numpy as np
import jax
import jax.numpy as jnp
from jax import lax
from jax.experimental import pallas as pl
from jax.experimental.pallas import tpu as pltpu

F32 = jnp.float32
BF16 = jnp.bfloat16

HEAD_DIM = 64
N_HEADS = 8
D_GRP = N_HEADS * HEAD_DIM
LANES = 128
SUBLANES = 8
N_PAIRS = D_GRP // LANES
KEY_TILE = 256
KEY_RUN = KEY_TILE // SUBLANES
AUG_STRIDE = 16
N_SPLIT = 3
FORGET_ROWS = 2 * SUBLANES
EPS = 1e-6
Q_SCALE = HEAD_DIM ** -0.5
LOG2E = 1.4426950408889634
VMEM_LIMIT = 52 * 1024 * 1024
UNDERFLOW_BITS = 160.0


def _split_bf16(x, n):
    parts = []
    r = x
    for i in range(n):
        p = r.astype(BF16)
        parts.append(p)
        if i + 1 < n:
            r = r - p.astype(F32)
    return parts


def _dot(a, b):
    return jnp.dot(a, b, preferred_element_type=F32)


def _dot_nt(a, b):
    return lax.dot_general(a, b, (((1,), (1,)), ((), ())), preferred_element_type=F32)


def _dot_tn(a, b):
    return lax.dot_general(a, b, (((0,), (0,)), ((), ())), preferred_element_type=F32)


def _softplus(z):
    return jnp.maximum(z, 0.0) + jnp.log1p(jnp.exp(-jnp.abs(z)))


def _key_of_row(p):
    return (p & (SUBLANES - 1)) * KEY_RUN + (p >> 3)


def _ada_kernel(c_ref, w_ref, b_ref, o_ref):
    c = c_ref[...]
    c_act = c / (1.0 + jnp.exp(-c))
    w_parts = _split_bf16(w_ref[0], 2)
    acc = jnp.zeros(o_ref.shape[1:], F32)
    for cp in _split_bf16(c_act, N_SPLIT):
        for wp in w_parts:
            acc = acc + _dot(cp, wp)
    o_ref[0] = acc + b_ref[0]


def _adaln(c, w_ada, b_ada):
    depth, d, d3 = w_ada.shape
    b = c.shape[0]
    rows = SUBLANES
    tn = 1024
    c_pad = jnp.zeros((rows, d), F32).at[:b].set(c)
    out = pl.pallas_call(
        _ada_kernel,
        grid=(depth, d3 // tn),
        in_specs=[
            pl.BlockSpec((rows, d), lambda l, n: (0, 0)),
            pl.BlockSpec((1, d, tn), lambda l, n: (l, 0, n)),
            pl.BlockSpec((1, 1, tn), lambda l, n: (l, 0, n)),
        ],
        out_specs=pl.BlockSpec((1, rows, tn), lambda l, n: (l, 0, n)),
        out_shape=jax.ShapeDtypeStruct((depth, rows, d3), F32),
        compiler_params=pltpu.CompilerParams(
            dimension_semantics=("parallel", "parallel"), vmem_limit_bytes=VMEM_LIMIT),
        name="adaln",
    )(c_pad, w_ada, b_ada.reshape(depth, 1, d3))
    return out[:, :b]


def _proj_kernel(x_ref, ada_ref, gn_ref, wq_ref, wk_ref, wv_ref, wg_ref, wf_ref, bf_ref,
                 pq_ref, pk_ref,
                 qsb_ref, ksb_ref, vsb_ref, qfx_ref, kfx_ref, vfx_ref, gate_ref, qaug_ref, kaug_ref,
                 carry_ref, *, d_model):
    ts = x_ref.shape[1]
    x = x_ref[0]
    ada = ada_ref[0]
    shift = ada[:, :d_model]
    scale = ada[:, d_model:2 * d_model]
    r = lax.rsqrt(jnp.mean(x * x, axis=-1, keepdims=True) + EPS)
    h = ((x * r) * gn_ref[...] * (1.0 + scale) + shift).astype(BF16)

    row = lax.broadcasted_iota(jnp.int32, (ts, ts), 0)
    col = lax.broadcasted_iota(jnp.int32, (ts, ts), 1)
    tile_row = lax.broadcasted_iota(jnp.int32, (KEY_TILE, KEY_TILE), 0)
    tile_col = lax.broadcasted_iota(jnp.int32, (KEY_TILE, KEY_TILE), 1)
    perm = jnp.where(tile_col == _key_of_row(tile_row), 1.0, 0.0).astype(BF16)
    h_keys = jnp.concatenate([_dot(perm, h[t * KEY_TILE:(t + 1) * KEY_TILE])
                              for t in range(ts // KEY_TILE)], axis=0).astype(BF16)

    qsb_ref[0] = _dot_nt(wq_ref[0], h).astype(BF16)
    qfx_ref[0] = _dot_nt(wq_ref[1], h).astype(BF16)
    gate_ref[0] = _dot_nt(wg_ref[...], h).astype(BF16)
    ksb_ref[0] = _dot(h_keys, wk_ref[0]).astype(BF16)
    kfx_ref[0] = _dot(h_keys, wk_ref[1]).astype(BF16)
    for o_ref, w in ((vsb_ref, wv_ref[0]), (vfx_ref, wv_ref[1])):
        v_t = _dot_nt(w, h_keys).astype(BF16)
        for t in range(ts // KEY_TILE):
            o_ref[0, t] = v_t[:, t * KEY_TILE:(t + 1) * KEY_TILE]

    tile_lanes = lambda a: jnp.concatenate([a] * (ts // LANES), axis=1)
    zf = _dot_nt(wf_ref[...], h) + tile_lanes(bf_ref[...])
    log_f_parts = jnp.concatenate(_split_bf16(-_softplus(-zf), N_SPLIT), axis=0)

    @pl.when(pl.program_id(1) == 0)
    def _():
        carry_ref[...] = jnp.zeros_like(carry_ref)

    carry = tile_lanes(carry_ref[...])
    key_col = (col - (col & (KEY_TILE - 1))) + _key_of_row(col & (KEY_TILE - 1))
    upto = jnp.where(row <= col, 1.0, 0.0).astype(BF16)
    upto_keys = jnp.where(row <= key_col, 1.0, 0.0).astype(BF16)
    sums = _dot(log_f_parts, jnp.concatenate([upto, upto_keys], axis=1))
    fold = lambda x: x[:FORGET_ROWS] + x[FORGET_ROWS:2 * FORGET_ROWS] + x[2 * FORGET_ROWS:]
    cum = carry + fold(sums[:, :ts])
    cum_keys = carry + fold(sums[:, ts:])
    carry_ref[...] = jnp.broadcast_to(cum[:, ts - 1:], carry_ref.shape)

    ones = jnp.ones((FORGET_ROWS, ts), BF16)
    pieces = jnp.concatenate(_split_bf16(cum * LOG2E, N_SPLIT) + [ones], axis=0)
    pieces_keys = jnp.concatenate(_split_bf16(cum_keys * LOG2E, N_SPLIT) + [ones], axis=0)
    qaug_ref[0] = _dot(pq_ref[...], pieces).astype(BF16)
    kaug_ref[0] = _dot_tn(pieces_keys, pk_ref[...]).astype(BF16)


def _aug_constants():
    pq = np.zeros((D_GRP, (N_SPLIT + 1) * FORGET_ROWS), np.float32)
    pk = np.zeros(((N_SPLIT + 1) * FORGET_ROWS, D_GRP), np.float32)
    one_row = N_SPLIT * FORGET_ROWS
    for h in range(N_HEADS):
        base = (h // 2) * LANES + AUG_STRIDE * (h % 2)
        for j in range(N_SPLIT):
            pq[base + j, j * FORGET_ROWS + h] = 1.0
            pk[one_row, base + j] = 1.0
            pk[j * FORGET_ROWS + h, base + N_SPLIT + j] = -1.0
            pq[base + N_SPLIT + j, one_row] = 1.0
    return jnp.asarray(pq, BF16), jnp.asarray(pk, BF16)


def _project(x, ada_l, g_norm_l, wq, wk, wv, wg, wf, bf, consts, ts):
    b, s, d = x.shape
    pq, pk = consts
    const = lambda shape: pl.BlockSpec(shape, lambda i, j: (0,) * len(shape))
    row_major = pl.BlockSpec((1, ts, D_GRP), lambda i, j: (i, j, 0))
    feat_major = pl.BlockSpec((1, D_GRP, ts), lambda i, j: (i, 0, j))
    v_spec = pl.BlockSpec((1, ts // KEY_TILE, D_GRP, KEY_TILE), lambda i, j: (i, j, 0, 0))
    row_shape = jax.ShapeDtypeStruct((b, s, D_GRP), BF16)
    feat_shape = jax.ShapeDtypeStruct((b, D_GRP, s), BF16)
    v_shape = jax.ShapeDtypeStruct((b, s // KEY_TILE, D_GRP, KEY_TILE), BF16)
    return pl.pallas_call(
        functools.partial(_proj_kernel, d_model=d),
        grid=(b, s // ts),
        in_specs=[
            pl.BlockSpec((1, ts, d), lambda i, j: (i, j, 0)),
            pl.BlockSpec((1, 1, 3 * d), lambda i, j: (i, 0, 0)),
            const((1, d)),
            const(wq.shape), const(wk.shape), const(wv.shape), const(wg.shape), const(wf.shape),
            const(bf.shape), const(pq.shape), const(pk.shape),
        ],
        out_specs=[feat_major, row_major, v_spec, feat_major, row_major, v_spec,
                   pl.BlockSpec((1, 2 * D_GRP, ts), lambda i, j: (i, 0, j)), feat_major, row_major],
        out_shape=[feat_shape, row_shape, v_shape, feat_shape, row_shape, v_shape,
                   jax.ShapeDtypeStruct((b, 2 * D_GRP, s), BF16), feat_shape, row_shape],
        scratch_shapes=[pltpu.VMEM((FORGET_ROWS, LANES), F32)],
        compiler_params=pltpu.CompilerParams(
            dimension_semantics=("parallel", "arbitrary"), vmem_limit_bytes=VMEM_LIMIT),
        name="project",
    )(x, ada_l.reshape(b, 1, 3 * d), g_norm_l.reshape(1, d), wq, wk, wv, wg, wf, bf, pq, pk)


def _tile_iotas(tq):
    shape = (KEY_RUN, SUBLANES, tq)
    run = lax.broadcasted_iota(jnp.int32, shape, 0)
    sub = lax.broadcasted_iota(jnp.int32, shape, 1)
    qry = lax.broadcasted_iota(jnp.int32, shape, 2)
    return run, sub, qry


def _head_rows(block, j, width):
    row = lax.broadcasted_iota(jnp.int32, block.shape, 0)
    return jnp.where((row >= j * width) & (row < (j + 1) * width), block, jnp.zeros_like(block))


def _shift_up(x, k, fill):
    sub = lax.broadcasted_iota(jnp.int32, x.shape, 0)
    return jnp.where(sub + k < SUBLANES, pltpu.roll(x, SUBLANES - k, axis=0), fill)


def _pipeline_start(qt, stages, state, mask_first=True, lowest=None):
    stage_scores, stage_weights, _ = stages
    stage_scores(1, qt)
    state = stage_weights(1, 0, state, mask_first)
    stage_scores(0, jnp.maximum(qt - 1, 0))
    return state, (0 if lowest is None else lowest(state))


def _pipeline_loop(qt, stages, state, low, alive=None):
    stage_scores, stage_weights, stage_values = stages

    def half(slot, kt, state):
        stage_scores(1 - slot, jnp.maximum(kt - 1, low))
        state = stage_values(slot, kt + 1, state, None)
        return stage_weights(slot, 1 - slot, state, False)

    n_bodies = (qt - low + 1) // 2
    if alive is None:
        def body(i, state):
            kt = qt - 1 - 2 * i
            return half(1, kt - 1, half(0, kt, state))

        return lax.fori_loop(0, n_bodies, body, state), n_bodies

    def body(loop):
        i, _, state = loop
        kt = qt - 1 - 2 * i
        state = half(0, kt, state)
        go = alive(state, kt - 1)
        return i + 1, go, half(1, kt - 1, state)

    done, _, state = lax.while_loop(lambda loop: (loop[0] < n_bodies) & loop[1], body,
                                    (jnp.int32(0), alive(state, qt), state))
    return state, done


def _pipeline_finish(qt, stages, state, low, done):
    last = qt - 2 * done
    return stages[2](0, jnp.maximum(last, low), state, last >= low)


def _tile_pipeline(qt, stages, state, alive=None, mask_first=True, lowest=None):
    state, low = _pipeline_start(qt, stages, state, mask_first, lowest)
    state, done = _pipeline_loop(qt, stages, state, low, alive)
    return _pipeline_finish(qt, stages, state, low, done)


def _attn_scratch(tq, slots=2):
    return [pltpu.VMEM((slots, 2, KEY_TILE, tq), F32), pltpu.VMEM((slots, 2, KEY_TILE, tq), BF16)]


def _sb_kernel(q_ref, k_ref, v_ref, o_ref, z_ref, a_ref, *, tq, group):
    run, sub, qry = _tile_iotas(tq)
    causal = sub * KEY_RUN + run < qry
    refs = (q_ref, k_ref, v_ref, o_ref, z_ref, a_ref)
    lax.fori_loop(0, q_ref.shape[2] // (tq * group),
                  lambda j, _: _sb_query_tiles([j * group + i for i in range(group)], causal, refs, tq), 0)


def _sb_query_tiles(qts, causal, refs, tq):
    q_ref, k_ref, v_ref, o_ref, z_ref, a_ref = refs

    def weights(half_z, carry, masked):
        th = jnp.tanh(half_z.reshape(KEY_RUN, SUBLANES, tq))
        rem = 0.5 - 0.5 * th
        if masked:
            rem = jnp.where(causal, rem, 1.0)
        prod = jnp.ones((SUBLANES, tq), F32)
        parts = [None] * KEY_RUN
        for r in reversed(range(KEY_RUN)):
            below = prod * rem[r]
            parts[r] = prod - below
            prod = below
        incl = prod
        for step in (1, 2, 4):
            incl = incl * _shift_up(incl, step, 1.0)
        scale = _shift_up(incl, 1, 1.0) * carry
        a = (jnp.stack(parts) * scale[None]).reshape(KEY_TILE, tq).astype(BF16)
        return a, carry * jnp.broadcast_to(incl[0:1], carry.shape)

    def stages(i, qt):
        base = 2 * i
        cols = pl.ds(pl.multiple_of(qt * tq, tq), tq)
        q_heads = [_head_rows(q_ref[0, :, cols], h, HEAD_DIM) for h in range(2)]

        def stage_scores(slot, kt):
            start = pl.multiple_of(kt * KEY_TILE, KEY_TILE)
            k = k_ref[0, pl.ds(start, KEY_TILE), :]
            for h in range(2):
                z_ref[base + slot, h] = _dot(k, q_heads[h])

        def stage_weights(src, dst, state, masked):
            carry, acc = state
            new_carry = []
            for h in range(2):
                a_ref[base + dst, h], c = weights(z_ref[base + src, h], carry[h], masked)
                new_carry.append(c)
            return tuple(new_carry), acc

        def stage_values(slot, kt, state, valid):
            carry, acc = state
            new_acc = []
            for h in range(2):
                av = _dot(v_ref[0, kt, h * HEAD_DIM:(h + 1) * HEAD_DIM, :], a_ref[base + slot, h])
                new_acc.append(acc[h] + (av if valid is None else jnp.where(valid, av, 0.0)))
            return carry, tuple(new_acc)

        return cols, stage_scores, stage_weights, stage_values

    alive = lambda st, kt: jnp.max(jnp.maximum(st[0][0], st[0][1])) > 0.0
    tiles = [(qt,) + stages(i, qt) for i, qt in enumerate(qts)]
    init = ((jnp.ones((SUBLANES, tq), F32),) * 2, (jnp.zeros((HEAD_DIM, tq), F32),) * 2)
    states = [init] * len(tiles)

    for qt, _, stage_scores, _, _ in tiles:
        stage_scores(1, qt)
        stage_scores(0, jnp.maximum(qt - 1, 0))
    for i, (qt, _, _, stage_weights, _) in enumerate(tiles):
        states[i] = stage_weights(0, 0, stage_weights(1, 1, states[i], True), False)
    for i, (qt, _, _, _, stage_values) in enumerate(tiles):
        states[i] = stage_values(0, jnp.maximum(qt - 1, 0), stage_values(1, qt, states[i], None), qt >= 1)

    for i, (qt, cols, stage_scores, stage_weights, stage_values) in enumerate(tiles):
        def rest(state, qt=qt, fns=(stage_scores, stage_weights, stage_values)):
            return _tile_pipeline(qt - 2, fns, state, alive, mask_first=False)

        _, acc = lax.cond((qt >= 2) & alive(states[i], qt - 1), rest, lambda st: st, states[i])
        o_ref[0, :, cols] = jnp.concatenate(acc, axis=0).astype(BF16)
    return 0


def _sb_attention(q_t, k, v_t, tq):
    b, _, s = q_t.shape
    q_spec = pl.BlockSpec((1, LANES, s), lambda i, p: (i, p, 0))
    group = 8 if (s // tq) % 8 == 0 else 1
    return pl.pallas_call(
        functools.partial(_sb_kernel, tq=tq, group=group),
        grid=(b, N_PAIRS),
        in_specs=[
            q_spec,
            pl.BlockSpec((1, s, LANES), lambda i, p: (i, 0, p)),
            pl.BlockSpec((1, s // KEY_TILE, LANES, KEY_TILE), lambda i, p: (i, 0, p, 0)),
        ],
        out_specs=q_spec,
        out_shape=jax.ShapeDtypeStruct((b, D_GRP, s), BF16),
        scratch_shapes=_attn_scratch(tq, 2 * group),
        compiler_params=pltpu.CompilerParams(
            dimension_semantics=("parallel", "parallel"), vmem_limit_bytes=VMEM_LIMIT),
        name="sb_attention",
    )(q_t, k, v_t)


def _fox_kernel(q_ref, qa_ref, k_ref, ka_ref, v_ref, o_ref, z_ref, p_ref, tails_ref, *, tq, group):
    run, sub, qry = _tile_iotas(tq)
    causal = (sub * KEY_RUN + run <= qry).reshape(KEY_TILE, tq)

    n_tiles = k_ref.shape[1] // KEY_TILE
    tails_ref[...] = jnp.zeros(tails_ref.shape, F32)

    def scan_keys(t, k_abs):
        start = pl.multiple_of(t * KEY_TILE, KEY_TILE)
        k = k_ref[0, pl.ds(start, KEY_TILE), :]
        tail = ka_ref[0, pl.ds(start + KEY_TILE - 2 * SUBLANES, 2 * SUBLANES), :].astype(F32)
        tails_ref[pl.ds(t, 1), :] = tail[2 * SUBLANES - 1:, :]
        return jnp.maximum(k_abs, jnp.max(jnp.abs(k.astype(F32)), axis=0, keepdims=True))

    k_abs = lax.fori_loop(0, n_tiles, scan_keys, jnp.zeros((1, LANES), F32))
    feat = lax.broadcasted_iota(jnp.int32, (SUBLANES, LANES), 1)
    head = lax.broadcasted_iota(jnp.int32, (SUBLANES, LANES), 0)
    first = head * AUG_STRIDE + N_SPLIT
    pick = jnp.where((feat >= first) & (feat < first + N_SPLIT) & (head < 2), 1.0, 0.0).astype(BF16)
    neg_cum_rows = _dot_nt(pick, tails_ref[...].astype(BF16))
    neg_cum = [neg_cum_rows[h:h + 1, :] for h in range(2)]
    k_bound = [HEAD_DIM ** 0.5 * jnp.max(k_abs[:, h * HEAD_DIM:(h + 1) * HEAD_DIM], axis=1, keepdims=True)
               for h in range(2)]
    refs = (q_ref, qa_ref, k_ref, ka_ref, v_ref, o_ref, z_ref, p_ref)
    lax.fori_loop(0, q_ref.shape[2] // (tq * group),
                  lambda j, _: _fox_query_tiles([j * group + i for i in range(group)], causal, k_bound,
                                                neg_cum, refs, tq), 0)


def _fox_query_tiles(qts, causal, k_bound, neg_cum, refs, tq):
    q_ref, qa_ref, k_ref, ka_ref, v_ref, o_ref, z_ref, p_ref = refs

    def stages(i, qt):
        base = 2 * i
        cols = pl.ds(pl.multiple_of(qt * tq, tq), tq)
        q_heads = [jnp.concatenate([_head_rows(q_ref[0, :, cols], h, HEAD_DIM),
                                    _head_rows(qa_ref[0, :, cols], h, AUG_STRIDE)], axis=0)
                   for h in range(2)]

        def stage_scores(slot, kt):
            start = pl.multiple_of(kt * KEY_TILE, KEY_TILE)
            k = jnp.concatenate([k_ref[0, pl.ds(start, KEY_TILE), :],
                                 ka_ref[0, pl.ds(start, KEY_TILE), :]], axis=1)
            for h in range(2):
                z_ref[base + slot, h] = _dot(k, q_heads[h])

        def stage_weights(src, dst, state, masked):
            m, l, _, _, acc = state
            m_new, l_new, alpha = [], [], []
            for h in range(2):
                logits = z_ref[base + src, h]
                if masked:
                    logits = jnp.where(causal, logits, -jnp.inf)
                mh = jnp.maximum(m[h], jnp.max(logits, axis=0, keepdims=True))
                p = jnp.exp2(logits - mh)
                ah = jnp.exp2(m[h] - mh)
                p_ref[base + dst, h] = p.astype(BF16)
                m_new.append(mh)
                alpha.append(ah)
                l_new.append(ah * l[h] + jnp.sum(p, axis=0, keepdims=True))
            return tuple(m_new), tuple(l_new), l, tuple(alpha), acc

        def stage_values(slot, kt, state, valid):
            m, l, l_prev, alpha, acc = state
            new_acc = []
            for h in range(2):
                pv = _dot(v_ref[0, kt, h * HEAD_DIM:(h + 1) * HEAD_DIM, :], p_ref[base + slot, h])
                upd = alpha[h] * acc[h] + pv
                new_acc.append(upd if valid is None else jnp.where(valid, upd, acc[h]))
            if valid is not None:
                l = tuple(jnp.where(valid, l[h], l_prev[h]) for h in range(2))
            return m, l, l_prev, alpha, tuple(new_acc)

        q32 = q_ref[0, :, cols].astype(F32)
        qa32 = qa_ref[0, :, cols].astype(F32)
        reach = []
        for h in range(2):
            q_norm = jnp.sqrt(jnp.sum(jnp.square(q32[h * HEAD_DIM:(h + 1) * HEAD_DIM]), axis=0, keepdims=True))
            cum_q = jnp.sum(qa32[h * AUG_STRIDE:h * AUG_STRIDE + N_SPLIT], axis=0, keepdims=True)
            reach.append(q_norm * k_bound[h] + cum_q)

        def lowest(state):
            m = state[0]
            tile = lax.broadcasted_iota(jnp.int32, (1, LANES), 1)
            first = []
            for h in range(2):
                gap = jnp.max(reach[h] - m[h], axis=1, keepdims=True)
                needed = (gap + neg_cum[h] >= -UNDERFLOW_BITS) | (tile >= qt)
                first.append(jnp.min(jnp.where(needed, tile, LANES).astype(F32), axis=1, keepdims=True))
            return jnp.minimum(first[0], first[1])[0, 0].astype(jnp.int32)

        return cols, (stage_scores, stage_weights, stage_values), lowest

    row = lambda v: (jnp.full((1, tq), v, F32),) * 2
    init = (row(-jnp.inf), row(0.0), row(0.0), row(1.0), (jnp.zeros((HEAD_DIM, tq), F32),) * 2)
    tiles = [(qt,) + stages(i, qt) for i, qt in enumerate(qts)]
    started = [_pipeline_start(qt, fns, init, True, lowest) for qt, _, fns, lowest in tiles]
    looped = [_pipeline_loop(qt, fns, state, low) for (qt, _, fns, _), (state, low) in zip(tiles, started)]
    for (qt, cols, fns, _), (_, low), (state, done) in zip(tiles, started, looped):
        _, l, _, _, acc = _pipeline_finish(qt, fns, state, low, done)
        o_ref[0, :, cols] = jnp.concatenate([acc[h] / l[h] for h in range(2)], axis=0).astype(BF16)
    return 0


def _fox_attention(q_t, qaug_t, k, kaug, v_t, tq):
    b, _, s = q_t.shape
    assert s // KEY_TILE <= LANES
    q_spec = pl.BlockSpec((1, LANES, s), lambda i, p: (i, p, 0))
    k_spec = pl.BlockSpec((1, s, LANES), lambda i, p: (i, 0, p))
    group = 8 if (s // tq) % 8 == 0 else 1
    return pl.pallas_call(
        functools.partial(_fox_kernel, tq=tq, group=group),
        grid=(b, N_PAIRS),
        in_specs=[q_spec, q_spec, k_spec, k_spec,
                  pl.BlockSpec((1, s // KEY_TILE, LANES, KEY_TILE), lambda i, p: (i, 0, p, 0))],
        out_specs=q_spec,
        out_shape=jax.ShapeDtypeStruct((b, D_GRP, s), BF16),
        scratch_shapes=_attn_scratch(tq, 2 * group) + [pltpu.VMEM((LANES, LANES), F32)],
        compiler_params=pltpu.CompilerParams(
            dimension_semantics=("parallel", "parallel"), vmem_limit_bytes=VMEM_LIMIT),
        name="fox_attention",
    )(q_t, qaug_t, k, kaug, v_t)


def _rms_rows(x):
    return x * lax.rsqrt(jnp.mean(x * x, axis=0, keepdims=True) + EPS)


def _out_kernel(osb_ref, ofx_ref, gate_ref, x_ref, wout_ref, ada_ref, gf_ref, o_ref, *, d_model, final):
    y = jnp.concatenate([_rms_rows(osb_ref[0].astype(F32)), _rms_rows(ofx_ref[0].astype(F32))], axis=0)
    g = gate_ref[0].astype(F32)
    y = (y * (g / (1.0 + jnp.exp(-g)))).astype(BF16)
    gate = ada_ref[0][:, 2 * d_model:]
    out = x_ref[0] + (1.0 + gate) * _dot_tn(y, wout_ref[...])
    if final:
        out = out * lax.rsqrt(jnp.mean(out * out, axis=-1, keepdims=True) + EPS) * gf_ref[...]
    o_ref[0] = out


def _output(o_sb, o_fx, gate_t, x, wout, ada_l, g_final, ts, final):
    b, s, d = x.shape
    const = lambda shape: pl.BlockSpec(shape, lambda i, j: (0,) * len(shape))
    return pl.pallas_call(
        functools.partial(_out_kernel, d_model=d, final=final),
        grid=(b, s // ts),
        in_specs=[
            pl.BlockSpec((1, D_GRP, ts), lambda i, j: (i, 0, j)),
            pl.BlockSpec((1, D_GRP, ts), lambda i, j: (i, 0, j)),
            pl.BlockSpec((1, 2 * D_GRP, ts), lambda i, j: (i, 0, j)),
            pl.BlockSpec((1, ts, d), lambda i, j: (i, j, 0)),
            const(wout.shape),
            pl.BlockSpec((1, 1, 3 * d), lambda i, j: (i, 0, 0)),
            const((1, d)),
        ],
        out_specs=pl.BlockSpec((1, ts, d), lambda i, j: (i, j, 0)),
        out_shape=jax.ShapeDtypeStruct((b, s, d), F32),
        compiler_params=pltpu.CompilerParams(
            dimension_semantics=("parallel", "parallel"), vmem_limit_bytes=VMEM_LIMIT),
        name="output",
    )(o_sb, o_fx, gate_t, x, wout, ada_l.reshape(b, 1, 3 * d), g_final.reshape(1, d))


def kernel(x, c, w_ada, b_ada, g_norm, w_in, b_f, g_grp, w_out, g_final):
    b, s, d = x.shape
    depth = w_ada.shape[0]
    ts = min(512, s)
    tq = min(256, s)
    consts = _aug_constants()

    grp = lambda i: w_in[:, :, i * D_GRP:(i + 1) * D_GRP]
    t = lambda w: jnp.swapaxes(w, 1, 2)
    wq = jnp.stack([t(grp(0)) * (0.5 * Q_SCALE), t(grp(3)) * (LOG2E * Q_SCALE)], axis=1).astype(BF16)
    wk = jnp.stack([grp(1), grp(4)], axis=1).astype(BF16)
    wv = jnp.stack([t(grp(2)), t(grp(5))], axis=1).astype(BF16)
    wg = t(w_in[:, :, 6 * D_GRP:8 * D_GRP]).astype(BF16)
    wf = jnp.zeros((depth, FORGET_ROWS, d), BF16).at[:, :N_HEADS].set(t(w_in[:, :, 8 * D_GRP:]).astype(BF16))
    bf = jnp.zeros((depth, FORGET_ROWS, LANES), F32).at[:, :N_HEADS].set(b_f[:, :, None])
    wout = (g_grp[:, :, None] * w_out).astype(BF16)

    ada = _adaln(c, w_ada, b_ada)
    for l in range(depth):
        q_sb, k_sb, v_sb, q_fx, k_fx, v_fx, gate_t, qaug, kaug = _project(
            x, ada[l], g_norm[l], wq[l], wk[l], wv[l], wg[l], wf[l], bf[l], consts, ts)
        o_sb = _sb_attention(q_sb, k_sb, v_sb, tq)
        o_fx = _fox_attention(q_fx, qaug, k_fx, kaug, v_fx, tq)
        x = _output(o_sb, o_fx, gate_t, x, wout[l], ada[l], g_final, min(2 * ts, s), l == depth - 1)
    return x
```
